```python
import jax, jax.numpy as jnp
from jax import lax
import numpy as np

D_MODEL = 4096
BATCH = 4
SEQ = 2048
DEPTH = 2

CTX_LEN = 256
GRID_W = 64
HEAD_DIM = 128
ROPE_THETA = 10000.0
AXIS_DIM = HEAD_DIM // 2
Q_BLOCK = 128
NORM_EPS = 1e-6

D_ATTN = D_MODEL // 4
N_Q_HEADS = D_ATTN // HEAD_DIM
N_KV_HEADS = N_Q_HEADS // 4
GQA_GROUP = N_Q_HEADS // N_KV_HEADS
KV_W = N_KV_HEADS * HEAD_DIM

D_FOURIER = D_MODEL // 4
N_FOURIER_GROUPS = 4
FOURIER_GROUP = D_FOURIER // N_FOURIER_GROUPS

D_CONV = D_MODEL // 4
CONV_WIDTH = 31
CONV_PAD = CONV_WIDTH // 2

D_SGU = D_MODEL // 4
N_SGU_HEADS = D_SGU // HEAD_DIM
SGU_HEAD = D_SGU // N_SGU_HEADS
SGU_CHUNK = 128

D_MIX = D_ATTN + D_FOURIER + D_CONV + D_SGU

Q_END = D_ATTN
K_END = Q_END + KV_W
V_END = K_END + KV_W
F_END = V_END + D_FOURIER
C_END = F_END + 2 * D_CONV
D_IN = C_END + 2 * D_SGU

N_MOD = 6
N_EXPERTS = 64
EXPERT_HIDDEN = D_MODEL // 16
SHARED_HIDDEN = EXPERT_HIDDEN
TOP_K = 8
N_EXPERT_GROUPS = 8
TOPK_GROUPS = 4
EXPERTS_PER_GROUP = N_EXPERTS // N_EXPERT_GROUPS
ROUTED_SCALE = 2.5

kernel_name = "hybrid_parallel_group_flow_block"


def rmsnorm(t, g):
    tf = t.astype(jnp.float32)
    out = tf * lax.rsqrt(jnp.mean(tf * tf, axis=-1, keepdims=True) + NORM_EPS)
    return (out * g.astype(jnp.float32)).astype(t.dtype)


def layernorm(t, g, b):
    tf = t.astype(jnp.float32)
    mu = jnp.mean(tf, axis=-1, keepdims=True)
    var = jnp.mean(jnp.square(tf - mu), axis=-1, keepdims=True)
    out = (tf - mu) * lax.rsqrt(var + NORM_EPS)
    return (out * g.astype(jnp.float32) + b.astype(jnp.float32)).astype(t.dtype)


def modulation(cond, w_ada, b_ada):
    m = jax.nn.silu(cond) @ w_ada + b_ada
    return jnp.split(m, N_MOD, axis=-1)


def rope_2d(t):
    L = t.shape[1]
    rows = L // GRID_W
    row = jnp.repeat(jnp.arange(rows), GRID_W).astype(jnp.float32)
    col = jnp.tile(jnp.arange(GRID_W), rows).astype(jnp.float32)
    inv = ROPE_THETA ** (-jnp.arange(0, AXIS_DIM, 2, dtype=jnp.float32) / AXIS_DIM)
    ang = jnp.concatenate([row[:, None] * inv, col[:, None] * inv], axis=-1)
    cos = jnp.cos(ang)[None, :, None, :]
    sin = jnp.sin(ang)[None, :, None, :]
    tf = t.astype(jnp.float32).reshape(*t.shape[:-1], HEAD_DIM // 2, 2)
    t0, t1 = tf[..., 0], tf[..., 1]
    out = jnp.stack([t0 * cos - t1 * sin, t0 * sin + t1 * cos], axis=-1).reshape(t.shape)
    return out.astype(t.dtype)


def split_proj(z):
    B, L, _ = z.shape
    q = z[..., :Q_END].reshape(B, L, N_Q_HEADS, HEAD_DIM)
    k = z[..., Q_END:K_END].reshape(B, L, N_KV_HEADS, HEAD_DIM)
    v = z[..., K_END:V_END].reshape(B, L, N_KV_HEADS, HEAD_DIM)
    return q, k, v, z[..., V_END:F_END], z[..., F_END:C_END], z[..., C_END:]


def blocked_attention(q, k, v):
    B, Lq = q.shape[:2]
    nb = Lq // Q_BLOCK
    qb = q.reshape(B, nb, Q_BLOCK, N_KV_HEADS, GQA_GROUP, HEAD_DIM).transpose(1, 0, 2, 3, 4, 5)
    scale = HEAD_DIM ** -0.5

    def one_block(qblk):
        s = jnp.einsum('bqgrd,bkgd->bgrqk', qblk, k, preferred_element_type=jnp.float32) * scale
        p = jax.nn.softmax(s, axis=-1)
        return jnp.einsum('bgrqk,bkgd->bqgrd', p.astype(v.dtype), v)

    o = lax.map(one_block, qb)
    return o.transpose(1, 0, 2, 3, 4, 5).reshape(B, Lq, N_Q_HEADS * HEAD_DIM)


def fourier_mixer(zf, w_f):
    B, L, _ = zf.shape
    t = zf.astype(jnp.float32).reshape(B, L, N_FOURIER_GROUPS, FOURIER_GROUP)
    t = jnp.fft.fft2(t, axes=(1, 3), norm='ortho').real
    y = jnp.einsum('blgc,gcd->blgd', t, w_f.astype(jnp.float32))
    return y.reshape(B, L, D_FOURIER).astype(zf.dtype)


def conv_module(zc, w_dw, b_dw, g_ln, b_ln, w_pw, b_pw):
    h = zc[..., :D_CONV] * jax.nn.sigmoid(zc[..., D_CONV:])
    h = lax.conv_general_dilated(h, w_dw[:, None, :], window_strides=(1,),
                                 padding=[(CONV_PAD, CONV_PAD)],
                                 dimension_numbers=('NWC', 'WIO', 'NWC'),
                                 feature_group_count=D_CONV) + b_dw
    h = jax.nn.silu(layernorm(h, g_ln, b_ln))
    return h @ w_pw + b_pw


def sgu_mixer(zs, g_sgu, w_s, b_s):
    B, L, _ = zs.shape
    a = jax.nn.gelu(zs)
    u, v = a[..., :D_SGU], a[..., D_SGU:]
    v = rmsnorm(v.reshape(B, L, N_SGU_HEADS, SGU_HEAD), g_sgu.reshape(N_SGU_HEADS, SGU_HEAD))
    v = v.reshape(B, L // SGU_CHUNK, SGU_CHUNK, N_SGU_HEADS, SGU_HEAD)
    s = jnp.einsum('hpq,bnqhc->bnphc', w_s, v) + b_s.T[None, None, :, :, None]
    return u * s.reshape(B, L, D_SGU)


def moe_ffn(t, w_router, b_router, w_ge, w_ue, w_de, w_gs, w_us, w_ds):
    T = t.shape[0]
    s = jax.nn.sigmoid((t @ w_router).astype(jnp.float32))
    sel = s + b_router.astype(jnp.float32)
    grp = sel.reshape(T, N_EXPERT_GROUPS, EXPERTS_PER_GROUP)
    gscore = lax.top_k(grp, 2)[0].sum(-1)
    _, gidx = lax.top_k(gscore, TOPK_GROUPS)
    gmask = jax.nn.one_hot(gidx, N_EXPERT_GROUPS).sum(-2) > 0
    emask = jnp.repeat(gmask, EXPERTS_PER_GROUP, axis=-1)
    _, eidx = lax.top_k(jnp.where(emask, sel, -jnp.inf), TOP_K)
    w_sel = jnp.take_along_axis(s, eidx, axis=-1)
    w_sel = w_sel / jnp.sum(w_sel, axis=-1, keepdims=True) * ROUTED_SCALE
    gates = jnp.sum(jax.nn.one_hot(eidx, N_EXPERTS, dtype=jnp.float32) * w_sel[..., None], axis=-2)
    hg = jnp.einsum('td,edf->tef', t, w_ge)
    hu = jnp.einsum('td,edf->tef', t, w_ue)
    hh = jax.nn.silu(hg) * hu * gates[..., None].astype(t.dtype)
    routed = jnp.einsum('tef,efd->td', hh, w_de)
    shared = (jax.nn.silu(t @ w_gs) * (t @ w_us)) @ w_ds
    return routed + shared


def setup_inputs(seed: int = 0) -> dict:
    key = jax.random.key(seed)
    ks = jax.random.split(key, 32)
    f32 = jnp.float32
    nrm = lambda k, shape, s: jax.random.normal(k, shape, f32) * s
    gain = lambda k, shape: 1.0 + 0.02 * jax.random.normal(k, shape, f32)
    return {
        "x": nrm(ks[0], (BATCH, SEQ, D_MODEL), 1.0),
        "c": nrm(ks[1], (BATCH, D_MODEL), 1.0),
        "ctx": nrm(ks[2], (BATCH, CTX_LEN, D_MODEL), 1.0),
        "c_ctx": nrm(ks[3], (D_MODEL,), 1.0),
        "w_ada": nrm(ks[4], (DEPTH, D_MODEL, N_MOD * D_MODEL), 0.5 * D_MODEL ** -0.5),
        "b_ada": nrm(ks[5], (DEPTH, N_MOD * D_MODEL), 0.02),
        "g_norm1": gain(ks[6], (DEPTH, D_MODEL)),
        "g_norm2": gain(ks[7], (DEPTH, D_MODEL)),
        "w_in": nrm(ks[8], (DEPTH, D_MODEL, D_IN), D_MODEL ** -0.5),
        "g_q": gain(ks[9], (DEPTH, HEAD_DIM)),
        "g_k": gain(ks[10], (DEPTH, HEAD_DIM)),
        "w_fourier": nrm(ks[11], (DEPTH, N_FOURIER_GROUPS, FOURIER_GROUP, FOURIER_GROUP), FOURIER_GROUP ** -0.5),
        "w_dw": nrm(ks[12], (DEPTH, CONV_WIDTH, D_CONV), CONV_WIDTH ** -0.5),
        "b_dw": nrm(ks[13], (DEPTH, D_CONV), 0.02),
        "g_conv_ln": gain(ks[14], (DEPTH, D_CONV)),
        "b_conv_ln": nrm(ks[15], (DEPTH, D_CONV), 0.02),
        "w_pw": nrm(ks[16], (DEPTH, D_CONV, D_CONV), D_CONV ** -0.5),
        "b_pw": nrm(ks[17], (DEPTH, D_CONV), 0.02),
        "g_sgu": gain(ks[18], (DEPTH, D_SGU)),
        "w_spatial": nrm(ks[19], (DEPTH, N_SGU_HEADS, SGU_CHUNK, SGU_CHUNK), SGU_CHUNK ** -0.5),
        "b_spatial": gain(ks[20], (DEPTH, N_SGU_HEADS, SGU_CHUNK)),
        "w_out": nrm(ks[21], (DEPTH, D_MIX, D_MODEL), D_MIX ** -0.5),
        "w_router": nrm(ks[22], (DEPTH, D_MODEL, N_EXPERTS), D_MODEL ** -0.5),
        "b_router": nrm(ks[23], (DEPTH, N_EXPERTS), 0.01),
        "w_gate_e": nrm(ks[24], (DEPTH, N_EXPERTS, D_MODEL, EXPERT_HIDDEN), D_MODEL ** -0.5),
        "w_up_e": nrm(ks[25], (DEPTH, N_EXPERTS, D_MODEL, EXPERT_HIDDEN), D_MODEL ** -0.5),
        "w_down_e": nrm(ks[26], (DEPTH, N_EXPERTS, EXPERT_HIDDEN, D_MODEL), EXPERT_HIDDEN ** -0.5),
        "w_gate_s": nrm(ks[27], (DEPTH, D_MODEL, SHARED_HIDDEN), D_MODEL ** -0.5),
        "w_up_s": nrm(ks[28], (DEPTH, D_MODEL, SHARED_HIDDEN), D_MODEL ** -0.5),
        "w_down_s": nrm(ks[29], (DEPTH, SHARED_HIDDEN, D_MODEL), SHARED_HIDDEN ** -0.5),
    }


def reference(x, c, ctx, c_ctx, w_ada, b_ada, g_norm1, g_norm2, w_in, g_q, g_k, w_fourier,
              w_dw, b_dw, g_conv_ln, b_conv_ln, w_pw, b_pw, g_sgu, w_spatial, b_spatial, w_out,
              w_router, b_router, w_gate_e, w_up_e, w_down_e, w_gate_s, w_up_s, w_down_s):
    B, L, D = x.shape
    Lc = ctx.shape[1]
    for l in range(DEPTH):
        last = l == DEPTH - 1
        sh1, sc1, ga1, sh2, sc2, ga2 = modulation(c, w_ada[l], b_ada[l])
        csh1, csc1, cga1, csh2, csc2, cga2 = modulation(c_ctx, w_ada[l], b_ada[l])

        h = rmsnorm(x, g_norm1[l]) * (1 + sc1[:, None]) + sh1[:, None]
        hc = rmsnorm(ctx, g_norm1[l]) * (1 + csc1) + csh1
        q, k, v, zf, zc, zs = split_proj(h @ w_in[l])
        q = rope_2d(rmsnorm(q, g_q[l]))
        k = rope_2d(rmsnorm(k, g_k[l]))
        if last:
            zkv = hc @ w_in[l][:, Q_END:V_END]
            k_c = rmsnorm(zkv[..., :KV_W].reshape(B, Lc, N_KV_HEADS, HEAD_DIM), g_k[l])
            v_c = zkv[..., KV_W:].reshape(B, Lc, N_KV_HEADS, HEAD_DIM)
        else:
            q_c, k_c, v_c, zf_c, zc_c, zs_c = split_proj(hc @ w_in[l])
            q_c = rmsnorm(q_c, g_q[l])
            k_c = rmsnorm(k_c, g_k[l])
        k_all = jnp.concatenate([k, k_c], axis=1)
        v_all = jnp.concatenate([v, v_c], axis=1)
        y = jnp.concatenate([
            blocked_attention(q, k_all, v_all),
            fourier_mixer(zf, w_fourier[l]),
            conv_module(zc, w_dw[l], b_dw[l], g_conv_ln[l], b_conv_ln[l], w_pw[l], b_pw[l]),
            sgu_mixer(zs, g_sgu[l], w_spatial[l], b_spatial[l]),
        ], axis=-1) @ w_out[l]
        x = x + ga1[:, None] * y
        if not last:
            yc = jnp.concatenate([
                blocked_attention(q_c, k_c, v_c),
                fourier_mixer(zf_c, w_fourier[l]),
                conv_module(zc_c, w_dw[l], b_dw[l], g_conv_ln[l], b_conv_ln[l], w_pw[l], b_pw[l]),
                sgu_mixer(zs_c, g_sgu[l], w_spatial[l], b_spatial[l]),
            ], axis=-1) @ w_out[l]
            ctx = ctx + cga1 * yc

        h2 = rmsnorm(x, g_norm2[l]) * (1 + sc2[:, None]) + sh2[:, None]
        tokens = h2.reshape(B * L, D)
        if not last:
            h2c = rmsnorm(ctx, g_norm2[l]) * (1 + csc2) + csh2
            tokens = jnp.concatenate([tokens, h2c.reshape(B * Lc, D)], axis=0)
        f = moe_ffn(tokens, w_router[l], b_router[l], w_gate_e[l], w_up_e[l], w_down_e[l],
                    w_gate_s[l], w_up_s[l], w_down_s[l])
        x = x + ga2[:, None] * f[:B * L].reshape(B, L, D)
        if not last:
            ctx = ctx + cga2 * f[B * L:].reshape(B, Lc, D)
    return x
```

```python
import functools

import jax
import jax.numpy as jnp
from jax import lax
from jax.experimental import pallas as pl
from jax.experimental.pallas import tpu as pltpu

F32 = jnp.float32
BF16 = jnp.bfloat16
I32 = jnp.int32

HEAD_DIM = 128
GRID_W = 64
ROPE_THETA = 10000.0
NORM_EPS = 1e-6
GQA_GROUP = 4
N_FOURIER_GROUPS = 4
CONV_WIDTH = 31
CONV_PAD = CONV_WIDTH // 2
CONV_HALO = 16
SGU_CHUNK = 128
N_MOD = 6
TOP_K = 8
EXPERTS_PER_GROUP = 8
TOPK_GROUPS = 4
ROUTED_SCALE = 2.5
COND_ROWS = 8

VMEM_LIMIT = 56 * 1024 * 1024


def _cp(sem, vmem=VMEM_LIMIT, **kw):
    return pltpu.CompilerParams(dimension_semantics=sem, vmem_limit_bytes=vmem, **kw)


def _pick(n, pref, mult=8):
    if n <= pref:
        return n
    for t in range(pref - pref % mult, 0, -mult):
        if n % t == 0:
            return t
    raise ValueError(f"no tile for {n} <= {pref}")


def _rms(t, eps=NORM_EPS):
    return t * lax.rsqrt(jnp.mean(t * t, axis=-1, keepdims=True) + eps)


def _cast_kernel(x_ref, o_ref):
    o_ref[...] = x_ref[...].astype(o_ref.dtype)


def cast_cols(w, l, col0, ncols, cb):
    _, R, _ = w.shape
    tr = _pick(R, 512)
    return pl.pallas_call(
        _cast_kernel,
        grid=(R // tr, ncols // cb),
        in_specs=[pl.BlockSpec((None, tr, cb), lambda i, j: (l, i, col0 // cb + j))],
        out_specs=pl.BlockSpec((tr, cb), lambda i, j: (i, j)),
        out_shape=jax.ShapeDtypeStruct((R, ncols), BF16),
        compiler_params=_cp(("parallel", "parallel")),
        name="cast_bf16",
    )(w)


def _modulation_kernel(c_ref, w_ref, b_ref, o_ref):
    s = jax.nn.silu(c_ref[...]).astype(BF16)
    o_ref[...] = jnp.dot(s, w_ref[...].astype(BF16), preferred_element_type=F32) + b_ref[...]


def modulation(cond, w_ada, b_ada):
    depth, D, N = w_ada.shape
    tn = _pick(N, 512, 128)
    return pl.pallas_call(
        _modulation_kernel,
        grid=(depth, N // tn),
        in_specs=[
            pl.BlockSpec((COND_ROWS, D), lambda l, j: (0, 0)),
            pl.BlockSpec((None, D, tn), lambda l, j: (l, 0, j)),
            pl.BlockSpec((None, 1, tn), lambda l, j: (l, 0, j)),
        ],
        out_specs=pl.BlockSpec((None, COND_ROWS, tn), lambda l, j: (l, 0, j)),
        out_shape=jax.ShapeDtypeStruct((depth, COND_ROWS, N), F32),
        compiler_params=_cp(("parallel", "parallel")),
        name="modulation",
    )(cond, w_ada, b_ada.reshape(depth, 1, N))


def _cond_row(i, tm, L, B):
    return jnp.minimum((i * tm) // L, B)


def _norm_mod_kernel(x_ref, g_ref, sh_ref, sc_ref, o_ref, *, tm, L, B):
    r = _cond_row(pl.program_id(0), tm, L, B)
    xn = _rms(x_ref[...]) * g_ref[...]
    o_ref[...] = (xn * (1.0 + sc_ref[pl.ds(r, 1), :]) + sh_ref[pl.ds(r, 1), :]).astype(o_ref.dtype)


def norm_mod(x, g, sh, sc, *, L, B):
    T, D = x.shape
    tm = _pick(L, 512)
    return pl.pallas_call(
        functools.partial(_norm_mod_kernel, tm=tm, L=L, B=B),
        grid=(T // tm,),
        in_specs=[
            pl.BlockSpec((tm, D), lambda i: (i, 0)),
            pl.BlockSpec((1, D), lambda i: (0, 0)),
            pl.BlockSpec((COND_ROWS, D), lambda i: (0, 0)),
            pl.BlockSpec((COND_ROWS, D), lambda i: (0, 0)),
        ],
        out_specs=pl.BlockSpec((tm, D), lambda i: (i, 0)),
        out_shape=jax.ShapeDtypeStruct((T, D), BF16),
        compiler_params=_cp(("parallel",)),
        name="norm_mod",
    )(x, g.reshape(1, D), sh, sc)


def _mm_kernel(a_ref, w_ref, o_ref):
    o_ref[...] = jnp.dot(a_ref[...], w_ref[...], preferred_element_type=F32).astype(o_ref.dtype)


def matmul(a, w, n_rows, out_dtype=BF16):
    _, K = a.shape
    N = w.shape[1]
    tm = _pick(n_rows, 512)
    tn = _pick(N, 1024, 128)
    return pl.pallas_call(
        _mm_kernel,
        grid=(N // tn, n_rows // tm),
        in_specs=[
            pl.BlockSpec((tm, K), lambda j, i: (i, 0)),
            pl.BlockSpec((K, tn), lambda j, i: (0, j)),
        ],
        out_specs=pl.BlockSpec((tm, tn), lambda j, i: (i, j)),
        out_shape=jax.ShapeDtypeStruct((n_rows, N), out_dtype),
        compiler_params=_cp(("parallel", "parallel")),
        name="in_proj",
    )(a, w)


def _rope_tables(L, tm):
    rows = L // GRID_W
    row = jnp.repeat(jnp.arange(rows), GRID_W).astype(F32)
    col = jnp.tile(jnp.arange(GRID_W), rows).astype(F32)
    axis_dim = HEAD_DIM // 2
    inv = ROPE_THETA ** (-jnp.arange(0, axis_dim, 2, dtype=F32) / axis_dim)
    ang = jnp.concatenate([row[:, None] * inv, col[:, None] * inv], axis=-1)
    cos = jnp.repeat(jnp.cos(ang), 2, axis=-1)
    sin = jnp.repeat(jnp.sin(ang), 2, axis=-1)
    sign = jnp.tile(jnp.array([-1.0, 1.0], F32), HEAD_DIM // 2)
    cos = jnp.concatenate([cos, jnp.ones((tm, HEAD_DIM), F32)], axis=0)
    sin = jnp.concatenate([sin * sign, jnp.zeros((tm, HEAD_DIM), F32)], axis=0)
    return cos, sin


def _qk_kernel(q_ref, k_ref, cos_ref, sin_ref, gq_ref, gk_ref, qo_ref, ko_ref, *, n_q, n_kv):
    cos = cos_ref[...]
    sin = sin_ref[...]
    even = (lax.broadcasted_iota(I32, cos.shape, 1) & 1) == 0

    def norm_rope(t, g):
        n = _rms(t.astype(F32)) * g
        partner = jnp.where(even, pltpu.roll(n, HEAD_DIM - 1, 1), pltpu.roll(n, 1, 1))
        return n * cos + partner * sin

    q_scale = HEAD_DIM ** -0.5
    for h in range(n_q):
        sl = slice(h * HEAD_DIM, (h + 1) * HEAD_DIM)
        qo_ref[:, sl] = (norm_rope(q_ref[:, sl], gq_ref[...]) * q_scale).astype(qo_ref.dtype)
    for h in range(n_kv):
        sl = slice(h * HEAD_DIM, (h + 1) * HEAD_DIM)
        ko_ref[:, sl] = norm_rope(k_ref[:, sl], gk_ref[...]).astype(ko_ref.dtype)


def qk_norm_rope(z_qkv, g_q, g_k, *, L, B, d_attn, kv_w):
    T = z_qkv.shape[0]
    tm = _pick(L, 256)
    cos, sin = _rope_tables(L, tm)
    n_lat = B * L // tm
    per_seq = L // tm

    def tab_idx(i):
        return (jnp.where(i < n_lat, i % per_seq, per_seq), 0)

    return pl.pallas_call(
        functools.partial(_qk_kernel, n_q=d_attn // HEAD_DIM, n_kv=kv_w // HEAD_DIM),
        grid=(T // tm,),
        in_specs=[
            pl.BlockSpec((tm, d_attn), lambda i: (i, 0)),
            pl.BlockSpec((tm, kv_w), lambda i: (i, d_attn // kv_w)),
            pl.BlockSpec((tm, HEAD_DIM), tab_idx),
            pl.BlockSpec((tm, HEAD_DIM), tab_idx),
            pl.BlockSpec((1, HEAD_DIM), lambda i: (0, 0)),
            pl.BlockSpec((1, HEAD_DIM), lambda i: (0, 0)),
        ],
        out_specs=[
            pl.BlockSpec((tm, d_attn), lambda i: (i, 0)),
            pl.BlockSpec((tm, kv_w), lambda i: (i, 0)),
        ],
        out_shape=[
            jax.ShapeDtypeStruct((T, d_attn), BF16),
            jax.ShapeDtypeStruct((T, kv_w), BF16),
        ],
        compiler_params=_cp(("parallel",)),
        name="qk_norm_rope",
    )(z_qkv, z_qkv, cos, sin, g_q.reshape(1, HEAD_DIM), g_k.reshape(1, HEAD_DIM))


def _attn_kernel(*refs, n_seg):
    q_ref = refs[0]
    k_refs = refs[1:1 + n_seg]
    v_refs = refs[1 + n_seg:1 + 2 * n_seg]
    o_ref = refs[-1]
    for h in range(GQA_GROUP):
        sl = slice(h * HEAD_DIM, (h + 1) * HEAD_DIM)
        q = q_ref[:, sl]
        s = [lax.dot_general(q, k[...], (((1,), (1,)), ((), ())), preferred_element_type=F32)
             for k in k_refs]
        m = functools.reduce(jnp.maximum, [jnp.max(x, axis=-1, keepdims=True) for x in s])
        p = [jnp.exp(x - m) for x in s]
        den = functools.reduce(jnp.add, [jnp.sum(x, axis=-1, keepdims=True) for x in p])
        o = functools.reduce(jnp.add, [jnp.dot(x.astype(BF16), v[...], preferred_element_type=F32)
                                       for x, v in zip(p, v_refs)])
        o_ref[:, sl] = (o / den).astype(o_ref.dtype)


def attention(qn, kn, z_qkv, y_prev, *, q_row0, Lq, segs, B, d_attn, kv_w):
    T = qn.shape[0]
    n_kv = kv_w // HEAD_DIM
    gw = GQA_GROUP * HEAD_DIM
    tq = _pick(Lq, 512)
    nq = Lq // tq
    v_col0 = (d_attn + kv_w) // HEAD_DIM

    def q_idx(b, g, qi):
        return (q_row0 // tq + b * nq + qi, g)

    in_specs = [pl.BlockSpec((tq, gw), q_idx)]
    args = [qn]
    for row0, ln in segs:
        in_specs.append(pl.BlockSpec((ln, HEAD_DIM), lambda b, g, qi, row0=row0, ln=ln: (row0 // ln + b, g)))
        args.append(kn)
    for row0, ln in segs:
        in_specs.append(
            pl.BlockSpec((ln, HEAD_DIM), lambda b, g, qi, row0=row0, ln=ln: (row0 // ln + b, v_col0 + g)))
        args.append(z_qkv)
    aliases = {}
    if y_prev is not None:
        in_specs.append(pl.BlockSpec(memory_space=pl.ANY))
        args.append(y_prev)
        aliases = {len(args) - 1: 0}
    kern = functools.partial(_attn_kernel, n_seg=len(segs))
    if y_prev is not None:
        kern = functools.partial(_drop_ref, kern, len(args) - 1)
    return pl.pallas_call(
        kern,
        grid=(B, n_kv, nq),
        in_specs=in_specs,
        out_specs=pl.BlockSpec((tq, gw), q_idx),
        out_shape=jax.ShapeDtypeStruct((T, d_attn), BF16),
        input_output_aliases=aliases,
        compiler_params=_cp(("parallel", "parallel", "parallel")),
        name="attention",
    )(*args)


def _drop_ref(kern, pos, *refs):
    return kern(*refs[:pos], *refs[pos + 1:])


def _dft_tables(n):
    k = jnp.arange(n, dtype=I32)
    ang = ((k[:, None] * k[None, :]) % n).astype(F32) * (2.0 * jnp.pi / n)
    return jnp.cos(ang), jnp.sin(ang)


def _fourier_kernel(t_ref, cl_ref, sl_ref, cc_ref, sc_ref, w_ref, o_ref, *, scale):
    w = w_ref[...]
    hi = lax.Precision.HIGHEST
    mc = jnp.dot(cc_ref[...], w, precision=hi, preferred_element_type=F32).astype(BF16)
    ms = jnp.dot(sc_ref[...], w, precision=hi, preferred_element_type=F32).astype(BF16)
    t = t_ref[...]
    bc = jnp.dot(t, mc, preferred_element_type=F32).astype(BF16)
    bs = jnp.dot(t, ms, preferred_element_type=F32).astype(BF16)
    y = (jnp.dot(cl_ref[...], bc, preferred_element_type=F32)
         - jnp.dot(sl_ref[...], bs, preferred_element_type=F32))
    o_ref[...] = (y * scale).astype(o_ref.dtype)


def fourier(z_f, w_f, y_prev, *, row0, Ls, B, n_out_rows):
    ng, fg, _ = w_f.shape
    cl, sl = _dft_tables(Ls)
    cc, sc = _dft_tables(fg)
    scale = 1.0 / float(Ls * fg) ** 0.5

    def idx(b, g):
        return (row0 // Ls + b, g)

    in_specs = [
        pl.BlockSpec((Ls, fg), idx),
        pl.BlockSpec((Ls, Ls), lambda b, g: (0, 0)),
        pl.BlockSpec((Ls, Ls), lambda b, g: (0, 0)),
        pl.BlockSpec((fg, fg), lambda b, g: (0, 0)),
        pl.BlockSpec((fg, fg), lambda b, g: (0, 0)),
        pl.BlockSpec((None, fg, fg), lambda b, g: (g, 0, 0)),
    ]
    args = [z_f, cl.astype(BF16), sl.astype(BF16), cc, sc, w_f]
    kern = functools.partial(_fourier_kernel, scale=scale)
    aliases = {}
    if y_prev is not None:
        in_specs.append(pl.BlockSpec(memory_space=pl.ANY))
        args.append(y_prev)
        aliases = {len(args) - 1: 0}
        kern = functools.partial(_drop_ref, kern, len(args) - 1)
    return pl.pallas_call(
        kern,
        grid=(B, ng),
        in_specs=in_specs,
        out_specs=pl.BlockSpec((Ls, fg), idx),
        out_shape=jax.ShapeDtypeStruct((n_out_rows, ng * fg), BF16),
        input_output_aliases=aliases,
        compiler_params=_cp(("parallel", "parallel")),
        name="fourier",
    )(*args)


def _conv_kernel(cur_ref, prev_ref, next_ref, wdw_ref, bdw_ref, gln_ref, bln_ref, wpw_ref, bpw_ref,
                 o_ref, gbuf, *, tr, dc, n_lat, per_lat, per_ctx):
    i = pl.program_id(0)
    j = jnp.where(i < n_lat, i % per_lat, (i - n_lat) % per_ctx)
    per = jnp.where(i < n_lat, per_lat, per_ctx)
    first = j == 0
    last = j == per - 1

    def glu(zb):
        return zb[:, :dc].astype(F32) * jax.nn.sigmoid(zb[:, dc:].astype(F32))

    gbuf[0:CONV_HALO, :] = jnp.where(first, 0.0, glu(prev_ref[...]))
    gbuf[CONV_HALO:CONV_HALO + tr, :] = glu(cur_ref[...])
    gbuf[CONV_HALO + tr:, :] = jnp.where(last, 0.0, glu(next_ref[...]))
    acc = jnp.broadcast_to(bdw_ref[...], (tr, dc))
    base = CONV_HALO - CONV_PAD
    for k in range(CONV_WIDTH):
        acc = acc + gbuf[base + k:base + k + tr, :] * wdw_ref[k:k + 1, :]
    mu = jnp.mean(acc, axis=-1, keepdims=True)
    cen = acc - mu
    var = jnp.mean(cen * cen, axis=-1, keepdims=True)
    hn = cen * lax.rsqrt(var + NORM_EPS) * gln_ref[...] + bln_ref[...]
    act = jax.nn.silu(hn).astype(BF16)
    o_ref[...] = (jnp.dot(act, wpw_ref[...], preferred_element_type=F32) + bpw_ref[...]).astype(o_ref.dtype)


def conv_module(z_c, w_dw, b_dw, g_ln, b_ln, w_pw_bf, b_pw, *, L, Lc, B, n_rows):
    dc = w_pw_bf.shape[0]
    tr = _pick(Lc, 256, CONV_HALO)
    hpt = tr // CONV_HALO
    n_halo = z_c.shape[0] // CONV_HALO
    w_pad = jnp.concatenate([w_dw, jnp.zeros((32 - CONV_WIDTH, dc), F32)], axis=0)
    row = lambda v: v.reshape(1, dc)
    return pl.pallas_call(
        functools.partial(_conv_kernel, tr=tr, dc=dc, n_lat=B * L // tr, per_lat=L // tr, per_ctx=Lc // tr),
        grid=(n_rows // tr,),
        in_specs=[
            pl.BlockSpec((tr, 2 * dc), lambda i: (i, 0)),
            pl.BlockSpec((CONV_HALO, 2 * dc), lambda i: (jnp.maximum(i * hpt - 1, 0), 0)),
            pl.BlockSpec((CONV_HALO, 2 * dc), lambda i: (jnp.minimum((i + 1) * hpt, n_halo - 1), 0)),
            pl.BlockSpec((32, dc), lambda i: (0, 0)),
            pl.BlockSpec((1, dc), lambda i: (0, 0)),
            pl.BlockSpec((1, dc), lambda i: (0, 0)),
            pl.BlockSpec((1, dc), lambda i: (0, 0)),
            pl.BlockSpec((dc, dc), lambda i: (0, 0)),
            pl.BlockSpec((1, dc), lambda i: (0, 0)),
        ],
        out_specs=pl.BlockSpec((tr, dc), lambda i: (i, 0)),
        out_shape=jax.ShapeDtypeStruct((n_rows, dc), BF16),
        scratch_shapes=[pltpu.VMEM((tr + 2 * CONV_HALO, dc), F32)],
        compiler_params=_cp(("parallel",)),
        name="conv_module",
    )(z_c, z_c, z_c, w_pad, row(b_dw), row(g_ln), row(b_ln), w_pw_bf, row(b_pw))


def _sgu_kernel(z_ref, g_ref, ws_ref, bs_ref, o_ref, *, tm, ds, n_heads):
    for h in range(n_heads):
        sl = slice(h * HEAD_DIM, (h + 1) * HEAD_DIM)
        u = jax.nn.gelu(z_ref[:, sl].astype(F32))
        v = jax.nn.gelu(z_ref[:, ds + h * HEAD_DIM:ds + (h + 1) * HEAD_DIM].astype(F32))
        vn = (_rms(v) * g_ref[:, sl]).astype(BF16)
        w = ws_ref[h].astype(BF16)
        b = bs_ref[:, h:h + 1]
        for c in range(tm // SGU_CHUNK):
            rows = slice(c * SGU_CHUNK, (c + 1) * SGU_CHUNK)
            s = jnp.dot(w, vn[rows, :], preferred_element_type=F32) + b
            o_ref[rows, sl] = (u[rows, :] * s).astype(o_ref.dtype)


def sgu_mixer(z_s, g_sgu, w_s, b_s, *, n_rows):
    n_heads = w_s.shape[0]
    ds = n_heads * HEAD_DIM
    tm = _pick(n_rows, 256, SGU_CHUNK)
    return pl.pallas_call(
        functools.partial(_sgu_kernel, tm=tm, ds=ds, n_heads=n_heads),
        grid=(n_rows // tm,),
        in_specs=[
            pl.BlockSpec((tm, 2 * ds), lambda i: (i, 0)),
            pl.BlockSpec((1, ds), lambda i: (0, 0)),
            pl.BlockSpec((n_heads, SGU_CHUNK, SGU_CHUNK), lambda i: (0, 0, 0)),
            pl.BlockSpec((SGU_CHUNK, n_heads), lambda i: (0, 0)),
        ],
        out_specs=pl.BlockSpec((tm, ds), lambda i: (i, 0)),
        out_shape=jax.ShapeDtypeStruct((n_rows, ds), BF16),
        compiler_params=_cp(("parallel",)),
        name="sgu_mixer",
    )(z_s, g_sgu.reshape(1, ds), w_s, b_s.T)


def _out_proj_kernel(ya_ref, yf_ref, yc_ref, ys_ref, w_ref, x_ref, ga_ref, o_ref, *, tm, L, B, widths):
    r = _cond_row(pl.program_id(1), tm, L, B)
    acc = None
    k0 = 0
    for y_ref, wd in zip((ya_ref, yf_ref, yc_ref, ys_ref), widths):
        part = jnp.dot(y_ref[...], w_ref[k0:k0 + wd, :], preferred_element_type=F32)
        acc = part if acc is None else acc + part
        k0 += wd
    o_ref[...] = x_ref[...] + ga_ref[pl.ds(r, 1), :] * acc


def out_proj(ys, w_out_bf, x, ga, *, L, B, n_rows):
    T, D = x.shape
    widths = tuple(y.shape[1] for y in ys)
    tm = _pick(L, 512)
    tn = _pick(D, 1024, 128)
    in_specs = [pl.BlockSpec((tm, wd), lambda j, i: (i, 0)) for wd in widths]
    in_specs += [
        pl.BlockSpec((sum(widths), tn), lambda j, i: (0, j)),
        pl.BlockSpec((tm, tn), lambda j, i: (i, j)),
        pl.BlockSpec((COND_ROWS, tn), lambda j, i: (0, j)),
    ]
    return pl.pallas_call(
        functools.partial(_out_proj_kernel, tm=tm, L=L, B=B, widths=widths),
        grid=(D // tn, n_rows // tm),
        in_specs=in_specs,
        out_specs=pl.BlockSpec((tm, tn), lambda j, i: (i, j)),
        out_shape=jax.ShapeDtypeStruct((T, D), F32),
        input_output_aliases={len(ys) + 1: 0},
        compiler_params=_cp(("parallel", "parallel")),
        name="out_proj",
    )(*ys, w_out_bf, x, ga)


def _router_kernel(x_ref, g_ref, sh_ref, sc_ref, wr_ref, br_ref,
                   h2_ref, eidx_ref, wsel_ref, rank_ref, cnt_ref, *, tm, L, B, n_exp):
    i = pl.program_id(0)
    r = _cond_row(i, tm, L, B)
    xn = _rms(x_ref[...]) * g_ref[...]
    h2 = xn * (1.0 + sc_ref[pl.ds(r, 1), :]) + sh_ref[pl.ds(r, 1), :]
    h2_ref[...] = h2
    logits = lax.dot_general(wr_ref[...], h2, (((1,), (1,)), ((), ())),
                             precision=lax.Precision.HIGHEST, preferred_element_type=F32)
    s = jax.nn.sigmoid(logits)
    sel = s + br_ref[...]
    shape = (n_exp, tm)
    e_io = lax.broadcasted_iota(I32, shape, 0)
    e_f = e_io.astype(F32)
    k_io = e_io & (EXPERTS_PER_GROUP - 1)
    g_io = e_io >> 3
    n_groups = n_exp // EXPERTS_PER_GROUP
    neg = -jnp.inf

    def partner(v, d):
        up = pltpu.roll(v, n_exp - d, 0)
        dn = pltpu.roll(v, d, 0)
        return jnp.where((k_io & d) == 0, up, dn)

    def group_all(v, op):
        d = 1
        while d < EXPERTS_PER_GROUP:
            v = op(v, partner(v, d))
            d *= 2
        return v

    kf = k_io.astype(F32)
    m1 = group_all(sel, jnp.maximum)
    i1 = group_all(jnp.where(sel == m1, kf, float(EXPERTS_PER_GROUP)), jnp.minimum)
    m2 = group_all(jnp.where(kf == i1, neg, sel), jnp.maximum)
    gs = m1 + m2
    beaten = jnp.zeros(shape, I32)
    for d in range(1, n_groups):
        other = pltpu.roll(gs, n_exp - EXPERTS_PER_GROUP * d, 0)
        og = (g_io + d) & (n_groups - 1)
        beats = (other > gs) | ((other == gs) & (og < g_io))
        beaten = beaten + beats.astype(I32)
    val = jnp.where(beaten < TOPK_GROUPS, sel, neg)

    assign = jnp.zeros(shape, F32)
    w_rows = []
    idx_rows = []
    for j in range(TOP_K):
        m = jnp.max(val, axis=0, keepdims=True)
        idx = jnp.min(jnp.where(val == m, e_f, float(n_exp)), axis=0, keepdims=True)
        hit = e_f == idx
        w_rows.append(jnp.sum(jnp.where(hit, s, 0.0), axis=0, keepdims=True))
        val = jnp.where(hit, neg, val)
        assign = assign + hit.astype(F32)
        idx_rows.append(idx)
        eidx_ref[j:j + 1, :] = idx.astype(I32)
    w_sum = functools.reduce(jnp.add, w_rows)
    for j in range(TOP_K):
        wsel_ref[j:j + 1, :] = w_rows[j] / w_sum * ROUTED_SCALE

    @pl.when(i == 0)
    def _():
        cnt_ref[...] = jnp.zeros_like(cnt_ref)

    earlier = (lax.broadcasted_iota(I32, (tm, tm), 0) < lax.broadcasted_iota(I32, (tm, tm), 1)).astype(BF16)
    before = jnp.dot(assign.astype(BF16), earlier, preferred_element_type=F32) + cnt_ref[...]
    for j in range(TOP_K):
        hit = e_f == idx_rows[j]
        rank_ref[j:j + 1, :] = jnp.sum(jnp.where(hit, before, 0.0), axis=0, keepdims=True).astype(I32)
    cnt_ref[...] = cnt_ref[...] + jnp.sum(assign, axis=1, keepdims=True)


def router(x, g, sh, sc, w_router, b_router, *, L, B, n_rows):
    D = x.shape[1]
    n_exp = w_router.shape[1]
    tm = _pick(L, 512, 128)
    assert n_rows % tm == 0
    return pl.pallas_call(
        functools.partial(_router_kernel, tm=tm, L=L, B=B, n_exp=n_exp),
        grid=(n_rows // tm,),
        in_specs=[
            pl.BlockSpec((tm, D), lambda i: (i, 0)),
            pl.BlockSpec((1, D), lambda i: (0, 0)),
            pl.BlockSpec((COND_ROWS, D), lambda i: (0, 0)),
            pl.BlockSpec((COND_ROWS, D), lambda i: (0, 0)),
            pl.BlockSpec((n_exp, D), lambda i: (0, 0)),
            pl.BlockSpec((n_exp, 1), lambda i: (0, 0)),
        ],
        out_specs=[
            pl.BlockSpec((tm, D), lambda i: (i, 0)),
            pl.BlockSpec((TOP_K, tm), lambda i: (0, i)),
            pl.BlockSpec((TOP_K, tm), lambda i: (0, i)),
            pl.BlockSpec((TOP_K, tm), lambda i: (0, i)),
            pl.BlockSpec((n_exp, 1), lambda i: (0, 0)),
        ],
        out_shape=[
            jax.ShapeDtypeStruct((n_rows, D), F32),
            jax.ShapeDtypeStruct((TOP_K, n_rows), I32),
            jax.ShapeDtypeStruct((TOP_K, n_rows), F32),
            jax.ShapeDtypeStruct((TOP_K, n_rows), I32),
            jax.ShapeDtypeStruct((n_exp, 1), F32),
        ],
        compiler_params=_cp(("arbitrary",)),
        name="router",
    )(x, g.reshape(1, D), sh, sc, w_router.T, b_router.reshape(n_exp, 1))


def _row_copy(src, s, dst, d, sem):
    return pltpu.make_async_copy(src.at[pl.ds(s, 1)], dst.at[pl.ds(d, 1)], sem)


def _dispatch_kernel(pos_ref, h2_ref, xs_ref, sem, *, tm):
    t0 = pl.program_id(0) * tm

    def issue(t, carry):
        for j in range(TOP_K):
            _row_copy(h2_ref, t0 + t, xs_ref, pos_ref[j, t], sem).start()
        return carry

    lax.fori_loop(0, tm, issue, 0)

    def drain(t, carry):
        for j in range(TOP_K):
            _row_copy(h2_ref, 0, xs_ref, 0, sem).wait()
        return carry

    lax.fori_loop(0, tm, drain, 0)


def dispatch(pos, h2, n_slots):
    T, D = h2.shape
    tm = _pick(T, 512, 128)
    return pl.pallas_call(
        functools.partial(_dispatch_kernel, tm=tm),
        grid=(T // tm,),
        in_specs=[
            pl.BlockSpec((TOP_K, tm), lambda i: (0, i), memory_space=pltpu.SMEM),
            pl.BlockSpec(memory_space=pl.ANY),
        ],
        out_specs=pl.BlockSpec(memory_space=pl.ANY),
        out_shape=jax.ShapeDtypeStruct((n_slots, D), F32),
        scratch_shapes=[pltpu.SemaphoreType.DMA(())],
        compiler_params=_cp(("arbitrary",), has_side_effects=True),
        name="dispatch",
    )(pos, h2)


def _experts_kernel(gid_ref, nvalid_ref, nused_ref, xs_ref, wg_ref, wu_ref, wd_ref, o_ref,
                    wg_bf, wu_bf, wd_bf, *, tr):
    i = pl.program_id(0)
    changed = jnp.logical_or(i == 0, gid_ref[i] != gid_ref[jnp.maximum(i - 1, 0)])

    @pl.when(changed)
    def _():
        wg_bf[...] = wg_ref[...].astype(BF16)
        wu_bf[...] = wu_ref[...].astype(BF16)
        wd_bf[...] = wd_ref[...].astype(BF16)

    @pl.when(i < nused_ref[0])
    def _():
        valid = lax.broadcasted_iota(I32, (tr, 1), 0) < nvalid_ref[i]
        x = jnp.where(valid, xs_ref[...], 0.0).astype(BF16)
        hg = jnp.dot(x, wg_bf[...], preferred_element_type=F32)
        hu = jnp.dot(x, wu_bf[...], preferred_element_type=F32)
        hh = (jax.nn.silu(hg) * hu).astype(BF16)
        o_ref[...] = jnp.dot(hh, wd_bf[...], preferred_element_type=F32)


def experts(xs, gid, nvalid, nused, w_ge, w_ue, w_de, l, *, tr):
    P, D = xs.shape
    F = w_ge.shape[3]
    n_tiles = P // tr

    def row_idx(i, gid, nvalid, nused):
        return (jnp.minimum(i, nused[0] - 1), 0)

    def w_idx(i, gid, nvalid, nused):
        return (l, gid[i], 0, 0)

    return pl.pallas_call(
        functools.partial(_experts_kernel, tr=tr),
        grid_spec=pltpu.PrefetchScalarGridSpec(
            num_scalar_prefetch=3,
            grid=(n_tiles,),
            in_specs=[
                pl.BlockSpec((tr, D), row_idx),
                pl.BlockSpec((None, None, D, F), w_idx),
                pl.BlockSpec((None, None, D, F), w_idx),
                pl.BlockSpec((None, None, F, D), w_idx),
            ],
            out_specs=pl.BlockSpec((tr, D), row_idx),
            scratch_shapes=[pltpu.VMEM((D, F), BF16), pltpu.VMEM((D, F), BF16), pltpu.VMEM((F, D), BF16)],
        ),
        out_shape=jax.ShapeDtypeStruct((P, D), F32),
        compiler_params=_cp(("arbitrary",)),
        name="experts",
    )(gid, nvalid, nused, xs, w_ge, w_ue, w_de)


def _combine_kernel(pos_ref, x_ref, h2_ref, wt_ref, ga_ref, wgs_ref, wus_ref, wds_ref, ys_ref,
                    o_ref, gbuf, sems, *, tm, L, B):
    r = _cond_row(pl.program_id(0), tm, L, B)

    def issue(j, slot):
        def body(t, carry):
            _row_copy(ys_ref, pos_ref[j, t], gbuf.at[slot], t, sems.at[slot]).start()
            return carry
        lax.fori_loop(0, tm, body, 0)

    def drain(slot):
        def body(t, carry):
            _row_copy(ys_ref, 0, gbuf.at[slot], 0, sems.at[slot]).wait()
            return carry
        lax.fori_loop(0, tm, body, 0)

    issue(0, 0)
    h = h2_ref[...].astype(BF16)
    hg = jnp.dot(h, wgs_ref[...], preferred_element_type=F32)
    hu = jnp.dot(h, wus_ref[...], preferred_element_type=F32)
    acc = jnp.dot((jax.nn.silu(hg) * hu).astype(BF16), wds_ref[...], preferred_element_type=F32)
    for j in range(TOP_K):
        if j + 1 < TOP_K:
            issue(j + 1, (j + 1) % 2)
        drain(j % 2)
        acc = acc + wt_ref[:, j:j + 1] * gbuf[j % 2]
    o_ref[...] = x_ref[...] + ga_ref[pl.ds(r, 1), :] * acc


def combine(pos, x, h2, w_t, ga, wgs_bf, wus_bf, wds_bf, ys, *, L, B, n_rows):
    T, D = x.shape
    F = wgs_bf.shape[1]
    tm = _pick(n_rows, 128, 128)
    return pl.pallas_call(
        functools.partial(_combine_kernel, tm=tm, L=L, B=B),
        grid=(n_rows // tm,),
        in_specs=[
            pl.BlockSpec((TOP_K, tm), lambda i: (0, i), memory_space=pltpu.SMEM),
            pl.BlockSpec((tm, D), lambda i: (i, 0)),
            pl.BlockSpec((tm, D), lambda i: (i, 0)),
            pl.BlockSpec((tm, TOP_K), lambda i: (i, 0)),
            pl.BlockSpec((COND_ROWS, D), lambda i: (0, 0)),
            pl.BlockSpec((D, F), lambda i: (0, 0)),
            pl.BlockSpec((D, F), lambda i: (0, 0)),
            pl.BlockSpec((F, D), lambda i: (0, 0)),
            pl.BlockSpec(memory_space=pl.ANY),
        ],
        out_specs=pl.BlockSpec((tm, D), lambda i: (i, 0)),
        out_shape=jax.ShapeDtypeStruct((T, D), F32),
        scratch_shapes=[pltpu.VMEM((2, tm, D), F32), pltpu.SemaphoreType.DMA((2,))],
        input_output_aliases={1: 0},
        compiler_params=_cp(("arbitrary",)),
        name="combine",
    )(pos, x, h2, w_t, ga, wgs_bf, wus_bf, wds_bf, ys)


def _dispatch_plan(cnt, eidx, rank, *, tr, n_tiles):
    n_exp = cnt.shape[0]
    tiles_e = (cnt + tr - 1) // tr
    tile_end = jnp.cumsum(tiles_e)
    tile_start = tile_end - tiles_e
    pos = (tile_start * tr)[eidx] + rank
    n_used = tile_end[-1]
    ti = jnp.arange(n_tiles, dtype=I32)
    ti_c = jnp.minimum(ti, n_used - 1)
    gid = jnp.minimum(jnp.searchsorted(tile_end, ti_c, side="right"), n_exp - 1).astype(I32)
    nvalid = jnp.where(ti < n_used, jnp.clip(cnt[gid] - (ti - tile_start[gid]) * tr, 0, tr), 0).astype(I32)
    return pos.astype(I32), gid, nvalid, n_used.reshape(1).astype(I32)


def kernel(x, c, ctx, c_ctx, w_ada, b_ada, g_norm1, g_norm2, w_in, g_q, g_k, w_fourier, w_dw, b_dw, g_conv_ln, b_conv_ln, w_pw, b_pw, g_sgu, w_spatial, b_spatial, w_out, w_router, b_router, w_gate_e, w_up_e, w_down_e, w_gate_s, w_up_s, w_down_s):
    B, L, D = x.shape
    Lc = ctx.shape[1]
    depth = w_ada.shape[0]
    assert B + 1 <= COND_ROWS and L % Lc == 0 and L % GRID_W == 0
    d_fourier = w_fourier.shape[1] * w_fourier.shape[2]
    d_conv = w_pw.shape[1]
    d_sgu = g_sgu.shape[1]
    d_attn = D - d_fourier - d_conv - d_sgu
    kv_w = d_attn // GQA_GROUP
    v_end = d_attn + 2 * kv_w
    f_end = v_end + d_fourier
    c_end = f_end + 2 * d_conv
    d_in = c_end + 2 * d_sgu
    assert w_in.shape[2] == d_in
    n_exp = w_router.shape[2]
    n_lat = B * L
    T_all = n_lat + B * Lc
    cb = D // 8
    exp_tr = 256

    assert n_exp // EXPERTS_PER_GROUP == 8 and (B * Lc) % _pick(L, 512) == 0
    fh = w_gate_s.shape[2]

    cond = jnp.concatenate([c, c_ctx[None, :], jnp.zeros((COND_ROWS - B - 1, D), F32)], axis=0)
    mod = modulation(cond, w_ada, b_ada)
    mod = mod.reshape(depth, COND_ROWS, N_MOD, D).transpose(0, 2, 1, 3)

    xa = jnp.concatenate([x.reshape(n_lat, D), ctx.reshape(B * Lc, D)], axis=0)

    for l in range(depth):
        last = l == depth - 1
        n_rows = n_lat if last else T_all
        sh1, sc1, ga1, sh2, sc2, ga2 = (mod[l, m] for m in range(N_MOD))

        w_qkv_bf = cast_cols(w_in, l, 0, v_end, cb)
        w_f_bf = cast_cols(w_in, l, v_end, d_fourier, cb)
        w_c_bf = cast_cols(w_in, l, f_end, 2 * d_conv, cb)
        w_s_bf = cast_cols(w_in, l, c_end, 2 * d_sgu, cb)
        w_out_bf = cast_cols(w_out, l, 0, D, cb)
        w_pw_bf = cast_cols(w_pw, l, 0, d_conv, _pick(d_conv, 512, 128))
        wgs_bf = cast_cols(w_gate_s, l, 0, fh, fh)
        wus_bf = cast_cols(w_up_s, l, 0, fh, fh)
        wds_bf = cast_cols(w_down_s, l, 0, D, cb)

        h = norm_mod(xa, g_norm1[l], sh1, sc1, L=L, B=B)
        z_qkv = matmul(h, w_qkv_bf, T_all)
        z_f = matmul(h, w_f_bf, n_rows)
        z_c = matmul(h, w_c_bf, n_rows)
        z_s = matmul(h, w_s_bf, n_rows)
        qn, kn = qk_norm_rope(z_qkv, g_q[l], g_k[l], L=L, B=B, d_attn=d_attn, kv_w=kv_w)
        y_a = attention(qn, kn, z_qkv, None, q_row0=0, Lq=L, segs=[(0, L), (n_lat, Lc)],
                        B=B, d_attn=d_attn, kv_w=kv_w)
        y_f = fourier(z_f, w_fourier[l], None, row0=0, Ls=L, B=B, n_out_rows=n_rows)
        if not last:
            y_a = attention(qn, kn, z_qkv, y_a, q_row0=n_lat, Lq=Lc, segs=[(n_lat, Lc)],
                            B=B, d_attn=d_attn, kv_w=kv_w)
            y_f = fourier(z_f, w_fourier[l], y_f, row0=n_lat, Ls=Lc, B=B, n_out_rows=n_rows)
        y_c = conv_module(z_c, w_dw[l], b_dw[l], g_conv_ln[l], b_conv_ln[l], w_pw_bf, b_pw[l],
                          L=L, Lc=Lc, B=B, n_rows=n_rows)
        y_s = sgu_mixer(z_s, g_sgu[l], w_spatial[l], b_spatial[l], n_rows=n_rows)
        xa = out_proj([y_a, y_f, y_c, y_s], w_out_bf, xa, ga1, L=L, B=B, n_rows=n_rows)

        h2, eidx, wsel, rank, cnt = router(xa, g_norm2[l], sh2, sc2, w_router[l], b_router[l],
                                           L=L, B=B, n_rows=n_rows)
        n_tiles = -(-n_rows * TOP_K // exp_tr) + n_exp
        pos, gid, nvalid, nused = _dispatch_plan(cnt[:, 0].astype(I32), eidx, rank, tr=exp_tr, n_tiles=n_tiles)
        xs = dispatch(pos, h2, n_tiles * exp_tr)
        ys = experts(xs, gid, nvalid, nused, w_gate_e, w_up_e, w_down_e, l, tr=exp_tr)
        xa = combine(pos, xa, h2, wsel.T, ga2, wgs_bf, wus_bf, wds_bf, ys, L=L, B=B, n_rows=n_rows)

    return xa[:n_lat].reshape(B, L, D)
```

```python
import functools

import jax
import jax.numpy as jnp
import numpy as np
from jax import lax
from jax.experimental import pallas as pl
from jax.experimental.pallas import tpu as pltpu

F32 = jnp.float32
BF16 = jnp.bfloat16
I32 = jnp.int32

HEAD_DIM = 128
GRID_W = 64
ROPE_THETA = 10000.0
NORM_EPS = 1e-6
GQA_GROUP = 4
N_FOURIER_GROUPS = 4
CONV_WIDTH = 31
CONV_PAD = CONV_WIDTH // 2
CONV_HALO = 16
SGU_CHUNK = 128
N_MOD = 6
TOP_K = 8
EXPERTS_PER_GROUP = 8
TOPK_GROUPS = 4
ROUTED_SCALE = 2.5
COND_ROWS = 8

VMEM_LIMIT = 56 * 1024 * 1024


def _cp(sem, vmem=VMEM_LIMIT, **kw):
    return pltpu.CompilerParams(dimension_semantics=sem, vmem_limit_bytes=vmem, **kw)


def _pick(n, pref, mult=8):
    if n <= pref:
        return n
    for t in range(pref - pref % mult, 0, -mult):
        if n % t == 0:
            return t
    raise ValueError(f"no tile for {n} <= {pref}")


def _rms(t, eps=NORM_EPS):
    return t * lax.rsqrt(jnp.mean(t * t, axis=-1, keepdims=True) + eps)


_HI16 = -65536


def _pack_halves(v):
    half = v.shape[1] // 2
    lo = lax.bitcast_convert_type(v[:, :half].astype(BF16).astype(F32), I32)
    hi = lax.bitcast_convert_type(v[:, half:].astype(BF16).astype(F32), I32)
    return lax.shift_right_logical(lo, 16) | (hi & _HI16)


def _unpack_halves(p):
    lo = lax.bitcast_convert_type(lax.shift_left(p, 16), F32)
    hi = lax.bitcast_convert_type(p & _HI16, F32)
    return lo, hi


def _cast_kernel(x_ref, o_ref):
    o_ref[...] = x_ref[...].astype(o_ref.dtype)


def cast_cols(w, l, col0, ncols, cb):
    _, R, _ = w.shape
    tr = _pick(R, 512)
    return pl.pallas_call(
        _cast_kernel,
        grid=(R // tr, ncols // cb),
        in_specs=[pl.BlockSpec((None, tr, cb), lambda i, j: (l, i, col0 // cb + j))],
        out_specs=pl.BlockSpec((tr, cb), lambda i, j: (i, j)),
        out_shape=jax.ShapeDtypeStruct((R, ncols), BF16),
        compiler_params=_cp(("parallel", "parallel")),
        name="cast_bf16",
    )(w)


def _modulation_kernel(c_ref, w_ref, b_ref, o_ref):
    s = jax.nn.silu(c_ref[...]).astype(BF16)
    o_ref[...] = jnp.dot(s, w_ref[...].astype(BF16), preferred_element_type=F32) + b_ref[...]


def modulation(cond, w_ada, b_ada):
    depth, D, N = w_ada.shape
    tn = _pick(N, 512, 128)
    return pl.pallas_call(
        _modulation_kernel,
        grid=(depth, N // tn),
        in_specs=[
            pl.BlockSpec((COND_ROWS, D), lambda l, j: (0, 0)),
            pl.BlockSpec((None, D, tn), lambda l, j: (l, 0, j)),
            pl.BlockSpec((None, 1, tn), lambda l, j: (l, 0, j)),
        ],
        out_specs=pl.BlockSpec((None, COND_ROWS, tn), lambda l, j: (l, 0, j)),
        out_shape=jax.ShapeDtypeStruct((depth, COND_ROWS, N), F32),
        compiler_params=_cp(("parallel", "parallel")),
        name="modulation",
    )(cond, w_ada, b_ada.reshape(depth, 1, N))


def _cond_row(i, tm, L, B):
    return jnp.minimum((i * tm) // L, B)


def _norm_mod_kernel(x_ref, g_ref, sh_ref, sc_ref, o_ref, *, tm, L, B):
    r = _cond_row(pl.program_id(0), tm, L, B)
    xn = _rms(x_ref[...]) * g_ref[...]
    o_ref[...] = (xn * (1.0 + sc_ref[pl.ds(r, 1), :]) + sh_ref[pl.ds(r, 1), :]).astype(o_ref.dtype)


def norm_mod(x, g, sh, sc, *, L, B):
    T, D = x.shape
    tm = _pick(L, 512)
    return pl.pallas_call(
        functools.partial(_norm_mod_kernel, tm=tm, L=L, B=B),
        grid=(T // tm,),
        in_specs=[
            pl.BlockSpec((tm, D), lambda i: (i, 0)),
            pl.BlockSpec((1, D), lambda i: (0, 0)),
            pl.BlockSpec((COND_ROWS, D), lambda i: (0, 0)),
            pl.BlockSpec((COND_ROWS, D), lambda i: (0, 0)),
        ],
        out_specs=pl.BlockSpec((tm, D), lambda i: (i, 0)),
        out_shape=jax.ShapeDtypeStruct((T, D), BF16),
        compiler_params=_cp(("parallel",)),
        name="norm_mod",
    )(x, g.reshape(1, D), sh, sc)


def _mm_kernel(a_ref, w_ref, o_ref):
    o_ref[...] = jnp.dot(a_ref[...], w_ref[...], preferred_element_type=F32).astype(o_ref.dtype)


def matmul(a, w, n_rows, out_dtype=BF16):
    _, K = a.shape
    N = w.shape[1]
    tm = _pick(n_rows, 512)
    tn = _pick(N, 1024, 128)
    return pl.pallas_call(
        _mm_kernel,
        grid=(N // tn, n_rows // tm),
        in_specs=[
            pl.BlockSpec((tm, K), lambda j, i: (i, 0)),
            pl.BlockSpec((K, tn), lambda j, i: (0, j)),
        ],
        out_specs=pl.BlockSpec((tm, tn), lambda j, i: (i, j)),
        out_shape=jax.ShapeDtypeStruct((n_rows, N), out_dtype),
        compiler_params=_cp(("parallel", "parallel")),
        name="in_proj",
    )(a, w)


def _rope_tables(L, tm):
    rows = L // GRID_W
    row = np.repeat(np.arange(rows), GRID_W).astype(np.float32)
    col = np.tile(np.arange(GRID_W), rows).astype(np.float32)
    axis_dim = HEAD_DIM // 2
    inv = (np.float32(ROPE_THETA) ** (-np.arange(0, axis_dim, 2, dtype=np.float32) / axis_dim)).astype(np.float32)
    ang = np.concatenate([row[:, None] * inv, col[:, None] * inv], axis=-1).astype(np.float64)
    cos = np.repeat(np.cos(ang), 2, axis=-1)
    sin = np.repeat(np.sin(ang), 2, axis=-1)
    sign = np.tile(np.array([-1.0, 1.0]), HEAD_DIM // 2)
    cos = np.concatenate([cos, np.ones((tm, HEAD_DIM))], axis=0)
    sin = np.concatenate([sin * sign, np.zeros((tm, HEAD_DIM))], axis=0)
    return jnp.asarray(cos, F32), jnp.asarray(sin, F32)


def _qk_kernel(q_ref, k_ref, cos_ref, sin_ref, gq_ref, gk_ref, qo_ref, ko_ref, *, n_q, n_kv):
    cos = cos_ref[...]
    sin = sin_ref[...]
    even = (lax.broadcasted_iota(I32, cos.shape, 1) & 1) == 0

    def norm_rope(t, g):
        n = _rms(t.astype(F32)) * g
        partner = jnp.where(even, pltpu.roll(n, HEAD_DIM - 1, 1), pltpu.roll(n, 1, 1))
        return n * cos + partner * sin

    q_scale = HEAD_DIM ** -0.5
    for h in range(n_q):
        sl = slice(h * HEAD_DIM, (h + 1) * HEAD_DIM)
        qo_ref[:, sl] = (norm_rope(q_ref[:, sl], gq_ref[...]) * q_scale).astype(qo_ref.dtype)
    for h in range(n_kv):
        sl = slice(h * HEAD_DIM, (h + 1) * HEAD_DIM)
        ko_ref[:, sl] = norm_rope(k_ref[:, sl], gk_ref[...]).astype(ko_ref.dtype)


def qk_norm_rope(z_qkv, g_q, g_k, *, L, B, d_attn, kv_w):
    T = z_qkv.shape[0]
    tm = _pick(L, 256)
    cos, sin = _rope_tables(L, tm)
    n_lat = B * L // tm
    per_seq = L // tm

    def tab_idx(i):
        return (jnp.where(i < n_lat, i % per_seq, per_seq), 0)

    return pl.pallas_call(
        functools.partial(_qk_kernel, n_q=d_attn // HEAD_DIM, n_kv=kv_w // HEAD_DIM),
        grid=(T // tm,),
        in_specs=[
            pl.BlockSpec((tm, d_attn), lambda i: (i, 0)),
            pl.BlockSpec((tm, kv_w), lambda i: (i, d_attn // kv_w)),
            pl.BlockSpec((tm, HEAD_DIM), tab_idx),
            pl.BlockSpec((tm, HEAD_DIM), tab_idx),
            pl.BlockSpec((1, HEAD_DIM), lambda i: (0, 0)),
            pl.BlockSpec((1, HEAD_DIM), lambda i: (0, 0)),
        ],
        out_specs=[
            pl.BlockSpec((tm, d_attn), lambda i: (i, 0)),
            pl.BlockSpec((tm, kv_w), lambda i: (i, 0)),
        ],
        out_shape=[
            jax.ShapeDtypeStruct((T, d_attn), BF16),
            jax.ShapeDtypeStruct((T, kv_w), BF16),
        ],
        compiler_params=_cp(("parallel",)),
        name="qk_norm_rope",
    )(z_qkv, z_qkv, cos, sin, g_q.reshape(1, HEAD_DIM), g_k.reshape(1, HEAD_DIM))


def _attn_kernel(*refs, n_seg):
    q_ref = refs[0]
    k_refs = refs[1:1 + n_seg]
    v_refs = refs[1 + n_seg:1 + 2 * n_seg]
    o_ref = refs[-1]
    for h in range(GQA_GROUP):
        sl = slice(h * HEAD_DIM, (h + 1) * HEAD_DIM)
        q = q_ref[:, sl]
        s = [lax.dot_general(q, k[...], (((1,), (1,)), ((), ())), preferred_element_type=F32)
             for k in k_refs]
        m = functools.reduce(jnp.maximum, [jnp.max(x, axis=-1, keepdims=True) for x in s])
        p = [jnp.exp(x - m) for x in s]
        den = functools.reduce(jnp.add, [jnp.sum(x, axis=-1, keepdims=True) for x in p])
        o = functools.reduce(jnp.add, [jnp.dot(x.astype(BF16), v[...], preferred_element_type=F32)
                                       for x, v in zip(p, v_refs)])
        o_ref[:, sl] = (o / den).astype(o_ref.dtype)


def attention(qn, kn, z_qkv, *, q_row0, Lq, segs, B, d_attn, kv_w):
    n_kv = kv_w // HEAD_DIM
    gw = GQA_GROUP * HEAD_DIM
    tq = _pick(Lq, 512)
    nq = Lq // tq
    v_col0 = (d_attn + kv_w) // HEAD_DIM

    def q_idx(b, g, qi):
        return (q_row0 // tq + b * nq + qi, g)

    in_specs = [pl.BlockSpec((tq, gw), q_idx)]
    args = [qn]
    for row0, ln in segs:
        in_specs.append(pl.BlockSpec((ln, HEAD_DIM), lambda b, g, qi, row0=row0, ln=ln: (row0 // ln + b, g)))
        args.append(kn)
    for row0, ln in segs:
        in_specs.append(
            pl.BlockSpec((ln, HEAD_DIM), lambda b, g, qi, row0=row0, ln=ln: (row0 // ln + b, v_col0 + g)))
        args.append(z_qkv)
    return pl.pallas_call(
        functools.partial(_attn_kernel, n_seg=len(segs)),
        grid=(B, n_kv, nq),
        in_specs=in_specs,
        out_specs=pl.BlockSpec((tq, gw), lambda b, g, qi: (b * nq + qi, g)),
        out_shape=jax.ShapeDtypeStruct((B * Lq, d_attn), BF16),
        compiler_params=_cp(("parallel", "parallel", "parallel")),
        name="attention",
    )(*args)


def _dft_tables(n, dtype):
    k = np.arange(n, dtype=np.int64)
    ang = ((k[:, None] * k[None, :]) % n).astype(np.float64) * (2.0 * np.pi / n)
    return jnp.asarray(np.cos(ang), dtype), jnp.asarray(np.sin(ang), dtype)


def _fourier_kernel(t_ref, cl_ref, sl_ref, cc_ref, sc_ref, w_ref, o_ref, *, scale):
    w = w_ref[...]
    hi = lax.Precision.HIGHEST
    mc = jnp.dot(cc_ref[...], w, precision=hi, preferred_element_type=F32).astype(BF16)
    ms = jnp.dot(sc_ref[...], w, precision=hi, preferred_element_type=F32).astype(BF16)
    t = t_ref[...]
    bc = jnp.dot(t, mc, preferred_element_type=F32).astype(BF16)
    bs = jnp.dot(t, ms, preferred_element_type=F32).astype(BF16)
    y = (jnp.dot(cl_ref[...], bc, preferred_element_type=F32)
         - jnp.dot(sl_ref[...], bs, preferred_element_type=F32))
    o_ref[...] = (y * scale).astype(o_ref.dtype)


def fourier(z_f, w_f, *, row0, Ls, B):
    ng, fg, _ = w_f.shape
    cl, sl = _dft_tables(Ls, BF16)
    cc, sc = _dft_tables(fg, F32)
    scale = 1.0 / float(Ls * fg) ** 0.5
    return pl.pallas_call(
        functools.partial(_fourier_kernel, scale=scale),
        grid=(B, ng),
        in_specs=[
            pl.BlockSpec((Ls, fg), lambda b, g: (row0 // Ls + b, g)),
            pl.BlockSpec((Ls, Ls), lambda b, g: (0, 0)),
            pl.BlockSpec((Ls, Ls), lambda b, g: (0, 0)),
            pl.BlockSpec((fg, fg), lambda b, g: (0, 0)),
            pl.BlockSpec((fg, fg), lambda b, g: (0, 0)),
            pl.BlockSpec((None, fg, fg), lambda b, g: (g, 0, 0)),
        ],
        out_specs=pl.BlockSpec((Ls, fg), lambda b, g: (b, g)),
        out_shape=jax.ShapeDtypeStruct((B * Ls, ng * fg), BF16),
        compiler_params=_cp(("parallel", "parallel")),
        name="fourier",
    )(z_f, cl, sl, cc, sc, w_f)


def _conv_kernel(cur_ref, prev_ref, next_ref, wdw_ref, bdw_ref, gln_ref, bln_ref, wpw_ref, bpw_ref,
                 o_ref, gbuf, *, tr, dc, n_lat, per_lat, per_ctx):
    i = pl.program_id(0)
    j = jnp.where(i < n_lat, i % per_lat, (i - n_lat) % per_ctx)
    per = jnp.where(i < n_lat, per_lat, per_ctx)
    first = j == 0
    last = j == per - 1

    def glu(zb):
        return zb[:, :dc].astype(F32) * jax.nn.sigmoid(zb[:, dc:].astype(F32))

    gbuf[0:CONV_HALO, :] = jnp.where(first, 0.0, glu(prev_ref[...]))
    gbuf[CONV_HALO:CONV_HALO + tr, :] = glu(cur_ref[...])
    gbuf[CONV_HALO + tr:, :] = jnp.where(last, 0.0, glu(next_ref[...]))
    acc = jnp.broadcast_to(bdw_ref[...], (tr, dc))
    base = CONV_HALO - CONV_PAD
    for k in range(CONV_WIDTH):
        acc = acc + gbuf[base + k:base + k + tr, :] * wdw_ref[k:k + 1, :]
    mu = jnp.mean(acc, axis=-1, keepdims=True)
    cen = acc - mu
    var = jnp.mean(cen * cen, axis=-1, keepdims=True)
    hn = cen * lax.rsqrt(var + NORM_EPS) * gln_ref[...] + bln_ref[...]
    act = jax.nn.silu(hn).astype(BF16)
    o_ref[...] = (jnp.dot(act, wpw_ref[...], preferred_element_type=F32) + bpw_ref[...]).astype(o_ref.dtype)


def conv_module(z_c, w_dw, b_dw, g_ln, b_ln, w_pw_bf, b_pw, *, L, Lc, B, n_rows):
    dc = w_pw_bf.shape[0]
    tr = _pick(Lc, 256, CONV_HALO)
    hpt = tr // CONV_HALO
    n_halo = z_c.shape[0] // CONV_HALO
    w_pad = jnp.concatenate([w_dw, jnp.zeros((32 - CONV_WIDTH, dc), F32)], axis=0)
    row = lambda v: v.reshape(1, dc)
    return pl.pallas_call(
        functools.partial(_conv_kernel, tr=tr, dc=dc, n_lat=B * L // tr, per_lat=L // tr, per_ctx=Lc // tr),
        grid=(n_rows // tr,),
        in_specs=[
            pl.BlockSpec((tr, 2 * dc), lambda i: (i, 0)),
            pl.BlockSpec((CONV_HALO, 2 * dc), lambda i: (jnp.maximum(i * hpt - 1, 0), 0)),
            pl.BlockSpec((CONV_HALO, 2 * dc), lambda i: (jnp.minimum((i + 1) * hpt, n_halo - 1), 0)),
            pl.BlockSpec((32, dc), lambda i: (0, 0)),
            pl.BlockSpec((1, dc), lambda i: (0, 0)),
            pl.BlockSpec((1, dc), lambda i: (0, 0)),
            pl.BlockSpec((1, dc), lambda i: (0, 0)),
            pl.BlockSpec((dc, dc), lambda i: (0, 0)),
            pl.BlockSpec((1, dc), lambda i: (0, 0)),
        ],
        out_specs=pl.BlockSpec((tr, dc), lambda i: (i, 0)),
        out_shape=jax.ShapeDtypeStruct((n_rows, dc), BF16),
        scratch_shapes=[pltpu.VMEM((tr + 2 * CONV_HALO, dc), F32)],
        compiler_params=_cp(("parallel",)),
        name="conv_module",
    )(z_c, z_c, z_c, w_pad, row(b_dw), row(g_ln), row(b_ln), w_pw_bf, row(b_pw))


def _sgu_kernel(z_ref, g_ref, ws_ref, bs_ref, o_ref, *, tm, ds, n_heads):
    for h in range(n_heads):
        sl = slice(h * HEAD_DIM, (h + 1) * HEAD_DIM)
        u = jax.nn.gelu(z_ref[:, sl].astype(F32))
        v = jax.nn.gelu(z_ref[:, ds + h * HEAD_DIM:ds + (h + 1) * HEAD_DIM].astype(F32))
        vn = (_rms(v) * g_ref[:, sl]).astype(BF16)
        w = ws_ref[h].astype(BF16)
        b = bs_ref[:, h:h + 1]
        for c in range(tm // SGU_CHUNK):
            rows = slice(c * SGU_CHUNK, (c + 1) * SGU_CHUNK)
            s = jnp.dot(w, vn[rows, :], preferred_element_type=F32) + b
            o_ref[rows, sl] = (u[rows, :] * s).astype(o_ref.dtype)


def sgu_mixer(z_s, g_sgu, w_s, b_s, *, n_rows):
    n_heads = w_s.shape[0]
    ds = n_heads * HEAD_DIM
    tm = _pick(n_rows, 256, SGU_CHUNK)
    return pl.pallas_call(
        functools.partial(_sgu_kernel, tm=tm, ds=ds, n_heads=n_heads),
        grid=(n_rows // tm,),
        in_specs=[
            pl.BlockSpec((tm, 2 * ds), lambda i: (i, 0)),
            pl.BlockSpec((1, ds), lambda i: (0, 0)),
            pl.BlockSpec((n_heads, SGU_CHUNK, SGU_CHUNK), lambda i: (0, 0, 0)),
            pl.BlockSpec((SGU_CHUNK, n_heads), lambda i: (0, 0)),
        ],
        out_specs=pl.BlockSpec((tm, ds), lambda i: (i, 0)),
        out_shape=jax.ShapeDtypeStruct((n_rows, ds), BF16),
        compiler_params=_cp(("parallel",)),
        name="sgu_mixer",
    )(z_s, g_sgu.reshape(1, ds), w_s, b_s.T)


def _out_proj_kernel(*refs, tm, L, B, widths, split, n_lat_tiles):
    n_y = sum(2 if sp else 1 for sp in split)
    y_refs = refs[:n_y]
    w_ref, x_ref, ga_ref, o_ref = refs[n_y:]
    i = pl.program_id(1)
    r = _cond_row(i, tm, L, B)
    acc = None
    k0 = 0
    p = 0
    for wd, sp in zip(widths, split):
        if sp:
            y = jnp.where(i < n_lat_tiles, y_refs[p][...], y_refs[p + 1][...])
            p += 2
        else:
            y = y_refs[p][...]
            p += 1
        part = jnp.dot(y, w_ref[k0:k0 + wd, :], preferred_element_type=F32)
        acc = part if acc is None else acc + part
        k0 += wd
    o_ref[...] = x_ref[...] + ga_ref[pl.ds(r, 1), :] * acc


def out_proj(ys, w_out_bf, x, ga, *, L, B, n_rows):
    D = x.shape[1]
    tm = _pick(L, 512)
    tn = _pick(D, 1024, 128)
    n_lat_tiles = B * L // tm
    split = tuple(isinstance(y, tuple) for y in ys)
    widths = tuple((y[0] if sp else y).shape[1] for y, sp in zip(ys, split))
    in_specs, args = [], []
    for y, wd, sp in zip(ys, widths, split):
        if sp:
            in_specs.append(pl.BlockSpec((tm, wd), lambda j, i: (jnp.minimum(i, n_lat_tiles - 1), 0)))
            in_specs.append(pl.BlockSpec((tm, wd), lambda j, i: (jnp.maximum(i - n_lat_tiles, 0), 0)))
            args += [y[0], y[1]]
        else:
            in_specs.append(pl.BlockSpec((tm, wd), lambda j, i: (i, 0)))
            args.append(y)
    in_specs += [
        pl.BlockSpec((sum(widths), tn), lambda j, i: (0, j)),
        pl.BlockSpec((tm, tn), lambda j, i: (i, j)),
        pl.BlockSpec((COND_ROWS, tn), lambda j, i: (0, j)),
    ]
    return pl.pallas_call(
        functools.partial(_out_proj_kernel, tm=tm, L=L, B=B, widths=widths, split=split,
                          n_lat_tiles=n_lat_tiles),
        grid=(D // tn, n_rows // tm),
        in_specs=in_specs,
        out_specs=pl.BlockSpec((tm, tn), lambda j, i: (i, j)),
        out_shape=jax.ShapeDtypeStruct((n_rows, D), F32),
        compiler_params=_cp(("parallel", "parallel")),
        name="out_proj",
    )(*args, w_out_bf, x, ga)


def _router_kernel(x_ref, g_ref, sh_ref, sc_ref, wr_ref, br_ref,
                   h2_ref, eidx_ref, wsel_ref, rank_ref, cnt_ref, *, tm, L, B, n_exp):
    i = pl.program_id(0)
    r = _cond_row(i, tm, L, B)
    xn = _rms(x_ref[...]) * g_ref[...]
    h2 = xn * (1.0 + sc_ref[pl.ds(r, 1), :]) + sh_ref[pl.ds(r, 1), :]
    h2_ref[...] = _pack_halves(h2)
    logits = lax.dot_general(wr_ref[...], h2, (((1,), (1,)), ((), ())),
                             precision=lax.Precision.HIGHEST, preferred_element_type=F32)
    s = jax.nn.sigmoid(logits)
    sel = s + br_ref[...]
    shape = (n_exp, tm)
    e_io = lax.broadcasted_iota(I32, shape, 0)
    e_f = e_io.astype(F32)
    k_io = e_io & (EXPERTS_PER_GROUP - 1)
    g_io = e_io >> 3
    n_groups = n_exp // EXPERTS_PER_GROUP
    neg = -jnp.inf

    def partner(v, d):
        up = pltpu.roll(v, n_exp - d, 0)
        dn = pltpu.roll(v, d, 0)
        return jnp.where((k_io & d) == 0, up, dn)

    def group_all(v, op):
        d = 1
        while d < EXPERTS_PER_GROUP:
            v = op(v, partner(v, d))
            d *= 2
        return v

    kf = k_io.astype(F32)
    m1 = group_all(sel, jnp.maximum)
    i1 = group_all(jnp.where(sel == m1, kf, float(EXPERTS_PER_GROUP)), jnp.minimum)
    m2 = group_all(jnp.where(kf == i1, neg, sel), jnp.maximum)
    gs = m1 + m2
    beaten = jnp.zeros(shape, I32)
    for d in range(1, n_groups):
        other = pltpu.roll(gs, n_exp - EXPERTS_PER_GROUP * d, 0)
        og = (g_io + d) & (n_groups - 1)
        beats = (other > gs) | ((other == gs) & (og < g_io))
        beaten = beaten + beats.astype(I32)
    val = jnp.where(beaten < TOPK_GROUPS, sel, neg)

    assign = jnp.zeros(shape, F32)
    w_rows = []
    idx_rows = []
    for j in range(TOP_K):
        m = jnp.max(val, axis=0, keepdims=True)
        idx = jnp.min(jnp.where(val == m, e_f, float(n_exp)), axis=0, keepdims=True)
        hit = e_f == idx
        w_rows.append(jnp.sum(jnp.where(hit, s, 0.0), axis=0, keepdims=True))
        val = jnp.where(hit, neg, val)
        assign = assign + hit.astype(F32)
        idx_rows.append(idx)
        eidx_ref[j:j + 1, :] = idx.astype(I32)
    w_sum = functools.reduce(jnp.add, w_rows)
    for j in range(TOP_K):
        wsel_ref[j:j + 1, :] = w_rows[j] / w_sum * ROUTED_SCALE

    @pl.when(i == 0)
    def _():
        cnt_ref[...] = jnp.zeros_like(cnt_ref)

    earlier = (lax.broadcasted_iota(I32, (tm, tm), 0) < lax.broadcasted_iota(I32, (tm, tm), 1)).astype(BF16)
    before = jnp.dot(assign.astype(BF16), earlier, preferred_element_type=F32) + cnt_ref[...]
    for j in range(TOP_K):
        hit = e_f == idx_rows[j]
        rank_ref[j:j + 1, :] = jnp.sum(jnp.where(hit, before, 0.0), axis=0, keepdims=True).astype(I32)
    cnt_ref[...] = cnt_ref[...] + jnp.sum(assign, axis=1, keepdims=True)


def router(x, g, sh, sc, w_router, b_router, *, L, B, n_rows):
    D = x.shape[1]
    n_exp = w_router.shape[1]
    tm = _pick(L, 512, 128)
    assert n_rows % tm == 0
    return pl.pallas_call(
        functools.partial(_router_kernel, tm=tm, L=L, B=B, n_exp=n_exp),
        grid=(n_rows // tm,),
        in_specs=[
            pl.BlockSpec((tm, D), lambda i: (i, 0)),
            pl.BlockSpec((1, D), lambda i: (0, 0)),
            pl.BlockSpec((COND_ROWS, D), lambda i: (0, 0)),
            pl.BlockSpec((COND_ROWS, D), lambda i: (0, 0)),
            pl.BlockSpec((n_exp, D), lambda i: (0, 0)),
            pl.BlockSpec((n_exp, 1), lambda i: (0, 0)),
        ],
        out_specs=[
            pl.BlockSpec((tm, D // 2), lambda i: (i, 0)),
            pl.BlockSpec((TOP_K, tm), lambda i: (0, i)),
            pl.BlockSpec((TOP_K, tm), lambda i: (0, i)),
            pl.BlockSpec((TOP_K, tm), lambda i: (0, i)),
            pl.BlockSpec((n_exp, 1), lambda i: (0, 0)),
        ],
        out_shape=[
            jax.ShapeDtypeStruct((n_rows, D // 2), I32),
            jax.ShapeDtypeStruct((TOP_K, n_rows), I32),
            jax.ShapeDtypeStruct((TOP_K, n_rows), F32),
            jax.ShapeDtypeStruct((TOP_K, n_rows), I32),
            jax.ShapeDtypeStruct((n_exp, 1), F32),
        ],
        compiler_params=_cp(("arbitrary",)),
        name="router",
    )(x, g.reshape(1, D), sh, sc, w_router.T, b_router.reshape(n_exp, 1))


def _row_copy(src, s, dst, d, sem):
    return pltpu.make_async_copy(src.at[pl.ds(s, 1)], dst.at[pl.ds(d, 1)], sem)


def _dispatch_kernel(pos_ref, h_ref, xs_ref, sem, *, tm):
    def issue(t, carry):
        for j in range(TOP_K):
            _row_copy(h_ref, t, xs_ref, pos_ref[j, t], sem).start()
        return carry

    lax.fori_loop(0, tm, issue, 0, unroll=4)
    for j in range(TOP_K):
        pltpu.make_async_copy(h_ref, xs_ref.at[pl.ds(0, tm)], sem).wait()


def dispatch(pos, h2p, n_slots):
    T, W = h2p.shape
    tm = _pick(T, 256, 128)
    return pl.pallas_call(
        functools.partial(_dispatch_kernel, tm=tm),
        grid=(T // tm,),
        in_specs=[
            pl.BlockSpec((TOP_K, tm), lambda i: (0, i), memory_space=pltpu.SMEM),
            pl.BlockSpec((tm, W), lambda i: (i, 0)),
        ],
        out_specs=pl.BlockSpec(memory_space=pl.ANY),
        out_shape=jax.ShapeDtypeStruct((n_slots, W), I32),
        scratch_shapes=[pltpu.SemaphoreType.DMA(())],
        compiler_params=_cp(("arbitrary",), has_side_effects=True),
        name="dispatch",
    )(pos, h2p)


def _experts_kernel(gid_ref, nvalid_ref, nused_ref, xs_ref, wg_ref, wu_ref, wd_ref, o_ref,
                    wg_bf, wu_bf, wd_bf, *, tr):
    i = pl.program_id(0)
    changed = jnp.logical_or(i == 0, gid_ref[i] != gid_ref[jnp.maximum(i - 1, 0)])

    @pl.when(changed)
    def _():
        wg_bf[...] = wg_ref[...].astype(BF16)
        wu_bf[...] = wu_ref[...].astype(BF16)
        wd_bf[...] = wd_ref[...].astype(BF16)

    @pl.when(i < nused_ref[0])
    def _():
        half = xs_ref.shape[1]
        valid = lax.broadcasted_iota(I32, (tr, 1), 0) < nvalid_ref[i]
        lo, hi = _unpack_halves(xs_ref[...])
        lo = jnp.where(valid, lo, 0.0).astype(BF16)
        hi = jnp.where(valid, hi, 0.0).astype(BF16)

        def proj(w_bf):
            return (jnp.dot(lo, w_bf[:half, :], preferred_element_type=F32)
                    + jnp.dot(hi, w_bf[half:, :], preferred_element_type=F32))

        hh = (jax.nn.silu(proj(wg_bf)) * proj(wu_bf)).astype(BF16)
        o_ref[...] = _pack_halves(jnp.dot(hh, wd_bf[...], preferred_element_type=F32))


def experts(xs, gid, nvalid, nused, w_ge, w_ue, w_de, l, *, tr):
    P, W = xs.shape
    D = 2 * W
    F = w_ge.shape[3]
    n_tiles = P // tr

    def row_idx(i, gid, nvalid, nused):
        return (jnp.minimum(i, nused[0] - 1), 0)

    def w_idx(i, gid, nvalid, nused):
        return (l, gid[i], 0, 0)

    return pl.pallas_call(
        functools.partial(_experts_kernel, tr=tr),
        grid_spec=pltpu.PrefetchScalarGridSpec(
            num_scalar_prefetch=3,
            grid=(n_tiles,),
            in_specs=[
                pl.BlockSpec((tr, W), row_idx),
                pl.BlockSpec((None, None, D, F), w_idx),
                pl.BlockSpec((None, None, D, F), w_idx),
                pl.BlockSpec((None, None, F, D), w_idx),
            ],
            out_specs=pl.BlockSpec((tr, W), row_idx),
            scratch_shapes=[pltpu.VMEM((D, F), BF16), pltpu.VMEM((D, F), BF16), pltpu.VMEM((F, D), BF16)],
        ),
        out_shape=jax.ShapeDtypeStruct((P, W), I32),
        compiler_params=_cp(("arbitrary",)),
        name="experts",
    )(gid, nvalid, nused, xs, w_ge, w_ue, w_de)


def _combine_kernel(pos_ref, x_ref, h_ref, wt_ref, ga_ref, wgs_ref, wus_ref, wds_ref, ys_ref,
                    o_ref, gbuf, sems, *, tm, L, B):
    r = _cond_row(pl.program_id(0), tm, L, B)
    half = h_ref.shape[1]

    def issue(j, slot):
        def body(t, carry):
            _row_copy(ys_ref, pos_ref[j, t], gbuf.at[slot], t, sems.at[slot]).start()
            return carry
        lax.fori_loop(0, tm, body, 0, unroll=8)

    def drain(slot):
        pltpu.make_async_copy(ys_ref.at[pl.ds(0, tm)], gbuf.at[slot], sems.at[slot]).wait()

    issue(0, 0)
    lo, hi = _unpack_halves(h_ref[...])
    lo = lo.astype(BF16)
    hi = hi.astype(BF16)

    def proj(w_ref):
        return (jnp.dot(lo, w_ref[:half, :], preferred_element_type=F32)
                + jnp.dot(hi, w_ref[half:, :], preferred_element_type=F32))

    hh = (jax.nn.silu(proj(wgs_ref)) * proj(wus_ref)).astype(BF16)
    shared = jnp.dot(hh, wds_ref[...], preferred_element_type=F32)
    acc_lo = shared[:, :half]
    acc_hi = shared[:, half:]
    for j in range(TOP_K):
        if j + 1 < TOP_K:
            issue(j + 1, (j + 1) % 2)
        drain(j % 2)
        y_lo, y_hi = _unpack_halves(gbuf[j % 2])
        w = wt_ref[:, j:j + 1]
        acc_lo = acc_lo + w * y_lo
        acc_hi = acc_hi + w * y_hi
    ga = ga_ref[pl.ds(r, 1), :]
    o_ref[:, :half] = x_ref[:, :half] + ga[:, :half] * acc_lo
    o_ref[:, half:] = x_ref[:, half:] + ga[:, half:] * acc_hi


def combine(pos, x, h2p, w_t, ga, wgs_bf, wus_bf, wds_bf, ys, *, L, B, n_rows):
    D = x.shape[1]
    W = D // 2
    F = wgs_bf.shape[1]
    tm = _pick(L, 256, 128)
    return pl.pallas_call(
        functools.partial(_combine_kernel, tm=tm, L=L, B=B),
        grid=(n_rows // tm,),
        in_specs=[
            pl.BlockSpec((TOP_K, tm), lambda i: (0, i), memory_space=pltpu.SMEM),
            pl.BlockSpec((tm, D), lambda i: (i, 0)),
            pl.BlockSpec((tm, W), lambda i: (i, 0)),
            pl.BlockSpec((tm, TOP_K), lambda i: (i, 0)),
            pl.BlockSpec((COND_ROWS, D), lambda i: (0, 0)),
            pl.BlockSpec((D, F), lambda i: (0, 0)),
            pl.BlockSpec((D, F), lambda i: (0, 0)),
            pl.BlockSpec((F, D), lambda i: (0, 0)),
            pl.BlockSpec(memory_space=pl.ANY),
        ],
        out_specs=pl.BlockSpec((tm, D), lambda i: (i, 0)),
        out_shape=jax.ShapeDtypeStruct((n_rows, D), F32),
        scratch_shapes=[pltpu.VMEM((2, tm, W), I32), pltpu.SemaphoreType.DMA((2,))],
        compiler_params=_cp(("arbitrary",)),
        name="combine",
    )(pos, x, h2p, w_t, ga, wgs_bf, wus_bf, wds_bf, ys)


def _dispatch_plan(cnt, eidx, rank, *, tr, n_tiles):
    n_exp = cnt.shape[0]
    e_ids = jnp.arange(n_exp, dtype=I32)
    tiles_e = (cnt + tr - 1) // tr
    tile_end = jnp.cumsum(tiles_e)
    tile_start = tile_end - tiles_e

    def lookup(table, idx):
        return jnp.sum(jnp.where(idx[..., None] == e_ids, table, 0), axis=-1)

    pos = lookup(tile_start * tr, eidx) + rank
    n_used = tile_end[-1]
    ti = jnp.arange(n_tiles, dtype=I32)
    ti_c = jnp.minimum(ti, n_used - 1)
    gid = jnp.sum((ti_c[:, None] >= tile_end[None, :]).astype(I32), axis=1)
    left = lookup(cnt, gid) - (ti - lookup(tile_start, gid)) * tr
    nvalid = jnp.where(ti < n_used, jnp.clip(left, 0, tr), 0)
    return pos.astype(I32), gid.astype(I32), nvalid.astype(I32), n_used.reshape(1).astype(I32)


def kernel(x, c, ctx, c_ctx, w_ada, b_ada, g_norm1, g_norm2, w_in, g_q, g_k, w_fourier, w_dw, b_dw, g_conv_ln, b_conv_ln, w_pw, b_pw, g_sgu, w_spatial, b_spatial, w_out, w_router, b_router, w_gate_e, w_up_e, w_down_e, w_gate_s, w_up_s, w_down_s):
    B, L, D = x.shape
    Lc = ctx.shape[1]
    depth = w_ada.shape[0]
    assert B + 1 <= COND_ROWS and L % Lc == 0 and L % GRID_W == 0
    d_fourier = w_fourier.shape[1] * w_fourier.shape[2]
    d_conv = w_pw.shape[1]
    d_sgu = g_sgu.shape[1]
    d_attn = D - d_fourier - d_conv - d_sgu
    kv_w = d_attn // GQA_GROUP
    v_end = d_attn + 2 * kv_w
    f_end = v_end + d_fourier
    c_end = f_end + 2 * d_conv
    d_in = c_end + 2 * d_sgu
    assert w_in.shape[2] == d_in
    n_exp = w_router.shape[2]
    n_lat = B * L
    T_all = n_lat + B * Lc
    cb = D // 8
    exp_tr = 256

    assert n_exp // EXPERTS_PER_GROUP == 8 and (B * Lc) % _pick(L, 512) == 0
    fh = w_gate_s.shape[2]

    cond = jnp.concatenate([c, c_ctx[None, :], jnp.zeros((COND_ROWS - B - 1, D), F32)], axis=0)
    mod = modulation(cond, w_ada, b_ada)
    mod = mod.reshape(depth, COND_ROWS, N_MOD, D).transpose(0, 2, 1, 3)

    xa = jnp.concatenate([x.reshape(n_lat, D), ctx.reshape(B * Lc, D)], axis=0)

    for l in range(depth):
        last = l == depth - 1
        n_rows = n_lat if last else T_all
        sh1, sc1, ga1, sh2, sc2, ga2 = (mod[l, m] for m in range(N_MOD))

        w_qkv_bf = cast_cols(w_in, l, 0, v_end, cb)
        w_f_bf = cast_cols(w_in, l, v_end, d_fourier, cb)
        w_c_bf = cast_cols(w_in, l, f_end, 2 * d_conv, cb)
        w_s_bf = cast_cols(w_in, l, c_end, 2 * d_sgu, cb)
        w_out_bf = cast_cols(w_out, l, 0, D, cb)
        w_pw_bf = cast_cols(w_pw, l, 0, d_conv, _pick(d_conv, 512, 128))
        wgs_bf = cast_cols(w_gate_s, l, 0, fh, fh)
        wus_bf = cast_cols(w_up_s, l, 0, fh, fh)
        wds_bf = cast_cols(w_down_s, l, 0, D, cb)

        h = norm_mod(xa, g_norm1[l], sh1, sc1, L=L, B=B)
        z_qkv = matmul(h, w_qkv_bf, T_all)
        z_f = matmul(h, w_f_bf, n_rows)
        z_c = matmul(h, w_c_bf, n_rows)
        z_s = matmul(h, w_s_bf, n_rows)
        qn, kn = qk_norm_rope(z_qkv, g_q[l], g_k[l], L=L, B=B, d_attn=d_attn, kv_w=kv_w)
        y_a = attention(qn, kn, z_qkv, q_row0=0, Lq=L, segs=[(0, L), (n_lat, Lc)],
                        B=B, d_attn=d_attn, kv_w=kv_w)
        y_f = fourier(z_f, w_fourier[l], row0=0, Ls=L, B=B)
        if not last:
            y_a = (y_a, attention(qn, kn, z_qkv, q_row0=n_lat, Lq=Lc, segs=[(n_lat, Lc)],
                                  B=B, d_attn=d_attn, kv_w=kv_w))
            y_f = (y_f, fourier(z_f, w_fourier[l], row0=n_lat, Ls=Lc, B=B))
        y_c = conv_module(z_c, w_dw[l], b_dw[l], g_conv_ln[l], b_conv_ln[l], w_pw_bf, b_pw[l],
                          L=L, Lc=Lc, B=B, n_rows=n_rows)
        y_s = sgu_mixer(z_s, g_sgu[l], w_spatial[l], b_spatial[l], n_rows=n_rows)
        xa = out_proj([y_a, y_f, y_c, y_s], w_out_bf, xa, ga1, L=L, B=B, n_rows=n_rows)

        h2p, eidx, wsel, rank, cnt = router(xa, g_norm2[l], sh2, sc2, w_router[l], b_router[l],
                                            L=L, B=B, n_rows=n_rows)
        n_tiles = -(-n_rows * TOP_K // exp_tr) + n_exp
        pos, gid, nvalid, nused = _dispatch_plan(cnt[:, 0].astype(I32), eidx, rank, tr=exp_tr, n_tiles=n_tiles)
        xs = dispatch(pos, h2p, n_tiles * exp_tr)
        ys = experts(xs, gid, nvalid, nused, w_gate_e, w_up_e, w_down_e, l, tr=exp_tr)
        xa = combine(pos, xa, h2p, wsel.T, ga2, wgs_bf, wus_bf, wds_bf, ys, L=L, B=B, n_rows=n_rows)

    return xa.reshape(B, L, D)
```

```python
import functools

import jax
import jax.numpy as jnp
import numpy as np
from jax import lax
from jax.experimental import pallas as pl
from jax.experimental.pallas import tpu as pltpu

F32 = jnp.float32
BF16 = jnp.bfloat16
I32 = jnp.int32

HEAD_DIM = 128
GRID_W = 64
ROPE_THETA = 10000.0
NORM_EPS = 1e-6
GQA_GROUP = 4
N_FOURIER_GROUPS = 4
CONV_WIDTH = 31
CONV_PAD = CONV_WIDTH // 2
CONV_HALO = 16
SGU_CHUNK = 128
N_MOD = 6
TOP_K = 8
EXPERTS_PER_GROUP = 8
TOPK_GROUPS = 4
ROUTED_SCALE = 2.5
COND_ROWS = 8
ROW_GROUP = 8
LANES = 128
CONV_ROW_BLOCK = 64

VMEM_LIMIT = 56 * 1024 * 1024


def _cp(sem, vmem=VMEM_LIMIT, **kw):
    return pltpu.CompilerParams(dimension_semantics=sem, vmem_limit_bytes=vmem, **kw)


def _pick(n, pref, mult=8):
    if n <= pref:
        return n
    for t in range(pref - pref % mult, 0, -mult):
        if n % t == 0:
            return t
    raise ValueError(f"no tile for {n} <= {pref}")


def _rms(t, eps=NORM_EPS):
    return t * lax.rsqrt(jnp.mean(t * t, axis=-1, keepdims=True) + eps)


_HI16 = -65536


def _pack_halves(v):
    half = v.shape[1] // 2
    lo = lax.bitcast_convert_type(v[:, :half].astype(BF16).astype(F32), I32)
    hi = lax.bitcast_convert_type(v[:, half:].astype(BF16).astype(F32), I32)
    return lax.shift_right_logical(lo, 16) | (hi & _HI16)


def _unpack_halves(p):
    lo = lax.bitcast_convert_type(lax.shift_left(p, 16), F32)
    hi = lax.bitcast_convert_type(p & _HI16, F32)
    return lo, hi


def _cast_kernel(x_ref, o_ref):
    o_ref[...] = x_ref[...].astype(o_ref.dtype)


def cast_cols(w, l, col0, ncols, cb):
    _, R, _ = w.shape
    tr = _pick(R, 512)
    return pl.pallas_call(
        _cast_kernel,
        grid=(R // tr, ncols // cb),
        in_specs=[pl.BlockSpec((None, tr, cb), lambda i, j: (l, i, col0 // cb + j))],
        out_specs=pl.BlockSpec((tr, cb), lambda i, j: (i, j)),
        out_shape=jax.ShapeDtypeStruct((R, ncols), BF16),
        compiler_params=_cp(("parallel", "parallel")),
        name="cast_bf16",
    )(w)


def _modulation_kernel(c_ref, w_ref, b_ref, o_ref):
    s = jax.nn.silu(c_ref[...]).astype(BF16)
    o_ref[...] = jnp.dot(s, w_ref[...].astype(BF16), preferred_element_type=F32) + b_ref[...]


def modulation(cond, w_ada, b_ada):
    depth, D, N = w_ada.shape
    tn = _pick(N, 512, 128)
    return pl.pallas_call(
        _modulation_kernel,
        grid=(depth, N // tn),
        in_specs=[
            pl.BlockSpec((COND_ROWS, D), lambda l, j: (0, 0)),
            pl.BlockSpec((None, D, tn), lambda l, j: (l, 0, j)),
            pl.BlockSpec((None, 1, tn), lambda l, j: (l, 0, j)),
        ],
        out_specs=pl.BlockSpec((None, COND_ROWS, tn), lambda l, j: (l, 0, j)),
        out_shape=jax.ShapeDtypeStruct((depth, COND_ROWS, N), F32),
        compiler_params=_cp(("parallel", "parallel")),
        name="modulation",
    )(cond, w_ada, b_ada.reshape(depth, 1, N))


def _cond_row(i, tm, L, B):
    return jnp.minimum((i * tm) // L, B)


def _norm_mod_kernel(x_ref, g_ref, sh_ref, sc_ref, o_ref, *, tm, L, B):
    r = _cond_row(pl.program_id(0), tm, L, B)
    xn = _rms(x_ref[...]) * g_ref[...]
    o_ref[...] = (xn * (1.0 + sc_ref[pl.ds(r, 1), :]) + sh_ref[pl.ds(r, 1), :]).astype(o_ref.dtype)


def norm_mod(x, g, sh, sc, *, L, B):
    T, D = x.shape
    tm = _pick(L, 512)
    return pl.pallas_call(
        functools.partial(_norm_mod_kernel, tm=tm, L=L, B=B),
        grid=(T // tm,),
        in_specs=[
            pl.BlockSpec((tm, D), lambda i: (i, 0)),
            pl.BlockSpec((1, D), lambda i: (0, 0)),
            pl.BlockSpec((COND_ROWS, D), lambda i: (0, 0)),
            pl.BlockSpec((COND_ROWS, D), lambda i: (0, 0)),
        ],
        out_specs=pl.BlockSpec((tm, D), lambda i: (i, 0)),
        out_shape=jax.ShapeDtypeStruct((T, D), BF16),
        compiler_params=_cp(("parallel",)),
        name="norm_mod",
    )(x, g.reshape(1, D), sh, sc)


def _mm_kernel(a_ref, w_ref, o_ref, w_bf):
    @pl.when(pl.program_id(1) == 0)
    def _():
        w_bf[...] = w_ref[...].astype(BF16)

    o_ref[...] = jnp.dot(a_ref[...], w_bf[...], preferred_element_type=F32).astype(o_ref.dtype)


def matmul(a, w, l, col0, ncols, n_rows, tn, out_dtype=BF16):
    _, K = a.shape
    tm = _pick(n_rows, 1024)
    return pl.pallas_call(
        _mm_kernel,
        grid=(ncols // tn, n_rows // tm),
        in_specs=[
            pl.BlockSpec((tm, K), lambda j, i: (i, 0)),
            pl.BlockSpec((None, K, tn), lambda j, i: (l, 0, col0 // tn + j)),
        ],
        out_specs=pl.BlockSpec((tm, tn), lambda j, i: (i, j)),
        out_shape=jax.ShapeDtypeStruct((n_rows, ncols), out_dtype),
        scratch_shapes=[pltpu.VMEM((K, tn), BF16)],
        compiler_params=_cp(("arbitrary", "arbitrary")),
        name="in_proj",
    )(a, w)


def _rope_tables(L, tm):
    rows = L // GRID_W
    row = np.repeat(np.arange(rows), GRID_W).astype(np.float32)
    col = np.tile(np.arange(GRID_W), rows).astype(np.float32)
    axis_dim = HEAD_DIM // 2
    inv = (np.float32(ROPE_THETA) ** (-np.arange(0, axis_dim, 2, dtype=np.float32) / axis_dim)).astype(np.float32)
    ang = np.concatenate([row[:, None] * inv, col[:, None] * inv], axis=-1).astype(np.float64)
    cos = np.repeat(np.cos(ang), 2, axis=-1)
    sin = np.repeat(np.sin(ang), 2, axis=-1)
    sign = np.tile(np.array([-1.0, 1.0]), HEAD_DIM // 2)
    cos = np.concatenate([cos, np.ones((tm, HEAD_DIM))], axis=0)
    sin = np.concatenate([sin * sign, np.zeros((tm, HEAD_DIM))], axis=0)
    return jnp.asarray(cos, F32), jnp.asarray(sin, F32)


def _qk_kernel(q_ref, k_ref, cos_ref, sin_ref, gq_ref, gk_ref, qo_ref, ko_ref, *, n_q, n_kv):
    cos = cos_ref[...]
    sin = sin_ref[...]
    even = (lax.broadcasted_iota(I32, cos.shape, 1) & 1) == 0

    def norm_rope(t, g):
        n = _rms(t.astype(F32)) * g
        partner = jnp.where(even, pltpu.roll(n, HEAD_DIM - 1, 1), pltpu.roll(n, 1, 1))
        return n * cos + partner * sin

    q_scale = HEAD_DIM ** -0.5
    for h in range(n_q):
        sl = slice(h * HEAD_DIM, (h + 1) * HEAD_DIM)
        qo_ref[:, sl] = (norm_rope(q_ref[:, sl], gq_ref[...]) * q_scale).astype(qo_ref.dtype)
    for h in range(n_kv):
        sl = slice(h * HEAD_DIM, (h + 1) * HEAD_DIM)
        ko_ref[:, sl] = norm_rope(k_ref[:, sl], gk_ref[...]).astype(ko_ref.dtype)


def qk_norm_rope(z_qkv, g_q, g_k, *, L, B, d_attn, kv_w):
    T = z_qkv.shape[0]
    tm = _pick(L, 256)
    cos, sin = _rope_tables(L, tm)
    n_lat = B * L // tm
    per_seq = L // tm

    def tab_idx(i):
        return (jnp.where(i < n_lat, i % per_seq, per_seq), 0)

    return pl.pallas_call(
        functools.partial(_qk_kernel, n_q=d_attn // HEAD_DIM, n_kv=kv_w // HEAD_DIM),
        grid=(T // tm,),
        in_specs=[
            pl.BlockSpec((tm, d_attn), lambda i: (i, 0)),
            pl.BlockSpec((tm, kv_w), lambda i: (i, d_attn // kv_w)),
            pl.BlockSpec((tm, HEAD_DIM), tab_idx),
            pl.BlockSpec((tm, HEAD_DIM), tab_idx),
            pl.BlockSpec((1, HEAD_DIM), lambda i: (0, 0)),
            pl.BlockSpec((1, HEAD_DIM), lambda i: (0, 0)),
        ],
        out_specs=[
            pl.BlockSpec((tm, d_attn), lambda i: (i, 0)),
            pl.BlockSpec((tm, kv_w), lambda i: (i, 0)),
        ],
        out_shape=[
            jax.ShapeDtypeStruct((T, d_attn), BF16),
            jax.ShapeDtypeStruct((T, kv_w), BF16),
        ],
        compiler_params=_cp(("parallel",)),
        name="qk_norm_rope",
    )(z_qkv, z_qkv, cos, sin, g_q.reshape(1, HEAD_DIM), g_k.reshape(1, HEAD_DIM))


def _attn_kernel(*refs, n_seg):
    q_ref = refs[0]
    k_refs = refs[1:1 + n_seg]
    v_refs = refs[1 + n_seg:1 + 2 * n_seg]
    o_ref = refs[-1]
    for h in range(GQA_GROUP):
        sl = slice(h * HEAD_DIM, (h + 1) * HEAD_DIM)
        q = q_ref[:, sl]
        s = [lax.dot_general(q, k[...], (((1,), (1,)), ((), ())), preferred_element_type=F32)
             for k in k_refs]
        m = functools.reduce(jnp.maximum, [jnp.max(x, axis=-1, keepdims=True) for x in s])
        p = [jnp.exp(x - m) for x in s]
        den = functools.reduce(jnp.add, [jnp.sum(x, axis=-1, keepdims=True) for x in p])
        o = functools.reduce(jnp.add, [jnp.dot(x.astype(BF16), v[...], preferred_element_type=F32)
                                       for x, v in zip(p, v_refs)])
        o_ref[:, sl] = (o / den).astype(o_ref.dtype)


def attention(qn, kn, z_qkv, *, q_row0, Lq, segs, B, d_attn, kv_w):
    n_kv = kv_w // HEAD_DIM
    gw = GQA_GROUP * HEAD_DIM
    tq = _pick(Lq, 512)
    nq = Lq // tq
    v_col0 = (d_attn + kv_w) // HEAD_DIM

    def q_idx(b, g, qi):
        return (q_row0 // tq + b * nq + qi, g)

    in_specs = [pl.BlockSpec((tq, gw), q_idx)]
    args = [qn]
    for row0, ln in segs:
        in_specs.append(pl.BlockSpec((ln, HEAD_DIM), lambda b, g, qi, row0=row0, ln=ln: (row0 // ln + b, g)))
        args.append(kn)
    for row0, ln in segs:
        in_specs.append(
            pl.BlockSpec((ln, HEAD_DIM), lambda b, g, qi, row0=row0, ln=ln: (row0 // ln + b, v_col0 + g)))
        args.append(z_qkv)
    return pl.pallas_call(
        functools.partial(_attn_kernel, n_seg=len(segs)),
        grid=(B, n_kv, nq),
        in_specs=in_specs,
        out_specs=pl.BlockSpec((tq, gw), lambda b, g, qi: (b * nq + qi, g)),
        out_shape=jax.ShapeDtypeStruct((B * Lq, d_attn), BF16),
        compiler_params=_cp(("parallel", "parallel", "parallel")),
        name="attention",
    )(*args)


def _dft_tables(n, dtype):
    k = np.arange(n, dtype=np.int64)
    ang = ((k[:, None] * k[None, :]) % n).astype(np.float64) * (2.0 * np.pi / n)
    return jnp.asarray(np.cos(ang), dtype), jnp.asarray(np.sin(ang), dtype)


def _fourier_kernel(t_ref, cl_ref, sl_ref, cc_ref, sc_ref, w_ref, o_ref, *, scale):
    w = w_ref[...]
    hi = lax.Precision.HIGHEST
    mc = jnp.dot(cc_ref[...], w, precision=hi, preferred_element_type=F32).astype(BF16)
    ms = jnp.dot(sc_ref[...], w, precision=hi, preferred_element_type=F32).astype(BF16)
    t = t_ref[...]
    bc = jnp.dot(t, mc, preferred_element_type=F32).astype(BF16)
    bs = jnp.dot(t, ms, preferred_element_type=F32).astype(BF16)
    y = (jnp.dot(cl_ref[...], bc, preferred_element_type=F32)
         - jnp.dot(sl_ref[...], bs, preferred_element_type=F32))
    o_ref[...] = (y * scale).astype(o_ref.dtype)


def fourier(z_f, w_f, *, row0, Ls, B):
    ng, fg, _ = w_f.shape
    cl, sl = _dft_tables(Ls, BF16)
    cc, sc = _dft_tables(fg, F32)
    scale = 1.0 / float(Ls * fg) ** 0.5
    return pl.pallas_call(
        functools.partial(_fourier_kernel, scale=scale),
        grid=(B, ng),
        in_specs=[
            pl.BlockSpec((Ls, fg), lambda b, g: (row0 // Ls + b, g)),
            pl.BlockSpec((Ls, Ls), lambda b, g: (0, 0)),
            pl.BlockSpec((Ls, Ls), lambda b, g: (0, 0)),
            pl.BlockSpec((fg, fg), lambda b, g: (0, 0)),
            pl.BlockSpec((fg, fg), lambda b, g: (0, 0)),
            pl.BlockSpec((None, fg, fg), lambda b, g: (g, 0, 0)),
        ],
        out_specs=pl.BlockSpec((Ls, fg), lambda b, g: (b, g)),
        out_shape=jax.ShapeDtypeStruct((B * Ls, ng * fg), BF16),
        compiler_params=_cp(("parallel", "parallel")),
        name="fourier",
    )(z_f, cl, sl, cc, sc, w_f)


def _conv_kernel(cur_ref, prev_ref, next_ref, wdw_ref, bdw_ref, gln_ref, bln_ref, wpw_ref, bpw_ref,
                 o_ref, gbuf, shifted, cbuf, *, tr, dc, n_lat, per_lat, per_ctx):
    i = pl.program_id(0)
    j = jnp.where(i < n_lat, i % per_lat, (i - n_lat) % per_ctx)
    per = jnp.where(i < n_lat, per_lat, per_ctx)
    first = j == 0
    last = j == per - 1

    def glu(zb):
        return zb[:, :dc].astype(F32) * jax.nn.sigmoid(zb[:, dc:].astype(F32))

    gbuf[0:CONV_HALO, :] = jnp.where(first, 0.0, glu(prev_ref[...]))
    gbuf[CONV_HALO:CONV_HALO + tr, :] = glu(cur_ref[...])
    gbuf[CONV_HALO + tr:, :] = jnp.where(last, 0.0, glu(next_ref[...]))
    n_sh = tr + CONV_HALO + ROW_GROUP
    for s in range(1, ROW_GROUP):
        shifted[s - 1, :, :] = gbuf[s:s + n_sh, :]
    first_off = CONV_HALO - CONV_PAD
    rb = min(tr, CONV_ROW_BLOCK)
    for c in range(dc // LANES):
        cols = slice(c * LANES, (c + 1) * LANES)
        for r0 in range(0, tr, rb):
            a = jnp.broadcast_to(bdw_ref[:, cols], (rb, LANES))
            for k in range(CONV_WIDTH):
                off = first_off + k + r0
                s = off % ROW_GROUP
                src = gbuf[off:off + rb, cols] if s == 0 else shifted[s - 1, off - s:off - s + rb, cols]
                a = a + src * wdw_ref[k:k + 1, cols]
            cbuf[r0:r0 + rb, cols] = a
    acc = cbuf[...]
    mu = jnp.mean(acc, axis=-1, keepdims=True)
    cen = acc - mu
    var = jnp.mean(cen * cen, axis=-1, keepdims=True)
    hn = cen * lax.rsqrt(var + NORM_EPS) * gln_ref[...] + bln_ref[...]
    act = jax.nn.silu(hn).astype(BF16)
    o_ref[...] = (jnp.dot(act, wpw_ref[...], preferred_element_type=F32) + bpw_ref[...]).astype(o_ref.dtype)


def conv_module(z_c, w_dw, b_dw, g_ln, b_ln, w_pw_bf, b_pw, *, L, Lc, B, n_rows):
    dc = w_pw_bf.shape[0]
    tr = _pick(Lc, 256, CONV_HALO)
    hpt = tr // CONV_HALO
    n_halo = z_c.shape[0] // CONV_HALO
    w_pad = jnp.concatenate([w_dw, jnp.zeros((32 - CONV_WIDTH, dc), F32)], axis=0)
    row = lambda v: v.reshape(1, dc)
    return pl.pallas_call(
        functools.partial(_conv_kernel, tr=tr, dc=dc, n_lat=B * L // tr, per_lat=L // tr, per_ctx=Lc // tr),
        grid=(n_rows // tr,),
        in_specs=[
            pl.BlockSpec((tr, 2 * dc), lambda i: (i, 0)),
            pl.BlockSpec((CONV_HALO, 2 * dc), lambda i: (jnp.maximum(i * hpt - 1, 0), 0)),
            pl.BlockSpec((CONV_HALO, 2 * dc), lambda i: (jnp.minimum((i + 1) * hpt, n_halo - 1), 0)),
            pl.BlockSpec((32, dc), lambda i: (0, 0)),
            pl.BlockSpec((1, dc), lambda i: (0, 0)),
            pl.BlockSpec((1, dc), lambda i: (0, 0)),
            pl.BlockSpec((1, dc), lambda i: (0, 0)),
            pl.BlockSpec((dc, dc), lambda i: (0, 0)),
            pl.BlockSpec((1, dc), lambda i: (0, 0)),
        ],
        out_specs=pl.BlockSpec((tr, dc), lambda i: (i, 0)),
        out_shape=jax.ShapeDtypeStruct((n_rows, dc), BF16),
        scratch_shapes=[
            pltpu.VMEM((tr + 2 * CONV_HALO, dc), F32),
            pltpu.VMEM((ROW_GROUP - 1, tr + CONV_HALO + ROW_GROUP, dc), F32),
            pltpu.VMEM((tr, dc), F32),
        ],
        compiler_params=_cp(("parallel",)),
        name="conv_module",
    )(z_c, z_c, z_c, w_pad, row(b_dw), row(g_ln), row(b_ln), w_pw_bf, row(b_pw))


def _sgu_kernel(z_ref, g_ref, ws_ref, bs_ref, o_ref, *, tm, ds, n_heads):
    for h in range(n_heads):
        sl = slice(h * HEAD_DIM, (h + 1) * HEAD_DIM)
        u = jax.nn.gelu(z_ref[:, sl].astype(F32))
        v = jax.nn.gelu(z_ref[:, ds + h * HEAD_DIM:ds + (h + 1) * HEAD_DIM].astype(F32))
        vn = (_rms(v) * g_ref[:, sl]).astype(BF16)
        w = ws_ref[h].astype(BF16)
        b = bs_ref[:, h:h + 1]
        for c in range(tm // SGU_CHUNK):
            rows = slice(c * SGU_CHUNK, (c + 1) * SGU_CHUNK)
            s = jnp.dot(w, vn[rows, :], preferred_element_type=F32) + b
            o_ref[rows, sl] = (u[rows, :] * s).astype(o_ref.dtype)


def sgu_mixer(z_s, g_sgu, w_s, b_s, *, n_rows):
    n_heads = w_s.shape[0]
    ds = n_heads * HEAD_DIM
    tm = _pick(n_rows, 256, SGU_CHUNK)
    return pl.pallas_call(
        functools.partial(_sgu_kernel, tm=tm, ds=ds, n_heads=n_heads),
        grid=(n_rows // tm,),
        in_specs=[
            pl.BlockSpec((tm, 2 * ds), lambda i: (i, 0)),
            pl.BlockSpec((1, ds), lambda i: (0, 0)),
            pl.BlockSpec((n_heads, SGU_CHUNK, SGU_CHUNK), lambda i: (0, 0, 0)),
            pl.BlockSpec((SGU_CHUNK, n_heads), lambda i: (0, 0)),
        ],
        out_specs=pl.BlockSpec((tm, ds), lambda i: (i, 0)),
        out_shape=jax.ShapeDtypeStruct((n_rows, ds), BF16),
        compiler_params=_cp(("parallel",)),
        name="sgu_mixer",
    )(z_s, g_sgu.reshape(1, ds), w_s, b_s.T)


def _out_proj_kernel(*refs, tm, L, B, widths, split, n_lat_tiles):
    n_y = sum(2 if sp else 1 for sp in split)
    y_refs = refs[:n_y]
    w_ref, x_ref, ga_ref, o_ref = refs[n_y:]
    i = pl.program_id(1)
    r = _cond_row(i, tm, L, B)
    acc = None
    k0 = 0
    p = 0
    for wd, sp in zip(widths, split):
        if sp:
            y = jnp.where(i < n_lat_tiles, y_refs[p][...], y_refs[p + 1][...])
            p += 2
        else:
            y = y_refs[p][...]
            p += 1
        part = jnp.dot(y, w_ref[k0:k0 + wd, :], preferred_element_type=F32)
        acc = part if acc is None else acc + part
        k0 += wd
    o_ref[...] = x_ref[...] + ga_ref[pl.ds(r, 1), :] * acc


def out_proj(ys, w_out_bf, x, ga, *, L, B, n_rows):
    D = x.shape[1]
    tm = _pick(L, 512)
    tn = _pick(D, 1024, 128)
    n_lat_tiles = B * L // tm
    split = tuple(isinstance(y, tuple) for y in ys)
    widths = tuple((y[0] if sp else y).shape[1] for y, sp in zip(ys, split))
    in_specs, args = [], []
    for y, wd, sp in zip(ys, widths, split):
        if sp:
            in_specs.append(pl.BlockSpec((tm, wd), lambda j, i: (jnp.minimum(i, n_lat_tiles - 1), 0)))
            in_specs.append(pl.BlockSpec((tm, wd), lambda j, i: (jnp.maximum(i - n_lat_tiles, 0), 0)))
            args += [y[0], y[1]]
        else:
            in_specs.append(pl.BlockSpec((tm, wd), lambda j, i: (i, 0)))
            args.append(y)
    in_specs += [
        pl.BlockSpec((sum(widths), tn), lambda j, i: (0, j)),
        pl.BlockSpec((tm, tn), lambda j, i: (i, j)),
        pl.BlockSpec((COND_ROWS, tn), lambda j, i: (0, j)),
    ]
    return pl.pallas_call(
        functools.partial(_out_proj_kernel, tm=tm, L=L, B=B, widths=widths, split=split,
                          n_lat_tiles=n_lat_tiles),
        grid=(D // tn, n_rows // tm),
        in_specs=in_specs,
        out_specs=pl.BlockSpec((tm, tn), lambda j, i: (i, j)),
        out_shape=jax.ShapeDtypeStruct((n_rows, D), F32),
        compiler_params=_cp(("parallel", "parallel")),
        name="out_proj",
    )(*args, w_out_bf, x, ga)


def _router_kernel(x_ref, g_ref, sh_ref, sc_ref, wr_ref, br_ref,
                   h2_ref, eidx_ref, wsel_ref, rank_ref, cnt_ref, *, tm, L, B, n_exp):
    i = pl.program_id(0)
    r = _cond_row(i, tm, L, B)
    xn = _rms(x_ref[...]) * g_ref[...]
    h2 = xn * (1.0 + sc_ref[pl.ds(r, 1), :]) + sh_ref[pl.ds(r, 1), :]
    h2_ref[...] = _pack_halves(h2).reshape(h2_ref.shape)
    logits = lax.dot_general(wr_ref[...], h2, (((1,), (1,)), ((), ())),
                             precision=lax.Precision.HIGHEST, preferred_element_type=F32)
    s = jax.nn.sigmoid(logits)
    sel = s + br_ref[...]
    shape = (n_exp, tm)
    e_io = lax.broadcasted_iota(I32, shape, 0)
    e_f = e_io.astype(F32)
    k_io = e_io & (EXPERTS_PER_GROUP - 1)
    g_io = e_io >> 3
    n_groups = n_exp // EXPERTS_PER_GROUP
    neg = -jnp.inf

    def partner(v, d):
        up = pltpu.roll(v, n_exp - d, 0)
        dn = pltpu.roll(v, d, 0)
        return jnp.where((k_io & d) == 0, up, dn)

    def group_all(v, op):
        d = 1
        while d < EXPERTS_PER_GROUP:
            v = op(v, partner(v, d))
            d *= 2
        return v

    kf = k_io.astype(F32)
    m1 = group_all(sel, jnp.maximum)
    i1 = group_all(jnp.where(sel == m1, kf, float(EXPERTS_PER_GROUP)), jnp.minimum)
    m2 = group_all(jnp.where(kf == i1, neg, sel), jnp.maximum)
    gs = m1 + m2
    beaten = jnp.zeros(shape, I32)
    for d in range(1, n_groups):
        other = pltpu.roll(gs, n_exp - EXPERTS_PER_GROUP * d, 0)
        og = (g_io + d) & (n_groups - 1)
        beats = (other > gs) | ((other == gs) & (og < g_io))
        beaten = beaten + beats.astype(I32)
    val = jnp.where(beaten < TOPK_GROUPS, sel, neg)

    assign = jnp.zeros(shape, F32)
    w_rows = []
    idx_rows = []
    for j in range(TOP_K):
        m = jnp.max(val, axis=0, keepdims=True)
        idx = jnp.min(jnp.where(val == m, e_f, float(n_exp)), axis=0, keepdims=True)
        hit = e_f == idx
        w_rows.append(jnp.sum(jnp.where(hit, s, 0.0), axis=0, keepdims=True))
        val = jnp.where(hit, neg, val)
        assign = assign + hit.astype(F32)
        idx_rows.append(idx)
        eidx_ref[j:j + 1, :] = idx.astype(I32)
    w_sum = functools.reduce(jnp.add, w_rows)
    for j in range(TOP_K):
        wsel_ref[j:j + 1, :] = w_rows[j] / w_sum * ROUTED_SCALE

    @pl.when(i == 0)
    def _():
        cnt_ref[...] = jnp.zeros_like(cnt_ref)

    earlier = (lax.broadcasted_iota(I32, (tm, tm), 0) < lax.broadcasted_iota(I32, (tm, tm), 1)).astype(BF16)
    before = jnp.dot(assign.astype(BF16), earlier, preferred_element_type=F32) + cnt_ref[...]
    for j in range(TOP_K):
        hit = e_f == idx_rows[j]
        rank_ref[j:j + 1, :] = jnp.sum(jnp.where(hit, before, 0.0), axis=0, keepdims=True).astype(I32)
    cnt_ref[...] = cnt_ref[...] + jnp.sum(assign, axis=1, keepdims=True)


def router(x, g, sh, sc, w_router, b_router, *, L, B, n_rows):
    D = x.shape[1]
    n_exp = w_router.shape[1]
    tm = _pick(L, 512, 128)
    assert n_rows % tm == 0
    return pl.pallas_call(
        functools.partial(_router_kernel, tm=tm, L=L, B=B, n_exp=n_exp),
        grid=(n_rows // tm,),
        in_specs=[
            pl.BlockSpec((tm, D), lambda i: (i, 0)),
            pl.BlockSpec((1, D), lambda i: (0, 0)),
            pl.BlockSpec((COND_ROWS, D), lambda i: (0, 0)),
            pl.BlockSpec((COND_ROWS, D), lambda i: (0, 0)),
            pl.BlockSpec((n_exp, D), lambda i: (0, 0)),
            pl.BlockSpec((n_exp, 1), lambda i: (0, 0)),
        ],
        out_specs=[
            pl.BlockSpec((tm // ROW_GROUP, ROW_GROUP, D // 2), lambda i: (i, 0, 0)),
            pl.BlockSpec((TOP_K, tm), lambda i: (0, i)),
            pl.BlockSpec((TOP_K, tm), lambda i: (0, i)),
            pl.BlockSpec((TOP_K, tm), lambda i: (0, i)),
            pl.BlockSpec((n_exp, 1), lambda i: (0, 0)),
        ],
        out_shape=[
            jax.ShapeDtypeStruct((n_rows // ROW_GROUP, ROW_GROUP, D // 2), I32),
            jax.ShapeDtypeStruct((TOP_K, n_rows), I32),
            jax.ShapeDtypeStruct((TOP_K, n_rows), F32),
            jax.ShapeDtypeStruct((TOP_K, n_rows), I32),
            jax.ShapeDtypeStruct((n_exp, 1), F32),
        ],
        compiler_params=_cp(("arbitrary",)),
        name="router",
    )(x, g.reshape(1, D), sh, sc, w_router.T, b_router.reshape(n_exp, 1))


def _vrow(ref, group, sub):
    return ref.at[group, pl.ds(sub, 1)]


def _hrow(ref, row):
    return ref.at[pl.ds(row, 1)]


def _dispatch_kernel(pos_ref, h_ref, xs_ref, sem, *, tm):
    def issue(g, carry):
        base = g * ROW_GROUP
        for u in range(ROW_GROUP):
            for j in range(TOP_K):
                pltpu.make_async_copy(_vrow(h_ref, g, u), _hrow(xs_ref, pos_ref[j * tm + base + u]), sem).start()
        return carry

    lax.fori_loop(0, tm // ROW_GROUP, issue, 0)
    for j in range(TOP_K):
        pltpu.make_async_copy(xs_ref.at[pl.ds(0, tm)], xs_ref.at[pl.ds(0, tm)], sem).wait()


def dispatch(pos, h2p, n_slots, tm):
    G, _, W = h2p.shape
    T = G * ROW_GROUP
    return pl.pallas_call(
        functools.partial(_dispatch_kernel, tm=tm),
        grid=(T // tm,),
        in_specs=[
            pl.BlockSpec((TOP_K * tm,), lambda i: (i,), memory_space=pltpu.SMEM),
            pl.BlockSpec((tm // ROW_GROUP, ROW_GROUP, W), lambda i: (i, 0, 0)),
        ],
        out_specs=pl.BlockSpec(memory_space=pl.ANY),
        out_shape=jax.ShapeDtypeStruct((n_slots, W), I32),
        scratch_shapes=[pltpu.SemaphoreType.DMA(())],
        compiler_params=_cp(("arbitrary",), has_side_effects=True),
        name="dispatch",
    )(pos, h2p)


def _experts_kernel(gid_ref, nvalid_ref, nused_ref, xs_ref, wg_ref, wu_ref, wd_ref, o_ref,
                    wg_bf, wu_bf, wd_bf, *, tr):
    i = pl.program_id(0)
    changed = jnp.logical_or(i == 0, gid_ref[i] != gid_ref[jnp.maximum(i - 1, 0)])

    @pl.when(changed)
    def _():
        wg_bf[...] = wg_ref[...].astype(BF16)
        wu_bf[...] = wu_ref[...].astype(BF16)
        wd_bf[...] = wd_ref[...].astype(BF16)

    @pl.when(i < nused_ref[0])
    def _():
        half = xs_ref.shape[1]
        valid = lax.broadcasted_iota(I32, (tr, 1), 0) < nvalid_ref[i]
        lo, hi = _unpack_halves(xs_ref[...])
        lo = jnp.where(valid, lo, 0.0).astype(BF16)
        hi = jnp.where(valid, hi, 0.0).astype(BF16)

        def proj(w_bf):
            return (jnp.dot(lo, w_bf[:half, :], preferred_element_type=F32)
                    + jnp.dot(hi, w_bf[half:, :], preferred_element_type=F32))

        hh = (jax.nn.silu(proj(wg_bf)) * proj(wu_bf)).astype(BF16)
        o_ref[...] = _pack_halves(jnp.dot(hh, wd_bf[...], preferred_element_type=F32))


def experts(xs, gid, nvalid, nused, w_ge, w_ue, w_de, l, *, tr):
    P, W = xs.shape
    D = 2 * W
    F = w_ge.shape[3]
    n_tiles = P // tr

    def row_idx(i, gid, nvalid, nused):
        return (jnp.minimum(i, nused[0] - 1), 0)

    def w_idx(i, gid, nvalid, nused):
        return (l, gid[i], 0, 0)

    return pl.pallas_call(
        functools.partial(_experts_kernel, tr=tr),
        grid_spec=pltpu.PrefetchScalarGridSpec(
            num_scalar_prefetch=3,
            grid=(n_tiles,),
            in_specs=[
                pl.BlockSpec((tr, W), row_idx),
                pl.BlockSpec((None, None, D, F), w_idx),
                pl.BlockSpec((None, None, D, F), w_idx),
                pl.BlockSpec((None, None, F, D), w_idx),
            ],
            out_specs=pl.BlockSpec((tr, W), row_idx),
            scratch_shapes=[pltpu.VMEM((D, F), BF16), pltpu.VMEM((D, F), BF16), pltpu.VMEM((F, D), BF16)],
        ),
        out_shape=jax.ShapeDtypeStruct((P, W), I32),
        compiler_params=_cp(("arbitrary",)),
        name="experts",
    )(gid, nvalid, nused, xs, w_ge, w_ue, w_de)


def _combine_kernel(pos_ref, x_ref, h_ref, wb_ref, ga_ref, wgs_ref, wus_ref, wds_ref, ys_ref,
                    o_ref, gbuf, acc, sems, *, tm, L, B):
    r = _cond_row(pl.program_id(0), tm, L, B)
    half = h_ref.shape[2]
    n_groups = tm // ROW_GROUP

    def issue_group(j, slot, g):
        base = j * tm + g * ROW_GROUP
        for u in range(ROW_GROUP):
            pltpu.make_async_copy(_hrow(ys_ref, pos_ref[base + u]), _vrow(gbuf.at[slot], g, u),
                                  sems.at[slot]).start()

    def issue_first(g, carry):
        issue_group(0, 0, g)
        return carry

    lax.fori_loop(0, n_groups, issue_first, 0)

    lo, hi = _unpack_halves(h_ref[...].reshape(tm, half))
    lo = lo.astype(BF16)
    hi = hi.astype(BF16)

    def proj(w_ref):
        return (jnp.dot(lo, w_ref[:half, :], preferred_element_type=F32)
                + jnp.dot(hi, w_ref[half:, :], preferred_element_type=F32))

    hh = (jax.nn.silu(proj(wgs_ref)) * proj(wus_ref)).astype(BF16)
    acc[...] = jnp.dot(hh, wds_ref[...], preferred_element_type=F32).reshape(acc.shape)

    for j in range(TOP_K):
        slot = j % 2
        pltpu.make_async_copy(ys_ref.at[pl.ds(0, tm)], ys_ref.at[pl.ds(0, tm)], sems.at[slot]).wait()

        def accumulate(g, carry, j=j, slot=slot):
            y_lo, y_hi = _unpack_halves(gbuf[slot, g])
            w = jnp.concatenate([wb_ref[j, g]] * (half // LANES), axis=1)
            acc[g, :, :half] = acc[g, :, :half] + w * y_lo
            acc[g, :, half:] = acc[g, :, half:] + w * y_hi
            if j + 1 < TOP_K:
                issue_group(j + 1, 1 - slot, g)
            return carry

        lax.fori_loop(0, n_groups, accumulate, 0)

    o_ref[...] = x_ref[...] + ga_ref[pl.ds(r, 1), :] * acc[...].reshape(o_ref.shape)


def combine(pos, x, h2p, wsel, ga, wgs_bf, wus_bf, wds_bf, ys, *, L, B, n_rows, tm):
    D = x.shape[1]
    W = D // 2
    F = wgs_bf.shape[1]
    tg = tm // ROW_GROUP
    wb = jnp.broadcast_to(wsel[:, :, None], (TOP_K, n_rows, LANES)).reshape(TOP_K, n_rows // ROW_GROUP, ROW_GROUP, LANES)
    return pl.pallas_call(
        functools.partial(_combine_kernel, tm=tm, L=L, B=B),
        grid=(n_rows // tm,),
        in_specs=[
            pl.BlockSpec((TOP_K * tm,), lambda i: (i,), memory_space=pltpu.SMEM),
            pl.BlockSpec((tm, D), lambda i: (i, 0)),
            pl.BlockSpec((tg, ROW_GROUP, W), lambda i: (i, 0, 0)),
            pl.BlockSpec((TOP_K, tg, ROW_GROUP, LANES), lambda i: (0, i, 0, 0)),
            pl.BlockSpec((COND_ROWS, D), lambda i: (0, 0)),
            pl.BlockSpec((D, F), lambda i: (0, 0)),
            pl.BlockSpec((D, F), lambda i: (0, 0)),
            pl.BlockSpec((F, D), lambda i: (0, 0)),
            pl.BlockSpec(memory_space=pl.ANY),
        ],
        out_specs=pl.BlockSpec((tm, D), lambda i: (i, 0)),
        out_shape=jax.ShapeDtypeStruct((n_rows, D), F32),
        scratch_shapes=[pltpu.VMEM((2, tg, ROW_GROUP, W), I32), pltpu.VMEM((tg, ROW_GROUP, D), F32),
                        pltpu.SemaphoreType.DMA((2,))],
        compiler_params=_cp(("arbitrary",)),
        name="combine",
    )(pos, x, h2p, wb, ga, wgs_bf, wus_bf, wds_bf, ys)


def _dispatch_plan(cnt, eidx, rank, *, tr, n_tiles, tok_tile):
    n_exp = cnt.shape[0]
    e_ids = jnp.arange(n_exp, dtype=I32)
    tiles_e = (cnt + tr - 1) // tr
    tile_end = jnp.cumsum(tiles_e)
    tile_start = tile_end - tiles_e

    def lookup(table, idx):
        return jnp.sum(jnp.where(idx[..., None] == e_ids, table, 0), axis=-1)

    pos = lookup(tile_start * tr, eidx) + rank
    n_used = tile_end[-1]
    ti = jnp.arange(n_tiles, dtype=I32)
    ti_c = jnp.minimum(ti, n_used - 1)
    gid = jnp.sum((ti_c[:, None] >= tile_end[None, :]).astype(I32), axis=1)
    left = lookup(cnt, gid) - (ti - lookup(tile_start, gid)) * tr
    nvalid = jnp.where(ti < n_used, jnp.clip(left, 0, tr), 0)
    k, T = pos.shape
    pos = pos.astype(I32).reshape(k, T // tok_tile, tok_tile).transpose(1, 0, 2).reshape(-1)
    return pos, gid.astype(I32), nvalid.astype(I32), n_used.reshape(1).astype(I32)


def kernel(x, c, ctx, c_ctx, w_ada, b_ada, g_norm1, g_norm2, w_in, g_q, g_k, w_fourier, w_dw, b_dw, g_conv_ln, b_conv_ln, w_pw, b_pw, g_sgu, w_spatial, b_spatial, w_out, w_router, b_router, w_gate_e, w_up_e, w_down_e, w_gate_s, w_up_s, w_down_s):
    B, L, D = x.shape
    Lc = ctx.shape[1]
    depth = w_ada.shape[0]
    assert B + 1 <= COND_ROWS and L % Lc == 0 and L % GRID_W == 0
    d_fourier = w_fourier.shape[1] * w_fourier.shape[2]
    d_conv = w_pw.shape[1]
    d_sgu = g_sgu.shape[1]
    d_attn = D - d_fourier - d_conv - d_sgu
    kv_w = d_attn // GQA_GROUP
    v_end = d_attn + 2 * kv_w
    f_end = v_end + d_fourier
    c_end = f_end + 2 * d_conv
    d_in = c_end + 2 * d_sgu
    assert w_in.shape[2] == d_in
    n_exp = w_router.shape[2]
    n_lat = B * L
    T_all = n_lat + B * Lc
    cb = D // 8
    exp_tr = 256
    tok_tile = _pick(L, 256, 128)

    assert n_exp // EXPERTS_PER_GROUP == 8 and (B * Lc) % _pick(L, 512) == 0
    assert all(c % cb == 0 for c in (v_end, f_end, c_end, d_in))
    fh = w_gate_s.shape[2]

    cond = jnp.concatenate([c, c_ctx[None, :], jnp.zeros((COND_ROWS - B - 1, D), F32)], axis=0)
    mod = modulation(cond, w_ada, b_ada)
    mod = mod.reshape(depth, COND_ROWS, N_MOD, D).transpose(0, 2, 1, 3)

    xa = jnp.concatenate([x.reshape(n_lat, D), ctx.reshape(B * Lc, D)], axis=0)

    for l in range(depth):
        last = l == depth - 1
        n_rows = n_lat if last else T_all
        sh1, sc1, ga1, sh2, sc2, ga2 = (mod[l, m] for m in range(N_MOD))

        w_out_bf = cast_cols(w_out, l, 0, D, cb)
        w_pw_bf = cast_cols(w_pw, l, 0, d_conv, _pick(d_conv, 512, 128))
        wgs_bf = cast_cols(w_gate_s, l, 0, fh, fh)
        wus_bf = cast_cols(w_up_s, l, 0, fh, fh)
        wds_bf = cast_cols(w_down_s, l, 0, D, cb)

        h = norm_mod(xa, g_norm1[l], sh1, sc1, L=L, B=B)
        z_qkv = matmul(h, w_in, l, 0, v_end, T_all, cb)
        z_f = matmul(h, w_in, l, v_end, d_fourier, n_rows, cb)
        z_c = matmul(h, w_in, l, f_end, 2 * d_conv, n_rows, cb)
        z_s = matmul(h, w_in, l, c_end, 2 * d_sgu, n_rows, cb)
        qn, kn = qk_norm_rope(z_qkv, g_q[l], g_k[l], L=L, B=B, d_attn=d_attn, kv_w=kv_w)
        y_a = attention(qn, kn, z_qkv, q_row0=0, Lq=L, segs=[(0, L), (n_lat, Lc)],
                        B=B, d_attn=d_attn, kv_w=kv_w)
        y_f = fourier(z_f, w_fourier[l], row0=0, Ls=L, B=B)
        if not last:
            y_a = (y_a, attention(qn, kn, z_qkv, q_row0=n_lat, Lq=Lc, segs=[(n_lat, Lc)],
                                  B=B, d_attn=d_attn, kv_w=kv_w))
            y_f = (y_f, fourier(z_f, w_fourier[l], row0=n_lat, Ls=Lc, B=B))
        y_c = conv_module(z_c, w_dw[l], b_dw[l], g_conv_ln[l], b_conv_ln[l], w_pw_bf, b_pw[l],
                          L=L, Lc=Lc, B=B, n_rows=n_rows)
        y_s = sgu_mixer(z_s, g_sgu[l], w_spatial[l], b_spatial[l], n_rows=n_rows)
        xa = out_proj([y_a, y_f, y_c, y_s], w_out_bf, xa, ga1, L=L, B=B, n_rows=n_rows)

        h2p, eidx, wsel, rank, cnt = router(xa, g_norm2[l], sh2, sc2, w_router[l], b_router[l],
                                            L=L, B=B, n_rows=n_rows)
        n_tiles = -(-n_rows * TOP_K // exp_tr) + n_exp
        pos, gid, nvalid, nused = _dispatch_plan(
            cnt[:, 0].astype(I32), eidx, rank, tr=exp_tr, n_tiles=n_tiles, tok_tile=tok_tile)
        xs = dispatch(pos, h2p, n_tiles * exp_tr, tok_tile)
        ys = experts(xs, gid, nvalid, nused, w_gate_e, w_up_e, w_down_e, l, tr=exp_tr)
        xa = combine(pos, xa, h2p, wsel, ga2, wgs_bf, wus_bf, wds_bf, ys,
                     L=L, B=B, n_rows=n_rows, tm=tok_tile)

    return xa.reshape(B, L, D)
```

```python
import functools

import jax
import jax.numpy as jnp
import numpy as np
from jax import lax
from jax.experimental import pallas as pl
from jax.experimental.pallas import tpu as pltpu

F32 = jnp.float32
BF16 = jnp.bfloat16
I32 = jnp.int32

HEAD_DIM = 128
GRID_W = 64
ROPE_THETA = 10000.0
NORM_EPS = 1e-6
GQA_GROUP = 4
N_FOURIER_GROUPS = 4
CONV_WIDTH = 31
CONV_PAD = CONV_WIDTH // 2
CONV_HALO = 16
SGU_CHUNK = 128
N_MOD = 6
TOP_K = 8
EXPERTS_PER_GROUP = 8
TOPK_GROUPS = 4
ROUTED_SCALE = 2.5
COND_ROWS = 8
ROW_GROUP = 8
LANES = 128
CONV_ROW_BLOCK = 64

VMEM_LIMIT = 56 * 1024 * 1024


def _cp(sem, vmem=VMEM_LIMIT, **kw):
    return pltpu.CompilerParams(dimension_semantics=sem, vmem_limit_bytes=vmem, **kw)


def _pick(n, pref, mult=8):
    if n <= pref:
        return n
    for t in range(pref - pref % mult, 0, -mult):
        if n % t == 0:
            return t
    raise ValueError(f"no tile for {n} <= {pref}")


def _rms(t, eps=NORM_EPS):
    return t * lax.rsqrt(jnp.mean(t * t, axis=-1, keepdims=True) + eps)


_HI16 = -65536


def _pack_halves(v):
    half = v.shape[1] // 2
    lo = lax.bitcast_convert_type(v[:, :half].astype(BF16).astype(F32), I32)
    hi = lax.bitcast_convert_type(v[:, half:].astype(BF16).astype(F32), I32)
    return lax.shift_right_logical(lo, 16) | (hi & _HI16)


def _unpack_halves(p):
    lo = lax.bitcast_convert_type(lax.shift_left(p, 16), F32)
    hi = lax.bitcast_convert_type(p & _HI16, F32)
    return lo, hi


def _cast_kernel(x_ref, o_ref):
    o_ref[...] = x_ref[...].astype(o_ref.dtype)


def cast_cols(w, l, col0, ncols, cb):
    _, R, _ = w.shape
    tr = _pick(R, 512)
    return pl.pallas_call(
        _cast_kernel,
        grid=(R // tr, ncols // cb),
        in_specs=[pl.BlockSpec((None, tr, cb), lambda i, j: (l, i, col0 // cb + j))],
        out_specs=pl.BlockSpec((tr, cb), lambda i, j: (i, j)),
        out_shape=jax.ShapeDtypeStruct((R, ncols), BF16),
        compiler_params=_cp(("parallel", "parallel")),
        name="cast_bf16",
    )(w)


def _modulation_kernel(c_ref, w_ref, b_ref, o_ref):
    s = jax.nn.silu(c_ref[...]).astype(BF16)
    o_ref[...] = jnp.dot(s, w_ref[...].astype(BF16), preferred_element_type=F32) + b_ref[...]


def modulation(cond, w_ada, b_ada):
    depth, D, N = w_ada.shape
    tn = _pick(N, 512, 128)
    return pl.pallas_call(
        _modulation_kernel,
        grid=(depth, N // tn),
        in_specs=[
            pl.BlockSpec((COND_ROWS, D), lambda l, j: (0, 0)),
            pl.BlockSpec((None, D, tn), lambda l, j: (l, 0, j)),
            pl.BlockSpec((None, 1, tn), lambda l, j: (l, 0, j)),
        ],
        out_specs=pl.BlockSpec((None, COND_ROWS, tn), lambda l, j: (l, 0, j)),
        out_shape=jax.ShapeDtypeStruct((depth, COND_ROWS, N), F32),
        compiler_params=_cp(("parallel", "parallel")),
        name="modulation",
    )(cond, w_ada, b_ada.reshape(depth, 1, N))


def _cond_row(i, tm, L, B):
    return jnp.minimum((i * tm) // L, B)


def _norm_mod_kernel(x_ref, g_ref, sh_ref, sc_ref, o_ref, *, tm, L, B):
    r = _cond_row(pl.program_id(0), tm, L, B)
    xn = _rms(x_ref[...]) * g_ref[...]
    o_ref[...] = (xn * (1.0 + sc_ref[pl.ds(r, 1), :]) + sh_ref[pl.ds(r, 1), :]).astype(o_ref.dtype)


def norm_mod(x, g, sh, sc, *, L, B):
    T, D = x.shape
    tm = _pick(L, 512)
    return pl.pallas_call(
        functools.partial(_norm_mod_kernel, tm=tm, L=L, B=B),
        grid=(T // tm,),
        in_specs=[
            pl.BlockSpec((tm, D), lambda i: (i, 0)),
            pl.BlockSpec((1, D), lambda i: (0, 0)),
            pl.BlockSpec((COND_ROWS, D), lambda i: (0, 0)),
            pl.BlockSpec((COND_ROWS, D), lambda i: (0, 0)),
        ],
        out_specs=pl.BlockSpec((tm, D), lambda i: (i, 0)),
        out_shape=jax.ShapeDtypeStruct((T, D), BF16),
        compiler_params=_cp(("parallel",)),
        name="norm_mod",
    )(x, g.reshape(1, D), sh, sc)


def _mm_kernel(a_ref, w_ref, o_ref, w_bf):
    @pl.when(pl.program_id(1) == 0)
    def _():
        w_bf[...] = w_ref[...].astype(BF16)

    o_ref[...] = jnp.dot(a_ref[...], w_bf[...], preferred_element_type=F32).astype(o_ref.dtype)


def matmul(a, w, l, col0, ncols, n_rows, tn, out_dtype=BF16):
    _, K = a.shape
    tm = _pick(n_rows, 1024)
    return pl.pallas_call(
        _mm_kernel,
        grid=(ncols // tn, n_rows // tm),
        in_specs=[
            pl.BlockSpec((tm, K), lambda j, i: (i, 0)),
            pl.BlockSpec((None, K, tn), lambda j, i: (l, 0, col0 // tn + j)),
        ],
        out_specs=pl.BlockSpec((tm, tn), lambda j, i: (i, j)),
        out_shape=jax.ShapeDtypeStruct((n_rows, ncols), out_dtype),
        scratch_shapes=[pltpu.VMEM((K, tn), BF16)],
        compiler_params=_cp(("arbitrary", "arbitrary")),
        name="in_proj",
    )(a, w)


def _rope_tables(L, tm):
    rows = L // GRID_W
    row = np.repeat(np.arange(rows), GRID_W).astype(np.float32)
    col = np.tile(np.arange(GRID_W), rows).astype(np.float32)
    axis_dim = HEAD_DIM // 2
    inv = (np.float32(ROPE_THETA) ** (-np.arange(0, axis_dim, 2, dtype=np.float32) / axis_dim)).astype(np.float32)
    ang = np.concatenate([row[:, None] * inv, col[:, None] * inv], axis=-1).astype(np.float64)
    cos = np.repeat(np.cos(ang), 2, axis=-1)
    sin = np.repeat(np.sin(ang), 2, axis=-1)
    sign = np.tile(np.array([-1.0, 1.0]), HEAD_DIM // 2)
    cos = np.concatenate([cos, np.ones((tm, HEAD_DIM))], axis=0)
    sin = np.concatenate([sin * sign, np.zeros((tm, HEAD_DIM))], axis=0)
    return jnp.asarray(cos, F32), jnp.asarray(sin, F32)


def _qk_kernel(q_ref, k_ref, cos_ref, sin_ref, gq_ref, gk_ref, qo_ref, ko_ref, *, n_q, n_kv):
    cos = cos_ref[...]
    sin = sin_ref[...]
    even = (lax.broadcasted_iota(I32, cos.shape, 1) & 1) == 0

    def norm_rope(t, g):
        n = _rms(t.astype(F32)) * g
        partner = jnp.where(even, pltpu.roll(n, HEAD_DIM - 1, 1), pltpu.roll(n, 1, 1))
        return n * cos + partner * sin

    q_scale = HEAD_DIM ** -0.5
    for h in range(n_q):
        sl = slice(h * HEAD_DIM, (h + 1) * HEAD_DIM)
        qo_ref[:, sl] = (norm_rope(q_ref[:, sl], gq_ref[...]) * q_scale).astype(qo_ref.dtype)
    for h in range(n_kv):
        sl = slice(h * HEAD_DIM, (h + 1) * HEAD_DIM)
        ko_ref[:, sl] = norm_rope(k_ref[:, sl], gk_ref[...]).astype(ko_ref.dtype)


def qk_norm_rope(z_qkv, g_q, g_k, *, L, B, d_attn, kv_w):
    T = z_qkv.shape[0]
    tm = _pick(L, 256)
    cos, sin = _rope_tables(L, tm)
    n_lat = B * L // tm
    per_seq = L // tm

    def tab_idx(i):
        return (jnp.where(i < n_lat, i % per_seq, per_seq), 0)

    return pl.pallas_call(
        functools.partial(_qk_kernel, n_q=d_attn // HEAD_DIM, n_kv=kv_w // HEAD_DIM),
        grid=(T // tm,),
        in_specs=[
            pl.BlockSpec((tm, d_attn), lambda i: (i, 0)),
            pl.BlockSpec((tm, kv_w), lambda i: (i, d_attn // kv_w)),
            pl.BlockSpec((tm, HEAD_DIM), tab_idx),
            pl.BlockSpec((tm, HEAD_DIM), tab_idx),
            pl.BlockSpec((1, HEAD_DIM), lambda i: (0, 0)),
            pl.BlockSpec((1, HEAD_DIM), lambda i: (0, 0)),
        ],
        out_specs=[
            pl.BlockSpec((tm, d_attn), lambda i: (i, 0)),
            pl.BlockSpec((tm, kv_w), lambda i: (i, 0)),
        ],
        out_shape=[
            jax.ShapeDtypeStruct((T, d_attn), BF16),
            jax.ShapeDtypeStruct((T, kv_w), BF16),
        ],
        compiler_params=_cp(("parallel",)),
        name="qk_norm_rope",
    )(z_qkv, z_qkv, cos, sin, g_q.reshape(1, HEAD_DIM), g_k.reshape(1, HEAD_DIM))


def _attn_kernel(*refs, n_seg):
    q_ref = refs[0]
    k_refs = refs[1:1 + n_seg]
    v_refs = refs[1 + n_seg:1 + 2 * n_seg]
    o_ref = refs[-1]
    for h in range(GQA_GROUP):
        sl = slice(h * HEAD_DIM, (h + 1) * HEAD_DIM)
        q = q_ref[:, sl]
        s = [lax.dot_general(q, k[...], (((1,), (1,)), ((), ())), preferred_element_type=F32)
             for k in k_refs]
        m = functools.reduce(jnp.maximum, [jnp.max(x, axis=-1, keepdims=True) for x in s])
        p = [jnp.exp(x - m) for x in s]
        den = functools.reduce(jnp.add, [jnp.sum(x, axis=-1, keepdims=True) for x in p])
        o = functools.reduce(jnp.add, [jnp.dot(x.astype(BF16), v[...], preferred_element_type=F32)
                                       for x, v in zip(p, v_refs)])
        o_ref[:, sl] = (o / den).astype(o_ref.dtype)


def attention(qn, kn, z_qkv, *, q_row0, Lq, segs, B, d_attn, kv_w):
    n_kv = kv_w // HEAD_DIM
    gw = GQA_GROUP * HEAD_DIM
    tq = _pick(Lq, 512)
    nq = Lq // tq
    v_col0 = (d_attn + kv_w) // HEAD_DIM

    def q_idx(b, g, qi):
        return (q_row0 // tq + b * nq + qi, g)

    in_specs = [pl.BlockSpec((tq, gw), q_idx)]
    args = [qn]
    for row0, ln in segs:
        in_specs.append(pl.BlockSpec((ln, HEAD_DIM), lambda b, g, qi, row0=row0, ln=ln: (row0 // ln + b, g)))
        args.append(kn)
    for row0, ln in segs:
        in_specs.append(
            pl.BlockSpec((ln, HEAD_DIM), lambda b, g, qi, row0=row0, ln=ln: (row0 // ln + b, v_col0 + g)))
        args.append(z_qkv)
    return pl.pallas_call(
        functools.partial(_attn_kernel, n_seg=len(segs)),
        grid=(B, n_kv, nq),
        in_specs=in_specs,
        out_specs=pl.BlockSpec((tq, gw), lambda b, g, qi: (b * nq + qi, g)),
        out_shape=jax.ShapeDtypeStruct((B * Lq, d_attn), BF16),
        compiler_params=_cp(("parallel", "parallel", "parallel")),
        name="attention",
    )(*args)


def _dft_tables(n, dtype):
    k = np.arange(n, dtype=np.int64)
    ang = ((k[:, None] * k[None, :]) % n).astype(np.float64) * (2.0 * np.pi / n)
    return jnp.asarray(np.cos(ang), dtype), jnp.asarray(np.sin(ang), dtype)


def _fourier_kernel(t_ref, cl_ref, sl_ref, cc_ref, sc_ref, w_ref, o_ref, *, scale):
    w = w_ref[...]
    hi = lax.Precision.HIGHEST
    mc = jnp.dot(cc_ref[...], w, precision=hi, preferred_element_type=F32).astype(BF16)
    ms = jnp.dot(sc_ref[...], w, precision=hi, preferred_element_type=F32).astype(BF16)
    t = t_ref[...]
    bc = jnp.dot(t, mc, preferred_element_type=F32).astype(BF16)
    bs = jnp.dot(t, ms, preferred_element_type=F32).astype(BF16)
    y = (jnp.dot(cl_ref[...], bc, preferred_element_type=F32)
         - jnp.dot(sl_ref[...], bs, preferred_element_type=F32))
    o_ref[...] = (y * scale).astype(o_ref.dtype)


def fourier(z_f, w_f, *, row0, Ls, B):
    ng, fg, _ = w_f.shape
    cl, sl = _dft_tables(Ls, BF16)
    cc, sc = _dft_tables(fg, F32)
    scale = 1.0 / float(Ls * fg) ** 0.5
    return pl.pallas_call(
        functools.partial(_fourier_kernel, scale=scale),
        grid=(B, ng),
        in_specs=[
            pl.BlockSpec((Ls, fg), lambda b, g: (row0 // Ls + b, g)),
            pl.BlockSpec((Ls, Ls), lambda b, g: (0, 0)),
            pl.BlockSpec((Ls, Ls), lambda b, g: (0, 0)),
            pl.BlockSpec((fg, fg), lambda b, g: (0, 0)),
            pl.BlockSpec((fg, fg), lambda b, g: (0, 0)),
            pl.BlockSpec((None, fg, fg), lambda b, g: (g, 0, 0)),
        ],
        out_specs=pl.BlockSpec((Ls, fg), lambda b, g: (b, g)),
        out_shape=jax.ShapeDtypeStruct((B * Ls, ng * fg), BF16),
        compiler_params=_cp(("parallel", "parallel")),
        name="fourier",
    )(z_f, cl, sl, cc, sc, w_f)


def _conv_kernel(cur_ref, prev_ref, next_ref, wdw_ref, bdw_ref, gln_ref, bln_ref, wpw_ref, bpw_ref,
                 o_ref, gbuf, shifted, cbuf, *, tr, dc, n_lat, per_lat, per_ctx):
    i = pl.program_id(0)
    j = jnp.where(i < n_lat, i % per_lat, (i - n_lat) % per_ctx)
    per = jnp.where(i < n_lat, per_lat, per_ctx)
    first = j == 0
    last = j == per - 1

    def glu(zb):
        return zb[:, :dc].astype(F32) * jax.nn.sigmoid(zb[:, dc:].astype(F32))

    gbuf[0:CONV_HALO, :] = jnp.where(first, 0.0, glu(prev_ref[...]))
    gbuf[CONV_HALO:CONV_HALO + tr, :] = glu(cur_ref[...])
    gbuf[CONV_HALO + tr:, :] = jnp.where(last, 0.0, glu(next_ref[...]))
    n_sh = tr + CONV_HALO + ROW_GROUP
    for s in range(1, ROW_GROUP):
        shifted[s - 1, :, :] = gbuf[s:s + n_sh, :]
    first_off = CONV_HALO - CONV_PAD
    rb = min(tr, CONV_ROW_BLOCK)
    for c in range(dc // LANES):
        cols = slice(c * LANES, (c + 1) * LANES)
        for r0 in range(0, tr, rb):
            a = jnp.broadcast_to(bdw_ref[:, cols], (rb, LANES))
            for k in range(CONV_WIDTH):
                off = first_off + k + r0
                s = off % ROW_GROUP
                src = gbuf[off:off + rb, cols] if s == 0 else shifted[s - 1, off - s:off - s + rb, cols]
                a = a + src * wdw_ref[k:k + 1, cols]
            cbuf[r0:r0 + rb, cols] = a
    acc = cbuf[...]
    mu = jnp.mean(acc, axis=-1, keepdims=True)
    cen = acc - mu
    var = jnp.mean(cen * cen, axis=-1, keepdims=True)
    hn = cen * lax.rsqrt(var + NORM_EPS) * gln_ref[...] + bln_ref[...]
    act = jax.nn.silu(hn).astype(BF16)
    o_ref[...] = (jnp.dot(act, wpw_ref[...], preferred_element_type=F32) + bpw_ref[...]).astype(o_ref.dtype)


def conv_module(z_c, w_dw, b_dw, g_ln, b_ln, w_pw_bf, b_pw, *, L, Lc, B, n_rows):
    dc = w_pw_bf.shape[0]
    tr = _pick(Lc, 256, CONV_HALO)
    hpt = tr // CONV_HALO
    n_halo = z_c.shape[0] // CONV_HALO
    w_pad = jnp.concatenate([w_dw, jnp.zeros((32 - CONV_WIDTH, dc), F32)], axis=0)
    row = lambda v: v.reshape(1, dc)
    return pl.pallas_call(
        functools.partial(_conv_kernel, tr=tr, dc=dc, n_lat=B * L // tr, per_lat=L // tr, per_ctx=Lc // tr),
        grid=(n_rows // tr,),
        in_specs=[
            pl.BlockSpec((tr, 2 * dc), lambda i: (i, 0)),
            pl.BlockSpec((CONV_HALO, 2 * dc), lambda i: (jnp.maximum(i * hpt - 1, 0), 0)),
            pl.BlockSpec((CONV_HALO, 2 * dc), lambda i: (jnp.minimum((i + 1) * hpt, n_halo - 1), 0)),
            pl.BlockSpec((32, dc), lambda i: (0, 0)),
            pl.BlockSpec((1, dc), lambda i: (0, 0)),
            pl.BlockSpec((1, dc), lambda i: (0, 0)),
            pl.BlockSpec((1, dc), lambda i: (0, 0)),
            pl.BlockSpec((dc, dc), lambda i: (0, 0)),
            pl.BlockSpec((1, dc), lambda i: (0, 0)),
        ],
        out_specs=pl.BlockSpec((tr, dc), lambda i: (i, 0)),
        out_shape=jax.ShapeDtypeStruct((n_rows, dc), BF16),
        scratch_shapes=[
            pltpu.VMEM((tr + 2 * CONV_HALO, dc), F32),
            pltpu.VMEM((ROW_GROUP - 1, tr + CONV_HALO + ROW_GROUP, dc), F32),
            pltpu.VMEM((tr, dc), F32),
        ],
        compiler_params=_cp(("parallel",)),
        name="conv_module",
    )(z_c, z_c, z_c, w_pad, row(b_dw), row(g_ln), row(b_ln), w_pw_bf, row(b_pw))


def _sgu_kernel(z_ref, g_ref, ws_ref, bs_ref, o_ref, *, tm, ds, n_heads):
    for h in range(n_heads):
        sl = slice(h * HEAD_DIM, (h + 1) * HEAD_DIM)
        u = jax.nn.gelu(z_ref[:, sl].astype(F32))
        v = jax.nn.gelu(z_ref[:, ds + h * HEAD_DIM:ds + (h + 1) * HEAD_DIM].astype(F32))
        vn = (_rms(v) * g_ref[:, sl]).astype(BF16)
        w = ws_ref[h].astype(BF16)
        b = bs_ref[:, h:h + 1]
        for c in range(tm // SGU_CHUNK):
            rows = slice(c * SGU_CHUNK, (c + 1) * SGU_CHUNK)
            s = jnp.dot(w, vn[rows, :], preferred_element_type=F32) + b
            o_ref[rows, sl] = (u[rows, :] * s).astype(o_ref.dtype)


def sgu_mixer(z_s, g_sgu, w_s, b_s, *, n_rows):
    n_heads = w_s.shape[0]
    ds = n_heads * HEAD_DIM
    tm = _pick(n_rows, 256, SGU_CHUNK)
    return pl.pallas_call(
        functools.partial(_sgu_kernel, tm=tm, ds=ds, n_heads=n_heads),
        grid=(n_rows // tm,),
        in_specs=[
            pl.BlockSpec((tm, 2 * ds), lambda i: (i, 0)),
            pl.BlockSpec((1, ds), lambda i: (0, 0)),
            pl.BlockSpec((n_heads, SGU_CHUNK, SGU_CHUNK), lambda i: (0, 0, 0)),
            pl.BlockSpec((SGU_CHUNK, n_heads), lambda i: (0, 0)),
        ],
        out_specs=pl.BlockSpec((tm, ds), lambda i: (i, 0)),
        out_shape=jax.ShapeDtypeStruct((n_rows, ds), BF16),
        compiler_params=_cp(("parallel",)),
        name="sgu_mixer",
    )(z_s, g_sgu.reshape(1, ds), w_s, b_s.T)


def _out_proj_kernel(*refs, tm, L, B, widths, split, n_lat_tiles):
    n_y = sum(2 if sp else 1 for sp in split)
    y_refs = refs[:n_y]
    w_ref, x_ref, ga_ref, o_ref = refs[n_y:]
    i = pl.program_id(1)
    r = _cond_row(i, tm, L, B)
    acc = None
    k0 = 0
    p = 0
    for wd, sp in zip(widths, split):
        if sp:
            y = jnp.where(i < n_lat_tiles, y_refs[p][...], y_refs[p + 1][...])
            p += 2
        else:
            y = y_refs[p][...]
            p += 1
        part = jnp.dot(y, w_ref[k0:k0 + wd, :], preferred_element_type=F32)
        acc = part if acc is None else acc + part
        k0 += wd
    o_ref[...] = x_ref[...] + ga_ref[pl.ds(r, 1), :] * acc


def out_proj(ys, w_out_bf, x, ga, *, L, B, n_rows):
    D = x.shape[1]
    tm = _pick(L, 512)
    tn = _pick(D, 1024, 128)
    n_lat_tiles = B * L // tm
    split = tuple(isinstance(y, tuple) for y in ys)
    widths = tuple((y[0] if sp else y).shape[1] for y, sp in zip(ys, split))
    in_specs, args = [], []
    for y, wd, sp in zip(ys, widths, split):
        if sp:
            in_specs.append(pl.BlockSpec((tm, wd), lambda j, i: (jnp.minimum(i, n_lat_tiles - 1), 0)))
            in_specs.append(pl.BlockSpec((tm, wd), lambda j, i: (jnp.maximum(i - n_lat_tiles, 0), 0)))
            args += [y[0], y[1]]
        else:
            in_specs.append(pl.BlockSpec((tm, wd), lambda j, i: (i, 0)))
            args.append(y)
    in_specs += [
        pl.BlockSpec((sum(widths), tn), lambda j, i: (0, j)),
        pl.BlockSpec((tm, tn), lambda j, i: (i, j)),
        pl.BlockSpec((COND_ROWS, tn), lambda j, i: (0, j)),
    ]
    return pl.pallas_call(
        functools.partial(_out_proj_kernel, tm=tm, L=L, B=B, widths=widths, split=split,
                          n_lat_tiles=n_lat_tiles),
        grid=(D // tn, n_rows // tm),
        in_specs=in_specs,
        out_specs=pl.BlockSpec((tm, tn), lambda j, i: (i, j)),
        out_shape=jax.ShapeDtypeStruct((n_rows, D), F32),
        compiler_params=_cp(("parallel", "parallel")),
        name="out_proj",
    )(*args, w_out_bf, x, ga)


def _router_kernel(x_ref, g_ref, sh_ref, sc_ref, wr_ref, br_ref,
                   h2_ref, eidx_ref, wsel_ref, rank_ref, cnt_ref, *, tm, L, B, n_exp):
    i = pl.program_id(0)
    r = _cond_row(i, tm, L, B)
    xn = _rms(x_ref[...]) * g_ref[...]
    h2 = xn * (1.0 + sc_ref[pl.ds(r, 1), :]) + sh_ref[pl.ds(r, 1), :]
    h2_ref[...] = _pack_halves(h2).reshape(h2_ref.shape)
    logits = lax.dot_general(wr_ref[...], h2, (((1,), (1,)), ((), ())),
                             precision=lax.Precision.HIGHEST, preferred_element_type=F32)
    s = jax.nn.sigmoid(logits)
    sel = s + br_ref[...]
    shape = (n_exp, tm)
    e_io = lax.broadcasted_iota(I32, shape, 0)
    e_f = e_io.astype(F32)
    k_io = e_io & (EXPERTS_PER_GROUP - 1)
    g_io = e_io >> 3
    n_groups = n_exp // EXPERTS_PER_GROUP
    neg = -jnp.inf

    def partner(v, d):
        up = pltpu.roll(v, n_exp - d, 0)
        dn = pltpu.roll(v, d, 0)
        return jnp.where((k_io & d) == 0, up, dn)

    def group_all(v, op):
        d = 1
        while d < EXPERTS_PER_GROUP:
            v = op(v, partner(v, d))
            d *= 2
        return v

    kf = k_io.astype(F32)
    m1 = group_all(sel, jnp.maximum)
    i1 = group_all(jnp.where(sel == m1, kf, float(EXPERTS_PER_GROUP)), jnp.minimum)
    m2 = group_all(jnp.where(kf == i1, neg, sel), jnp.maximum)
    gs = m1 + m2
    beaten = jnp.zeros(shape, I32)
    for d in range(1, n_groups):
        other = pltpu.roll(gs, n_exp - EXPERTS_PER_GROUP * d, 0)
        og = (g_io + d) & (n_groups - 1)
        beats = (other > gs) | ((other == gs) & (og < g_io))
        beaten = beaten + beats.astype(I32)
    val = jnp.where(beaten < TOPK_GROUPS, sel, neg)

    assign = jnp.zeros(shape, F32)
    w_rows = []
    idx_rows = []
    for j in range(TOP_K):
        m = jnp.max(val, axis=0, keepdims=True)
        idx = jnp.min(jnp.where(val == m, e_f, float(n_exp)), axis=0, keepdims=True)
        hit = e_f == idx
        w_rows.append(jnp.sum(jnp.where(hit, s, 0.0), axis=0, keepdims=True))
        val = jnp.where(hit, neg, val)
        assign = assign + hit.astype(F32)
        idx_rows.append(idx)
        eidx_ref[j:j + 1, :] = idx.astype(I32)
    w_sum = functools.reduce(jnp.add, w_rows)
    for j in range(TOP_K):
        wsel_ref[j:j + 1, :] = w_rows[j] / w_sum * ROUTED_SCALE

    @pl.when(i == 0)
    def _():
        cnt_ref[...] = jnp.zeros_like(cnt_ref)

    earlier = (lax.broadcasted_iota(I32, (tm, tm), 0) < lax.broadcasted_iota(I32, (tm, tm), 1)).astype(BF16)
    before = jnp.dot(assign.astype(BF16), earlier, preferred_element_type=F32) + cnt_ref[...]
    for j in range(TOP_K):
        hit = e_f == idx_rows[j]
        rank_ref[j:j + 1, :] = jnp.sum(jnp.where(hit, before, 0.0), axis=0, keepdims=True).astype(I32)
    cnt_ref[...] = cnt_ref[...] + jnp.sum(assign, axis=1, keepdims=True)


def router(x, g, sh, sc, w_router, b_router, *, L, B, n_rows):
    D = x.shape[1]
    n_exp = w_router.shape[1]
    tm = _pick(L, 512, 128)
    assert n_rows % tm == 0
    return pl.pallas_call(
        functools.partial(_router_kernel, tm=tm, L=L, B=B, n_exp=n_exp),
        grid=(n_rows // tm,),
        in_specs=[
            pl.BlockSpec((tm, D), lambda i: (i, 0)),
            pl.BlockSpec((1, D), lambda i: (0, 0)),
            pl.BlockSpec((COND_ROWS, D), lambda i: (0, 0)),
            pl.BlockSpec((COND_ROWS, D), lambda i: (0, 0)),
            pl.BlockSpec((n_exp, D), lambda i: (0, 0)),
            pl.BlockSpec((n_exp, 1), lambda i: (0, 0)),
        ],
        out_specs=[
            pl.BlockSpec((tm // ROW_GROUP, ROW_GROUP, D // 2), lambda i: (i, 0, 0)),
            pl.BlockSpec((TOP_K, tm), lambda i: (0, i)),
            pl.BlockSpec((TOP_K, tm), lambda i: (0, i)),
            pl.BlockSpec((TOP_K, tm), lambda i: (0, i)),
            pl.BlockSpec((n_exp, 1), lambda i: (0, 0)),
        ],
        out_shape=[
            jax.ShapeDtypeStruct((n_rows // ROW_GROUP, ROW_GROUP, D // 2), I32),
            jax.ShapeDtypeStruct((TOP_K, n_rows), I32),
            jax.ShapeDtypeStruct((TOP_K, n_rows), F32),
            jax.ShapeDtypeStruct((TOP_K, n_rows), I32),
            jax.ShapeDtypeStruct((n_exp, 1), F32),
        ],
        compiler_params=_cp(("arbitrary",)),
        name="router",
    )(x, g.reshape(1, D), sh, sc, w_router.T, b_router.reshape(n_exp, 1))


def _vrow(ref, group, sub):
    return ref.at[group, pl.ds(sub, 1)]


def _hrow(ref, row):
    return ref.at[pl.ds(row, 1)]


def _dispatch_kernel(pos_ref, h_ref, xs_ref, sem, *, tm):
    def issue(g, carry):
        base = g * ROW_GROUP
        for u in range(ROW_GROUP):
            for j in range(TOP_K):
                pltpu.make_async_copy(_vrow(h_ref, g, u), _hrow(xs_ref, pos_ref[j * tm + base + u]), sem).start()
        return carry

    lax.fori_loop(0, tm // ROW_GROUP, issue, 0)
    for j in range(TOP_K):
        pltpu.make_async_copy(xs_ref.at[pl.ds(0, tm)], xs_ref.at[pl.ds(0, tm)], sem).wait()


def dispatch(pos, h2p, n_slots, tm):
    G, _, W = h2p.shape
    T = G * ROW_GROUP
    return pl.pallas_call(
        functools.partial(_dispatch_kernel, tm=tm),
        grid=(T // tm,),
        in_specs=[
            pl.BlockSpec((TOP_K * tm,), lambda i: (i,), memory_space=pltpu.SMEM),
            pl.BlockSpec((tm // ROW_GROUP, ROW_GROUP, W), lambda i: (i, 0, 0)),
        ],
        out_specs=pl.BlockSpec(memory_space=pl.ANY),
        out_shape=jax.ShapeDtypeStruct((n_slots, W), I32),
        scratch_shapes=[pltpu.SemaphoreType.DMA(())],
        compiler_params=_cp(("arbitrary",), has_side_effects=True),
        name="dispatch",
    )(pos, h2p)


def _experts_kernel(gid_ref, nvalid_ref, nused_ref, wslot_ref, nxt_ref, xs_ref, wg_hbm, wu_hbm, wd_hbm, ys_ref,
                    wg_st, wu_st, wd_st, wg_bf, wu_bf, wd_bf, obuf, wsem, osem, *, tr, l, n_tiles):
    i = pl.program_id(0)
    n_used = nused_ref[0]
    first = jnp.logical_or(i == 0, gid_ref[i] != gid_ref[jnp.maximum(i - 1, 0)])
    n_lane_tiles = obuf.shape[2] // LANES

    def weight_copies(e, slot):
        return [pltpu.make_async_copy(src.at[l, e], dst.at[slot], wsem.at[slot])
                for src, dst in ((wg_hbm, wg_st), (wu_hbm, wu_st), (wd_hbm, wd_st))]

    def out_copies(tile, slot):
        return [pltpu.make_async_copy(obuf.at[slot, :, pl.ds(c * LANES, LANES)],
                                      ys_ref.at[pl.ds(tile * tr, tr), c], osem.at[slot])
                for c in range(n_lane_tiles)]

    @pl.when(i == 0)
    def _():
        for cp in weight_copies(gid_ref[0], wslot_ref[0]):
            cp.start()

    @pl.when(first)
    def _():
        slot = wslot_ref[i]
        for cp in weight_copies(gid_ref[i], slot):
            cp.wait()
        wg_bf[...] = wg_st[slot].astype(BF16)
        wu_bf[...] = wu_st[slot].astype(BF16)
        wd_bf[...] = wd_st[slot].astype(BF16)

        @pl.when(nxt_ref[i] >= 0)
        def _():
            for cp in weight_copies(nxt_ref[i], 1 - slot):
                cp.start()

    @pl.when(jnp.logical_and(i >= 2, i - 2 < n_used))
    def _():
        for cp in out_copies(i - 2, i % 2):
            cp.wait()

    @pl.when(i < n_used)
    def _():
        half = xs_ref.shape[1]
        valid = lax.broadcasted_iota(I32, (tr, 1), 0) < nvalid_ref[i]
        lo, hi = _unpack_halves(xs_ref[...])
        lo = jnp.where(valid, lo, 0.0).astype(BF16)
        hi = jnp.where(valid, hi, 0.0).astype(BF16)

        def proj(w_bf):
            return (jnp.dot(lo, w_bf[:half, :], preferred_element_type=F32)
                    + jnp.dot(hi, w_bf[half:, :], preferred_element_type=F32))

        hh = (jax.nn.silu(proj(wg_bf)) * proj(wu_bf)).astype(BF16)
        slot = i % 2
        obuf[slot] = _pack_halves(jnp.dot(hh, wd_bf[...], preferred_element_type=F32))
        for cp in out_copies(i, slot):
            cp.start()

    @pl.when(i == n_tiles - 1)
    def _():
        @pl.when(jnp.logical_and(i >= 1, i - 1 < n_used))
        def _():
            for cp in out_copies(i - 1, (i - 1) % 2):
                cp.wait()

        @pl.when(i < n_used)
        def _():
            for cp in out_copies(i, i % 2):
                cp.wait()


def experts(xs, gid, nvalid, nused, wslot, nxt, w_ge, w_ue, w_de, l, *, tr):
    P, W = xs.shape
    D = 2 * W
    F = w_ge.shape[3]
    n_tiles = P // tr

    def row_idx(i, gid, nvalid, nused, wslot, nxt):
        return (jnp.minimum(i, nused[0] - 1), 0)

    return pl.pallas_call(
        functools.partial(_experts_kernel, tr=tr, l=l, n_tiles=n_tiles),
        grid_spec=pltpu.PrefetchScalarGridSpec(
            num_scalar_prefetch=5,
            grid=(n_tiles,),
            in_specs=[
                pl.BlockSpec((tr, W), row_idx),
                pl.BlockSpec(memory_space=pl.ANY),
                pl.BlockSpec(memory_space=pl.ANY),
                pl.BlockSpec(memory_space=pl.ANY),
            ],
            out_specs=pl.BlockSpec(memory_space=pl.ANY),
            scratch_shapes=[
                pltpu.VMEM((2, D, F), F32), pltpu.VMEM((2, D, F), F32), pltpu.VMEM((2, F, D), F32),
                pltpu.VMEM((D, F), BF16), pltpu.VMEM((D, F), BF16), pltpu.VMEM((F, D), BF16),
                pltpu.VMEM((2, tr, W), I32),
                pltpu.SemaphoreType.DMA((2,)), pltpu.SemaphoreType.DMA((2,)),
            ],
        ),
        out_shape=jax.ShapeDtypeStruct((P, W // LANES, LANES), I32),
        compiler_params=_cp(("arbitrary",), has_side_effects=True),
        name="experts",
    )(gid, nvalid, nused, wslot, nxt, xs, w_ge, w_ue, w_de)


def _combine_kernel(pos_ref, w_ref, x_ref, h_ref, ga_ref, wgs_ref, wus_ref, wds_ref, ys_ref,
                    o_ref, gbuf, acc, sem, *, tm, L, B):
    r = _cond_row(pl.program_id(0), tm, L, B)
    half = h_ref.shape[2]
    n_slab = gbuf.shape[2]

    def issue(t, carry):
        for j in range(TOP_K):
            pltpu.make_async_copy(ys_ref.at[pos_ref[j * tm + t]], gbuf.at[j, t], sem).start()
        return carry

    lax.fori_loop(0, tm, issue, 0, unroll=2)

    lo, hi = _unpack_halves(h_ref[...].reshape(tm, half))
    lo = lo.astype(BF16)
    hi = hi.astype(BF16)

    def proj(w_bf):
        return (jnp.dot(lo, w_bf[:half, :], preferred_element_type=F32)
                + jnp.dot(hi, w_bf[half:, :], preferred_element_type=F32))

    hh = (jax.nn.silu(proj(wgs_ref)) * proj(wus_ref)).astype(BF16)
    shared = jnp.dot(hh, wds_ref[...], preferred_element_type=F32)

    for j in range(TOP_K):
        pltpu.make_async_copy(ys_ref.at[pl.ds(0, tm)], gbuf.at[j], sem).wait()

    def token(t, carry):
        a_lo = jnp.zeros((n_slab, LANES), F32)
        a_hi = jnp.zeros((n_slab, LANES), F32)
        for j in range(TOP_K):
            y_lo, y_hi = _unpack_halves(gbuf[j, t])
            w = w_ref[j * tm + t]
            a_lo = a_lo + w * y_lo
            a_hi = a_hi + w * y_hi
        acc[t, 0:n_slab, :] = a_lo
        acc[t, n_slab:2 * n_slab, :] = a_hi
        return carry

    lax.fori_loop(0, tm, token, 0, unroll=2)

    ga = ga_ref[pl.ds(r, 1), :]
    for c in range(2 * n_slab):
        cols = slice(c * LANES, (c + 1) * LANES)
        o_ref[:, cols] = x_ref[:, cols] + ga[:, cols] * (shared[:, cols] + acc[:, c, :])


def combine(pos, x, h2p, wsel_flat, ga, wgs_bf, wus_bf, wds_bf, ys, *, L, B, n_rows, tm):
    D = x.shape[1]
    W = D // 2
    F = wgs_bf.shape[1]
    n_slab = W // LANES
    return pl.pallas_call(
        functools.partial(_combine_kernel, tm=tm, L=L, B=B),
        grid=(n_rows // tm,),
        in_specs=[
            pl.BlockSpec((TOP_K * tm,), lambda i: (i,), memory_space=pltpu.SMEM),
            pl.BlockSpec((TOP_K * tm,), lambda i: (i,), memory_space=pltpu.SMEM),
            pl.BlockSpec((tm, D), lambda i: (i, 0)),
            pl.BlockSpec((tm // ROW_GROUP, ROW_GROUP, W), lambda i: (i, 0, 0)),
            pl.BlockSpec((COND_ROWS, D), lambda i: (0, 0)),
            pl.BlockSpec((D, F), lambda i: (0, 0)),
            pl.BlockSpec((D, F), lambda i: (0, 0)),
            pl.BlockSpec((F, D), lambda i: (0, 0)),
            pl.BlockSpec(memory_space=pl.ANY),
        ],
        out_specs=pl.BlockSpec((tm, D), lambda i: (i, 0)),
        out_shape=jax.ShapeDtypeStruct((n_rows, D), F32),
        scratch_shapes=[pltpu.VMEM((TOP_K, tm, n_slab, LANES), I32), pltpu.VMEM((tm, 2 * n_slab, LANES), F32),
                        pltpu.SemaphoreType.DMA(())],
        compiler_params=_cp(("arbitrary",)),
        name="combine",
    )(pos, wsel_flat, x, h2p, ga, wgs_bf, wus_bf, wds_bf, ys)


def _tile_major(a, tile):
    k, T = a.shape
    return a.reshape(k, T // tile, tile).transpose(1, 0, 2).reshape(-1)


def _dispatch_plan(cnt, eidx, rank, *, tr, n_tiles):
    n_exp = cnt.shape[0]
    e_ids = jnp.arange(n_exp, dtype=I32)
    tiles_e = (cnt + tr - 1) // tr
    tile_end = jnp.cumsum(tiles_e)
    tile_start = tile_end - tiles_e

    def lookup(table, idx, keys=e_ids):
        return jnp.sum(jnp.where(idx[..., None] == keys, table, 0), axis=-1)

    pos = lookup(tile_start * tr, eidx) + rank
    n_used = tile_end[-1]
    ti = jnp.arange(n_tiles, dtype=I32)
    ti_c = jnp.minimum(ti, n_used - 1)
    gid = jnp.sum((ti_c[:, None] >= tile_end[None, :]).astype(I32), axis=1)
    left = lookup(cnt, gid) - (ti - lookup(tile_start, gid)) * tr
    nvalid = jnp.where(ti < n_used, jnp.clip(left, 0, tr), 0)
    changed = jnp.concatenate([jnp.ones((1,), I32), (gid[1:] != gid[:-1]).astype(I32)])
    wslot = (jnp.cumsum(changed) - 1) % 2
    after = lookup(tile_end, gid)
    nxt = jnp.where(after < n_used, lookup(gid, jnp.minimum(after, n_tiles - 1), ti), -1)
    as_i32 = lambda a: a.astype(I32)
    return (as_i32(pos), as_i32(gid), as_i32(nvalid), n_used.reshape(1).astype(I32), as_i32(wslot), as_i32(nxt))


def kernel(x, c, ctx, c_ctx, w_ada, b_ada, g_norm1, g_norm2, w_in, g_q, g_k, w_fourier, w_dw, b_dw, g_conv_ln, b_conv_ln, w_pw, b_pw, g_sgu, w_spatial, b_spatial, w_out, w_router, b_router, w_gate_e, w_up_e, w_down_e, w_gate_s, w_up_s, w_down_s):
    B, L, D = x.shape
    Lc = ctx.shape[1]
    depth = w_ada.shape[0]
    assert B + 1 <= COND_ROWS and L % Lc == 0 and L % GRID_W == 0
    d_fourier = w_fourier.shape[1] * w_fourier.shape[2]
    d_conv = w_pw.shape[1]
    d_sgu = g_sgu.shape[1]
    d_attn = D - d_fourier - d_conv - d_sgu
    kv_w = d_attn // GQA_GROUP
    v_end = d_attn + 2 * kv_w
    f_end = v_end + d_fourier
    c_end = f_end + 2 * d_conv
    d_in = c_end + 2 * d_sgu
    assert w_in.shape[2] == d_in
    n_exp = w_router.shape[2]
    n_lat = B * L
    T_all = n_lat + B * Lc
    cb = D // 8
    exp_tr = 256
    tok_tile = _pick(L, 256, 128)
    comb_tile = 128

    assert n_exp // EXPERTS_PER_GROUP == 8 and (B * Lc) % _pick(L, 512) == 0
    assert all(c % cb == 0 for c in (v_end, f_end, c_end, d_in))
    fh = w_gate_s.shape[2]

    cond = jnp.concatenate([c, c_ctx[None, :], jnp.zeros((COND_ROWS - B - 1, D), F32)], axis=0)
    mod = modulation(cond, w_ada, b_ada)
    mod = mod.reshape(depth, COND_ROWS, N_MOD, D).transpose(0, 2, 1, 3)

    xa = jnp.concatenate([x.reshape(n_lat, D), ctx.reshape(B * Lc, D)], axis=0)

    for l in range(depth):
        last = l == depth - 1
        n_rows = n_lat if last else T_all
        sh1, sc1, ga1, sh2, sc2, ga2 = (mod[l, m] for m in range(N_MOD))

        w_out_bf = cast_cols(w_out, l, 0, D, cb)
        w_pw_bf = cast_cols(w_pw, l, 0, d_conv, _pick(d_conv, 512, 128))
        wgs_bf = cast_cols(w_gate_s, l, 0, fh, fh)
        wus_bf = cast_cols(w_up_s, l, 0, fh, fh)
        wds_bf = cast_cols(w_down_s, l, 0, D, cb)

        h = norm_mod(xa, g_norm1[l], sh1, sc1, L=L, B=B)
        z_qkv = matmul(h, w_in, l, 0, v_end, T_all, cb)
        z_f = matmul(h, w_in, l, v_end, d_fourier, n_rows, cb)
        z_c = matmul(h, w_in, l, f_end, 2 * d_conv, n_rows, cb)
        z_s = matmul(h, w_in, l, c_end, 2 * d_sgu, n_rows, cb)
        qn, kn = qk_norm_rope(z_qkv, g_q[l], g_k[l], L=L, B=B, d_attn=d_attn, kv_w=kv_w)
        y_a = attention(qn, kn, z_qkv, q_row0=0, Lq=L, segs=[(0, L), (n_lat, Lc)],
                        B=B, d_attn=d_attn, kv_w=kv_w)
        y_f = fourier(z_f, w_fourier[l], row0=0, Ls=L, B=B)
        if not last:
            y_a = (y_a, attention(qn, kn, z_qkv, q_row0=n_lat, Lq=Lc, segs=[(n_lat, Lc)],
                                  B=B, d_attn=d_attn, kv_w=kv_w))
            y_f = (y_f, fourier(z_f, w_fourier[l], row0=n_lat, Ls=Lc, B=B))
        y_c = conv_module(z_c, w_dw[l], b_dw[l], g_conv_ln[l], b_conv_ln[l], w_pw_bf, b_pw[l],
                          L=L, Lc=Lc, B=B, n_rows=n_rows)
        y_s = sgu_mixer(z_s, g_sgu[l], w_spatial[l], b_spatial[l], n_rows=n_rows)
        xa = out_proj([y_a, y_f, y_c, y_s], w_out_bf, xa, ga1, L=L, B=B, n_rows=n_rows)

        h2p, eidx, wsel, rank, cnt = router(xa, g_norm2[l], sh2, sc2, w_router[l], b_router[l],
                                            L=L, B=B, n_rows=n_rows)
        n_tiles = -(-n_rows * TOP_K // exp_tr) + n_exp
        pos, gid, nvalid, nused, wslot, nxt = _dispatch_plan(
            cnt[:, 0].astype(I32), eidx, rank, tr=exp_tr, n_tiles=n_tiles)
        xs = dispatch(_tile_major(pos, tok_tile), h2p, n_tiles * exp_tr, tok_tile)
        ys = experts(xs, gid, nvalid, nused, wslot, nxt, w_gate_e, w_up_e, w_down_e, l, tr=exp_tr)
        xa = combine(_tile_major(pos, comb_tile), xa, h2p, _tile_major(wsel, comb_tile), ga2,
                     wgs_bf, wus_bf, wds_bf, ys, L=L, B=B, n_rows=n_rows, tm=comb_tile)

    return xa.reshape(B, L, D)
```

```python
import functools

import jax
import jax.numpy as jnp
import numpy as np
from jax import lax
from jax.experimental import pallas as pl
from jax.experimental.pallas import tpu as pltpu

F32 = jnp.float32
BF16 = jnp.bfloat16
I32 = jnp.int32

HEAD_DIM = 128
GRID_W = 64
ROPE_THETA = 10000.0
NORM_EPS = 1e-6
GQA_GROUP = 4
N_FOURIER_GROUPS = 4
CONV_WIDTH = 31
CONV_PAD = CONV_WIDTH // 2
CONV_HALO = 16
SGU_CHUNK = 128
N_MOD = 6
TOP_K = 8
EXPERTS_PER_GROUP = 8
TOPK_GROUPS = 4
ROUTED_SCALE = 2.5
COND_ROWS = 8
ROW_GROUP = 8
LANES = 128
CONV_ROW_BLOCK = 64

VMEM_LIMIT = 56 * 1024 * 1024


def _cp(sem, vmem=VMEM_LIMIT, **kw):
    return pltpu.CompilerParams(dimension_semantics=sem, vmem_limit_bytes=vmem, **kw)


def _pick(n, pref, mult=8):
    if n <= pref:
        return n
    for t in range(pref - pref % mult, 0, -mult):
        if n % t == 0:
            return t
    raise ValueError(f"no tile for {n} <= {pref}")


def _rms(t, eps=NORM_EPS):
    return t * lax.rsqrt(jnp.mean(t * t, axis=-1, keepdims=True) + eps)


_HI16 = -65536


def _pack_halves(v):
    half = v.shape[1] // 2
    lo = lax.bitcast_convert_type(v[:, :half].astype(BF16).astype(F32), I32)
    hi = lax.bitcast_convert_type(v[:, half:].astype(BF16).astype(F32), I32)
    return lax.shift_right_logical(lo, 16) | (hi & _HI16)


def _unpack_halves(p):
    lo = lax.bitcast_convert_type(lax.shift_left(p, 16), F32)
    hi = lax.bitcast_convert_type(p & _HI16, F32)
    return lo, hi


def _cast_kernel(x_ref, o_ref):
    o_ref[...] = x_ref[...].astype(o_ref.dtype)


def cast_cols(w, l, col0, ncols, cb):
    _, R, _ = w.shape
    tr = _pick(R, 512)
    return pl.pallas_call(
        _cast_kernel,
        grid=(R // tr, ncols // cb),
        in_specs=[pl.BlockSpec((None, tr, cb), lambda i, j: (l, i, col0 // cb + j))],
        out_specs=pl.BlockSpec((tr, cb), lambda i, j: (i, j)),
        out_shape=jax.ShapeDtypeStruct((R, ncols), BF16),
        compiler_params=_cp(("parallel", "parallel")),
        name="cast_bf16",
    )(w)


def _modulation_kernel(c_ref, w_ref, b_ref, o_ref):
    s = jax.nn.silu(c_ref[...]).astype(BF16)
    o_ref[...] = jnp.dot(s, w_ref[...].astype(BF16), preferred_element_type=F32) + b_ref[...]


def modulation(cond, w_ada, b_ada):
    depth, D, N = w_ada.shape
    tn = _pick(N, 512, 128)
    return pl.pallas_call(
        _modulation_kernel,
        grid=(depth, N // tn),
        in_specs=[
            pl.BlockSpec((COND_ROWS, D), lambda l, j: (0, 0)),
            pl.BlockSpec((None, D, tn), lambda l, j: (l, 0, j)),
            pl.BlockSpec((None, 1, tn), lambda l, j: (l, 0, j)),
        ],
        out_specs=pl.BlockSpec((None, COND_ROWS, tn), lambda l, j: (l, 0, j)),
        out_shape=jax.ShapeDtypeStruct((depth, COND_ROWS, N), F32),
        compiler_params=_cp(("parallel", "parallel")),
        name="modulation",
    )(cond, w_ada, b_ada.reshape(depth, 1, N))


def _cond_row(i, tm, L, B):
    return jnp.minimum((i * tm) // L, B)


def _norm_mod_kernel(x_ref, g_ref, sh_ref, sc_ref, o_ref, *, tm, L, B):
    r = _cond_row(pl.program_id(0), tm, L, B)
    xn = _rms(x_ref[...]) * g_ref[...]
    o_ref[...] = (xn * (1.0 + sc_ref[pl.ds(r, 1), :]) + sh_ref[pl.ds(r, 1), :]).astype(o_ref.dtype)


def norm_mod(x, g, sh, sc, *, L, B):
    T, D = x.shape
    tm = _pick(L, 512)
    return pl.pallas_call(
        functools.partial(_norm_mod_kernel, tm=tm, L=L, B=B),
        grid=(T // tm,),
        in_specs=[
            pl.BlockSpec((tm, D), lambda i: (i, 0)),
            pl.BlockSpec((1, D), lambda i: (0, 0)),
            pl.BlockSpec((COND_ROWS, D), lambda i: (0, 0)),
            pl.BlockSpec((COND_ROWS, D), lambda i: (0, 0)),
        ],
        out_specs=pl.BlockSpec((tm, D), lambda i: (i, 0)),
        out_shape=jax.ShapeDtypeStruct((T, D), BF16),
        compiler_params=_cp(("parallel",)),
        name="norm_mod",
    )(x, g.reshape(1, D), sh, sc)


def _mm_kernel(a_ref, w_ref, o_ref, w_bf):
    @pl.when(pl.program_id(1) == 0)
    def _():
        w_bf[...] = w_ref[...].astype(BF16)

    o_ref[...] = jnp.dot(a_ref[...], w_bf[...], preferred_element_type=F32).astype(o_ref.dtype)


def matmul(a, w, l, col0, ncols, n_rows, tn, out_dtype=BF16):
    _, K = a.shape
    tm = _pick(n_rows, 1024)
    return pl.pallas_call(
        _mm_kernel,
        grid=(ncols // tn, n_rows // tm),
        in_specs=[
            pl.BlockSpec((tm, K), lambda j, i: (i, 0)),
            pl.BlockSpec((None, K, tn), lambda j, i: (l, 0, col0 // tn + j)),
        ],
        out_specs=pl.BlockSpec((tm, tn), lambda j, i: (i, j)),
        out_shape=jax.ShapeDtypeStruct((n_rows, ncols), out_dtype),
        scratch_shapes=[pltpu.VMEM((K, tn), BF16)],
        compiler_params=_cp(("arbitrary", "arbitrary")),
        name="in_proj",
    )(a, w)


def _rope_tables(L, tm):
    rows = L // GRID_W
    row = np.repeat(np.arange(rows), GRID_W).astype(np.float32)
    col = np.tile(np.arange(GRID_W), rows).astype(np.float32)
    axis_dim = HEAD_DIM // 2
    inv = (np.float32(ROPE_THETA) ** (-np.arange(0, axis_dim, 2, dtype=np.float32) / axis_dim)).astype(np.float32)
    ang = np.concatenate([row[:, None] * inv, col[:, None] * inv], axis=-1).astype(np.float64)
    cos = np.repeat(np.cos(ang), 2, axis=-1)
    sin = np.repeat(np.sin(ang), 2, axis=-1)
    sign = np.tile(np.array([-1.0, 1.0]), HEAD_DIM // 2)
    cos = np.concatenate([cos, np.ones((tm, HEAD_DIM))], axis=0)
    sin = np.concatenate([sin * sign, np.zeros((tm, HEAD_DIM))], axis=0)
    return jnp.asarray(cos, F32), jnp.asarray(sin, F32)


def _qk_kernel(q_ref, k_ref, cos_ref, sin_ref, gq_ref, gk_ref, qo_ref, ko_ref, *, n_q, n_kv):
    cos = cos_ref[...]
    sin = sin_ref[...]
    even = (lax.broadcasted_iota(I32, cos.shape, 1) & 1) == 0

    def norm_rope(t, g):
        n = _rms(t.astype(F32)) * g
        partner = jnp.where(even, pltpu.roll(n, HEAD_DIM - 1, 1), pltpu.roll(n, 1, 1))
        return n * cos + partner * sin

    q_scale = HEAD_DIM ** -0.5
    for h in range(n_q):
        sl = slice(h * HEAD_DIM, (h + 1) * HEAD_DIM)
        qo_ref[:, sl] = (norm_rope(q_ref[:, sl], gq_ref[...]) * q_scale).astype(qo_ref.dtype)
    for h in range(n_kv):
        sl = slice(h * HEAD_DIM, (h + 1) * HEAD_DIM)
        ko_ref[:, sl] = norm_rope(k_ref[:, sl], gk_ref[...]).astype(ko_ref.dtype)


def qk_norm_rope(z_qkv, g_q, g_k, *, L, B, d_attn, kv_w):
    T = z_qkv.shape[0]
    tm = _pick(L, 256)
    cos, sin = _rope_tables(L, tm)
    n_lat = B * L // tm
    per_seq = L // tm

    def tab_idx(i):
        return (jnp.where(i < n_lat, i % per_seq, per_seq), 0)

    return pl.pallas_call(
        functools.partial(_qk_kernel, n_q=d_attn // HEAD_DIM, n_kv=kv_w // HEAD_DIM),
        grid=(T // tm,),
        in_specs=[
            pl.BlockSpec((tm, d_attn), lambda i: (i, 0)),
            pl.BlockSpec((tm, kv_w), lambda i: (i, d_attn // kv_w)),
            pl.BlockSpec((tm, HEAD_DIM), tab_idx),
            pl.BlockSpec((tm, HEAD_DIM), tab_idx),
            pl.BlockSpec((1, HEAD_DIM), lambda i: (0, 0)),
            pl.BlockSpec((1, HEAD_DIM), lambda i: (0, 0)),
        ],
        out_specs=[
            pl.BlockSpec((tm, d_attn), lambda i: (i, 0)),
            pl.BlockSpec((tm, kv_w), lambda i: (i, 0)),
        ],
        out_shape=[
            jax.ShapeDtypeStruct((T, d_attn), BF16),
            jax.ShapeDtypeStruct((T, kv_w), BF16),
        ],
        compiler_params=_cp(("parallel",)),
        name="qk_norm_rope",
    )(z_qkv, z_qkv, cos, sin, g_q.reshape(1, HEAD_DIM), g_k.reshape(1, HEAD_DIM))


def _attn_kernel(*refs, n_seg):
    q_ref = refs[0]
    k_refs = refs[1:1 + n_seg]
    v_refs = refs[1 + n_seg:1 + 2 * n_seg]
    o_ref = refs[-1]
    for h in range(GQA_GROUP):
        sl = slice(h * HEAD_DIM, (h + 1) * HEAD_DIM)
        q = q_ref[:, sl]
        s = [lax.dot_general(q, k[...], (((1,), (1,)), ((), ())), preferred_element_type=F32)
             for k in k_refs]
        m = functools.reduce(jnp.maximum, [jnp.max(x, axis=-1, keepdims=True) for x in s])
        p = [jnp.exp(x - m) for x in s]
        den = functools.reduce(jnp.add, [jnp.sum(x, axis=-1, keepdims=True) for x in p])
        o = functools.reduce(jnp.add, [jnp.dot(x.astype(BF16), v[...], preferred_element_type=F32)
                                       for x, v in zip(p, v_refs)])
        o_ref[:, sl] = (o / den).astype(o_ref.dtype)


def attention(qn, kn, z_qkv, *, q_row0, Lq, segs, B, d_attn, kv_w):
    n_kv = kv_w // HEAD_DIM
    gw = GQA_GROUP * HEAD_DIM
    tq = _pick(Lq, 512)
    nq = Lq // tq
    v_col0 = (d_attn + kv_w) // HEAD_DIM

    def q_idx(b, g, qi):
        return (q_row0 // tq + b * nq + qi, g)

    in_specs = [pl.BlockSpec((tq, gw), q_idx)]
    args = [qn]
    for row0, ln in segs:
        in_specs.append(pl.BlockSpec((ln, HEAD_DIM), lambda b, g, qi, row0=row0, ln=ln: (row0 // ln + b, g)))
        args.append(kn)
    for row0, ln in segs:
        in_specs.append(
            pl.BlockSpec((ln, HEAD_DIM), lambda b, g, qi, row0=row0, ln=ln: (row0 // ln + b, v_col0 + g)))
        args.append(z_qkv)
    return pl.pallas_call(
        functools.partial(_attn_kernel, n_seg=len(segs)),
        grid=(B, n_kv, nq),
        in_specs=in_specs,
        out_specs=pl.BlockSpec((tq, gw), lambda b, g, qi: (b * nq + qi, g)),
        out_shape=jax.ShapeDtypeStruct((B * Lq, d_attn), BF16),
        compiler_params=_cp(("parallel", "parallel", "parallel")),
        name="attention",
    )(*args)


def _dft_tables(n, dtype):
    k = np.arange(n, dtype=np.int64)
    ang = ((k[:, None] * k[None, :]) % n).astype(np.float64) * (2.0 * np.pi / n)
    return jnp.asarray(np.cos(ang), dtype), jnp.asarray(np.sin(ang), dtype)


def _fourier_kernel(t_ref, cl_ref, sl_ref, cc_ref, sc_ref, w_ref, o_ref, *, scale):
    w = w_ref[...]
    hi = lax.Precision.HIGHEST
    mc = jnp.dot(cc_ref[...], w, precision=hi, preferred_element_type=F32).astype(BF16)
    ms = jnp.dot(sc_ref[...], w, precision=hi, preferred_element_type=F32).astype(BF16)
    t = t_ref[...]
    bc = jnp.dot(t, mc, preferred_element_type=F32).astype(BF16)
    bs = jnp.dot(t, ms, preferred_element_type=F32).astype(BF16)
    y = (jnp.dot(cl_ref[...], bc, preferred_element_type=F32)
         - jnp.dot(sl_ref[...], bs, preferred_element_type=F32))
    o_ref[...] = (y * scale).astype(o_ref.dtype)


def fourier(z_f, w_f, *, row0, Ls, B):
    ng, fg, _ = w_f.shape
    cl, sl = _dft_tables(Ls, BF16)
    cc, sc = _dft_tables(fg, F32)
    scale = 1.0 / float(Ls * fg) ** 0.5
    return pl.pallas_call(
        functools.partial(_fourier_kernel, scale=scale),
        grid=(B, ng),
        in_specs=[
            pl.BlockSpec((Ls, fg), lambda b, g: (row0 // Ls + b, g)),
            pl.BlockSpec((Ls, Ls), lambda b, g: (0, 0)),
            pl.BlockSpec((Ls, Ls), lambda b, g: (0, 0)),
            pl.BlockSpec((fg, fg), lambda b, g: (0, 0)),
            pl.BlockSpec((fg, fg), lambda b, g: (0, 0)),
            pl.BlockSpec((None, fg, fg), lambda b, g: (g, 0, 0)),
        ],
        out_specs=pl.BlockSpec((Ls, fg), lambda b, g: (b, g)),
        out_shape=jax.ShapeDtypeStruct((B * Ls, ng * fg), BF16),
        compiler_params=_cp(("parallel", "parallel")),
        name="fourier",
    )(z_f, cl, sl, cc, sc, w_f)


def _conv_kernel(cur_ref, prev_ref, next_ref, wdw_ref, bdw_ref, gln_ref, bln_ref, wpw_ref, bpw_ref,
                 o_ref, gbuf, shifted, cbuf, *, tr, dc, n_lat, per_lat, per_ctx):
    i = pl.program_id(0)
    j = jnp.where(i < n_lat, i % per_lat, (i - n_lat) % per_ctx)
    per = jnp.where(i < n_lat, per_lat, per_ctx)
    first = j == 0
    last = j == per - 1

    def glu(zb):
        return zb[:, :dc].astype(F32) * jax.nn.sigmoid(zb[:, dc:].astype(F32))

    gbuf[0:CONV_HALO, :] = jnp.where(first, 0.0, glu(prev_ref[...]))
    gbuf[CONV_HALO:CONV_HALO + tr, :] = glu(cur_ref[...])
    gbuf[CONV_HALO + tr:, :] = jnp.where(last, 0.0, glu(next_ref[...]))
    n_sh = tr + CONV_HALO + ROW_GROUP
    for s in range(1, ROW_GROUP):
        shifted[s - 1, :, :] = gbuf[s:s + n_sh, :]
    first_off = CONV_HALO - CONV_PAD
    rb = min(tr, CONV_ROW_BLOCK)
    for c in range(dc // LANES):
        cols = slice(c * LANES, (c + 1) * LANES)
        for r0 in range(0, tr, rb):
            a = jnp.broadcast_to(bdw_ref[:, cols], (rb, LANES))
            for k in range(CONV_WIDTH):
                off = first_off + k + r0
                s = off % ROW_GROUP
                src = gbuf[off:off + rb, cols] if s == 0 else shifted[s - 1, off - s:off - s + rb, cols]
                a = a + src * wdw_ref[k:k + 1, cols]
            cbuf[r0:r0 + rb, cols] = a
    acc = cbuf[...]
    mu = jnp.mean(acc, axis=-1, keepdims=True)
    cen = acc - mu
    var = jnp.mean(cen * cen, axis=-1, keepdims=True)
    hn = cen * lax.rsqrt(var + NORM_EPS) * gln_ref[...] + bln_ref[...]
    act = jax.nn.silu(hn).astype(BF16)
    o_ref[...] = (jnp.dot(act, wpw_ref[...], preferred_element_type=F32) + bpw_ref[...]).astype(o_ref.dtype)


def conv_module(z_c, w_dw, b_dw, g_ln, b_ln, w_pw_bf, b_pw, *, L, Lc, B, n_rows):
    dc = w_pw_bf.shape[0]
    tr = _pick(Lc, 256, CONV_HALO)
    hpt = tr // CONV_HALO
    n_halo = z_c.shape[0] // CONV_HALO
    w_pad = jnp.concatenate([w_dw, jnp.zeros((32 - CONV_WIDTH, dc), F32)], axis=0)
    row = lambda v: v.reshape(1, dc)
    return pl.pallas_call(
        functools.partial(_conv_kernel, tr=tr, dc=dc, n_lat=B * L // tr, per_lat=L // tr, per_ctx=Lc // tr),
        grid=(n_rows // tr,),
        in_specs=[
            pl.BlockSpec((tr, 2 * dc), lambda i: (i, 0)),
            pl.BlockSpec((CONV_HALO, 2 * dc), lambda i: (jnp.maximum(i * hpt - 1, 0), 0)),
            pl.BlockSpec((CONV_HALO, 2 * dc), lambda i: (jnp.minimum((i + 1) * hpt, n_halo - 1), 0)),
            pl.BlockSpec((32, dc), lambda i: (0, 0)),
            pl.BlockSpec((1, dc), lambda i: (0, 0)),
            pl.BlockSpec((1, dc), lambda i: (0, 0)),
            pl.BlockSpec((1, dc), lambda i: (0, 0)),
            pl.BlockSpec((dc, dc), lambda i: (0, 0)),
            pl.BlockSpec((1, dc), lambda i: (0, 0)),
        ],
        out_specs=pl.BlockSpec((tr, dc), lambda i: (i, 0)),
        out_shape=jax.ShapeDtypeStruct((n_rows, dc), BF16),
        scratch_shapes=[
            pltpu.VMEM((tr + 2 * CONV_HALO, dc), F32),
            pltpu.VMEM((ROW_GROUP - 1, tr + CONV_HALO + ROW_GROUP, dc), F32),
            pltpu.VMEM((tr, dc), F32),
        ],
        compiler_params=_cp(("parallel",)),
        name="conv_module",
    )(z_c, z_c, z_c, w_pad, row(b_dw), row(g_ln), row(b_ln), w_pw_bf, row(b_pw))


def _sgu_kernel(z_ref, g_ref, ws_ref, bs_ref, o_ref, *, tm, ds, n_heads):
    for h in range(n_heads):
        sl = slice(h * HEAD_DIM, (h + 1) * HEAD_DIM)
        u = jax.nn.gelu(z_ref[:, sl].astype(F32))
        v = jax.nn.gelu(z_ref[:, ds + h * HEAD_DIM:ds + (h + 1) * HEAD_DIM].astype(F32))
        vn = (_rms(v) * g_ref[:, sl]).astype(BF16)
        w = ws_ref[h].astype(BF16)
        b = bs_ref[:, h:h + 1]
        for c in range(tm // SGU_CHUNK):
            rows = slice(c * SGU_CHUNK, (c + 1) * SGU_CHUNK)
            s = jnp.dot(w, vn[rows, :], preferred_element_type=F32) + b
            o_ref[rows, sl] = (u[rows, :] * s).astype(o_ref.dtype)


def sgu_mixer(z_s, g_sgu, w_s, b_s, *, n_rows):
    n_heads = w_s.shape[0]
    ds = n_heads * HEAD_DIM
    tm = _pick(n_rows, 256, SGU_CHUNK)
    return pl.pallas_call(
        functools.partial(_sgu_kernel, tm=tm, ds=ds, n_heads=n_heads),
        grid=(n_rows // tm,),
        in_specs=[
            pl.BlockSpec((tm, 2 * ds), lambda i: (i, 0)),
            pl.BlockSpec((1, ds), lambda i: (0, 0)),
            pl.BlockSpec((n_heads, SGU_CHUNK, SGU_CHUNK), lambda i: (0, 0, 0)),
            pl.BlockSpec((SGU_CHUNK, n_heads), lambda i: (0, 0)),
        ],
        out_specs=pl.BlockSpec((tm, ds), lambda i: (i, 0)),
        out_shape=jax.ShapeDtypeStruct((n_rows, ds), BF16),
        compiler_params=_cp(("parallel",)),
        name="sgu_mixer",
    )(z_s, g_sgu.reshape(1, ds), w_s, b_s.T)


def _out_proj_kernel(*refs, tm, L, B, widths, split, n_lat_tiles):
    n_y = sum(2 if sp else 1 for sp in split)
    y_refs = refs[:n_y]
    w_ref, x_ref, ga_ref, o_ref = refs[n_y:]
    i = pl.program_id(1)
    r = _cond_row(i, tm, L, B)
    acc = None
    k0 = 0
    p = 0
    for wd, sp in zip(widths, split):
        if sp:
            y = jnp.where(i < n_lat_tiles, y_refs[p][...], y_refs[p + 1][...])
            p += 2
        else:
            y = y_refs[p][...]
            p += 1
        part = jnp.dot(y, w_ref[k0:k0 + wd, :], preferred_element_type=F32)
        acc = part if acc is None else acc + part
        k0 += wd
    o_ref[...] = x_ref[...] + ga_ref[pl.ds(r, 1), :] * acc


def out_proj(ys, w_out_bf, x, ga, *, L, B, n_rows):
    D = x.shape[1]
    tm = _pick(L, 512)
    tn = _pick(D, 1024, 128)
    n_lat_tiles = B * L // tm
    split = tuple(isinstance(y, tuple) for y in ys)
    widths = tuple((y[0] if sp else y).shape[1] for y, sp in zip(ys, split))
    in_specs, args = [], []
    for y, wd, sp in zip(ys, widths, split):
        if sp:
            in_specs.append(pl.BlockSpec((tm, wd), lambda j, i: (jnp.minimum(i, n_lat_tiles - 1), 0)))
            in_specs.append(pl.BlockSpec((tm, wd), lambda j, i: (jnp.maximum(i - n_lat_tiles, 0), 0)))
            args += [y[0], y[1]]
        else:
            in_specs.append(pl.BlockSpec((tm, wd), lambda j, i: (i, 0)))
            args.append(y)
    in_specs += [
        pl.BlockSpec((sum(widths), tn), lambda j, i: (0, j)),
        pl.BlockSpec((tm, tn), lambda j, i: (i, j)),
        pl.BlockSpec((COND_ROWS, tn), lambda j, i: (0, j)),
    ]
    return pl.pallas_call(
        functools.partial(_out_proj_kernel, tm=tm, L=L, B=B, widths=widths, split=split,
                          n_lat_tiles=n_lat_tiles),
        grid=(D // tn, n_rows // tm),
        in_specs=in_specs,
        out_specs=pl.BlockSpec((tm, tn), lambda j, i: (i, j)),
        out_shape=jax.ShapeDtypeStruct((n_rows, D), F32),
        compiler_params=_cp(("parallel", "parallel")),
        name="out_proj",
    )(*args, w_out_bf, x, ga)


def _router_kernel(x_ref, g_ref, sh_ref, sc_ref, wr_ref, br_ref,
                   h2_ref, eidx_ref, wsel_ref, rank_ref, cnt_ref, *, tm, L, B, n_exp):
    i = pl.program_id(0)
    r = _cond_row(i, tm, L, B)
    xn = _rms(x_ref[...]) * g_ref[...]
    h2 = xn * (1.0 + sc_ref[pl.ds(r, 1), :]) + sh_ref[pl.ds(r, 1), :]
    h2_ref[...] = _pack_halves(h2).reshape(h2_ref.shape)
    logits = lax.dot_general(wr_ref[...], h2, (((1,), (1,)), ((), ())),
                             precision=lax.Precision.HIGHEST, preferred_element_type=F32)
    s = jax.nn.sigmoid(logits)
    sel = s + br_ref[...]
    shape = (n_exp, tm)
    e_io = lax.broadcasted_iota(I32, shape, 0)
    e_f = e_io.astype(F32)
    k_io = e_io & (EXPERTS_PER_GROUP - 1)
    g_io = e_io >> 3
    n_groups = n_exp // EXPERTS_PER_GROUP
    neg = -jnp.inf

    def partner(v, d):
        up = pltpu.roll(v, n_exp - d, 0)
        dn = pltpu.roll(v, d, 0)
        return jnp.where((k_io & d) == 0, up, dn)

    def group_all(v, op):
        d = 1
        while d < EXPERTS_PER_GROUP:
            v = op(v, partner(v, d))
            d *= 2
        return v

    kf = k_io.astype(F32)
    m1 = group_all(sel, jnp.maximum)
    i1 = group_all(jnp.where(sel == m1, kf, float(EXPERTS_PER_GROUP)), jnp.minimum)
    m2 = group_all(jnp.where(kf == i1, neg, sel), jnp.maximum)
    gs = m1 + m2
    beaten = jnp.zeros(shape, I32)
    for d in range(1, n_groups):
        other = pltpu.roll(gs, n_exp - EXPERTS_PER_GROUP * d, 0)
        og = (g_io + d) & (n_groups - 1)
        beats = (other > gs) | ((other == gs) & (og < g_io))
        beaten = beaten + beats.astype(I32)
    val = jnp.where(beaten < TOPK_GROUPS, sel, neg)

    assign = jnp.zeros(shape, F32)
    w_rows = []
    idx_rows = []
    for j in range(TOP_K):
        m = jnp.max(val, axis=0, keepdims=True)
        idx = jnp.min(jnp.where(val == m, e_f, float(n_exp)), axis=0, keepdims=True)
        hit = e_f == idx
        w_rows.append(jnp.sum(jnp.where(hit, s, 0.0), axis=0, keepdims=True))
        val = jnp.where(hit, neg, val)
        assign = assign + hit.astype(F32)
        idx_rows.append(idx)
        eidx_ref[j:j + 1, :] = idx.astype(I32)
    w_sum = functools.reduce(jnp.add, w_rows)
    for j in range(TOP_K):
        wsel_ref[j:j + 1, :] = w_rows[j] / w_sum * ROUTED_SCALE

    @pl.when(i == 0)
    def _():
        cnt_ref[...] = jnp.zeros_like(cnt_ref)

    earlier = (lax.broadcasted_iota(I32, (tm, tm), 0) < lax.broadcasted_iota(I32, (tm, tm), 1)).astype(BF16)
    before = jnp.dot(assign.astype(BF16), earlier, preferred_element_type=F32) + cnt_ref[...]
    for j in range(TOP_K):
        hit = e_f == idx_rows[j]
        rank_ref[j:j + 1, :] = jnp.sum(jnp.where(hit, before, 0.0), axis=0, keepdims=True).astype(I32)
    cnt_ref[...] = cnt_ref[...] + jnp.sum(assign, axis=1, keepdims=True)


def router(x, g, sh, sc, w_router, b_router, *, L, B, n_rows):
    D = x.shape[1]
    n_exp = w_router.shape[1]
    tm = _pick(L, 512, 128)
    assert n_rows % tm == 0
    return pl.pallas_call(
        functools.partial(_router_kernel, tm=tm, L=L, B=B, n_exp=n_exp),
        grid=(n_rows // tm,),
        in_specs=[
            pl.BlockSpec((tm, D), lambda i: (i, 0)),
            pl.BlockSpec((1, D), lambda i: (0, 0)),
            pl.BlockSpec((COND_ROWS, D), lambda i: (0, 0)),
            pl.BlockSpec((COND_ROWS, D), lambda i: (0, 0)),
            pl.BlockSpec((n_exp, D), lambda i: (0, 0)),
            pl.BlockSpec((n_exp, 1), lambda i: (0, 0)),
        ],
        out_specs=[
            pl.BlockSpec((tm // ROW_GROUP, ROW_GROUP, D // 2), lambda i: (i, 0, 0)),
            pl.BlockSpec((TOP_K, tm), lambda i: (0, i)),
            pl.BlockSpec((TOP_K, tm), lambda i: (0, i)),
            pl.BlockSpec((TOP_K, tm), lambda i: (0, i)),
            pl.BlockSpec((n_exp, 1), lambda i: (0, 0)),
        ],
        out_shape=[
            jax.ShapeDtypeStruct((n_rows // ROW_GROUP, ROW_GROUP, D // 2), I32),
            jax.ShapeDtypeStruct((TOP_K, n_rows), I32),
            jax.ShapeDtypeStruct((TOP_K, n_rows), F32),
            jax.ShapeDtypeStruct((TOP_K, n_rows), I32),
            jax.ShapeDtypeStruct((n_exp, 1), F32),
        ],
        compiler_params=_cp(("arbitrary",)),
        name="router",
    )(x, g.reshape(1, D), sh, sc, w_router.T, b_router.reshape(n_exp, 1))


def _vrow(ref, group, sub):
    return ref.at[group, pl.ds(sub, 1)]


def _hrow(ref, row):
    return ref.at[pl.ds(row, 1)]


def _dispatch_kernel(pos_ref, h_ref, xs_ref, sem, *, tm):
    def issue(g, carry):
        base = g * ROW_GROUP
        for u in range(ROW_GROUP):
            for j in range(TOP_K):
                pltpu.make_async_copy(_vrow(h_ref, g, u), _hrow(xs_ref, pos_ref[j * tm + base + u]),
                                      sem).start(priority=j % 2)
        return carry

    lax.fori_loop(0, tm // ROW_GROUP, issue, 0)
    for j in range(TOP_K):
        pltpu.make_async_copy(xs_ref.at[pl.ds(0, tm)], xs_ref.at[pl.ds(0, tm)], sem).wait()


def dispatch(pos, h2p, n_slots, tm):
    G, _, W = h2p.shape
    T = G * ROW_GROUP
    return pl.pallas_call(
        functools.partial(_dispatch_kernel, tm=tm),
        grid=(T // tm,),
        in_specs=[
            pl.BlockSpec((TOP_K * tm,), lambda i: (i,), memory_space=pltpu.SMEM),
            pl.BlockSpec((tm // ROW_GROUP, ROW_GROUP, W), lambda i: (i, 0, 0)),
        ],
        out_specs=pl.BlockSpec(memory_space=pl.ANY),
        out_shape=jax.ShapeDtypeStruct((n_slots, W), I32),
        scratch_shapes=[pltpu.SemaphoreType.DMA(())],
        compiler_params=_cp(("arbitrary",), has_side_effects=True),
        name="dispatch",
    )(pos, h2p)


def _experts_kernel(gid_ref, nvalid_ref, nused_ref, wslot_ref, nxt_ref, xs_ref, wg_hbm, wu_hbm, wd_hbm, ys_ref,
                    wg_st, wu_st, wd_st, wg_bf, wu_bf, wd_bf, obuf, wsem, osem, *, tr, l, n_tiles):
    i = pl.program_id(0)
    n_used = nused_ref[0]
    first = jnp.logical_or(i == 0, gid_ref[i] != gid_ref[jnp.maximum(i - 1, 0)])
    n_lane_tiles = obuf.shape[2] // LANES

    def weight_copies(e, slot):
        return [pltpu.make_async_copy(src.at[l, e], dst.at[slot], wsem.at[slot])
                for src, dst in ((wg_hbm, wg_st), (wu_hbm, wu_st), (wd_hbm, wd_st))]

    def out_copies(tile, slot):
        return [pltpu.make_async_copy(obuf.at[slot, :, pl.ds(c * LANES, LANES)],
                                      ys_ref.at[pl.ds(tile * tr, tr), c], osem.at[slot])
                for c in range(n_lane_tiles)]

    @pl.when(i == 0)
    def _():
        for cp in weight_copies(gid_ref[0], wslot_ref[0]):
            cp.start(priority=1)

    @pl.when(first)
    def _():
        slot = wslot_ref[i]
        for cp in weight_copies(gid_ref[i], slot):
            cp.wait()
        wg_bf[...] = wg_st[slot].astype(BF16)
        wu_bf[...] = wu_st[slot].astype(BF16)
        wd_bf[...] = wd_st[slot].astype(BF16)

        @pl.when(nxt_ref[i] >= 0)
        def _():
            for cp in weight_copies(nxt_ref[i], 1 - slot):
                cp.start(priority=1)

    @pl.when(jnp.logical_and(i >= 2, i - 2 < n_used))
    def _():
        for cp in out_copies(i - 2, i % 2):
            cp.wait()

    @pl.when(i < n_used)
    def _():
        half = xs_ref.shape[1]
        valid = lax.broadcasted_iota(I32, (tr, 1), 0) < nvalid_ref[i]
        lo, hi = _unpack_halves(xs_ref[...])
        lo = jnp.where(valid, lo, 0.0).astype(BF16)
        hi = jnp.where(valid, hi, 0.0).astype(BF16)

        def proj(w_bf):
            return (jnp.dot(lo, w_bf[:half, :], preferred_element_type=F32)
                    + jnp.dot(hi, w_bf[half:, :], preferred_element_type=F32))

        hh = (jax.nn.silu(proj(wg_bf)) * proj(wu_bf)).astype(BF16)
        slot = i % 2
        obuf[slot] = _pack_halves(jnp.dot(hh, wd_bf[...], preferred_element_type=F32))
        for cp in out_copies(i, slot):
            cp.start()

    @pl.when(i == n_tiles - 1)
    def _():
        @pl.when(jnp.logical_and(i >= 1, i - 1 < n_used))
        def _():
            for cp in out_copies(i - 1, (i - 1) % 2):
                cp.wait()

        @pl.when(i < n_used)
        def _():
            for cp in out_copies(i, i % 2):
                cp.wait()


def experts(xs, gid, nvalid, nused, wslot, nxt, w_ge, w_ue, w_de, l, *, tr):
    P, W = xs.shape
    D = 2 * W
    F = w_ge.shape[3]
    n_tiles = P // tr

    def row_idx(i, gid, nvalid, nused, wslot, nxt):
        return (jnp.minimum(i, nused[0] - 1), 0)

    return pl.pallas_call(
        functools.partial(_experts_kernel, tr=tr, l=l, n_tiles=n_tiles),
        grid_spec=pltpu.PrefetchScalarGridSpec(
            num_scalar_prefetch=5,
            grid=(n_tiles,),
            in_specs=[
                pl.BlockSpec((tr, W), row_idx),
                pl.BlockSpec(memory_space=pl.ANY),
                pl.BlockSpec(memory_space=pl.ANY),
                pl.BlockSpec(memory_space=pl.ANY),
            ],
            out_specs=pl.BlockSpec(memory_space=pl.ANY),
            scratch_shapes=[
                pltpu.VMEM((2, D, F), F32), pltpu.VMEM((2, D, F), F32), pltpu.VMEM((2, F, D), F32),
                pltpu.VMEM((D, F), BF16), pltpu.VMEM((D, F), BF16), pltpu.VMEM((F, D), BF16),
                pltpu.VMEM((2, tr, W), I32),
                pltpu.SemaphoreType.DMA((2,)), pltpu.SemaphoreType.DMA((2,)),
            ],
        ),
        out_shape=jax.ShapeDtypeStruct((P, W // LANES, LANES), I32),
        compiler_params=_cp(("arbitrary",), has_side_effects=True),
        name="experts",
    )(gid, nvalid, nused, wslot, nxt, xs, w_ge, w_ue, w_de)


def _combine_kernel(pos_ref, w_ref, x_ref, h_ref, ga_ref, wgs_ref, wus_ref, wds_ref, ys_ref,
                    o_ref, gbuf, acc, sem, *, tm, L, B):
    r = _cond_row(pl.program_id(0), tm, L, B)
    half = h_ref.shape[2]
    n_slab = gbuf.shape[2]

    def issue(t, carry):
        for j in range(TOP_K):
            pltpu.make_async_copy(ys_ref.at[pos_ref[j * tm + t]], gbuf.at[j, t], sem).start(priority=j % 2)
        return carry

    lax.fori_loop(0, tm, issue, 0, unroll=2)

    lo, hi = _unpack_halves(h_ref[...].reshape(tm, half))
    lo = lo.astype(BF16)
    hi = hi.astype(BF16)

    def proj(w_bf):
        return (jnp.dot(lo, w_bf[:half, :], preferred_element_type=F32)
                + jnp.dot(hi, w_bf[half:, :], preferred_element_type=F32))

    hh = (jax.nn.silu(proj(wgs_ref)) * proj(wus_ref)).astype(BF16)
    shared = jnp.dot(hh, wds_ref[...], preferred_element_type=F32)

    for j in range(TOP_K):
        pltpu.make_async_copy(ys_ref.at[pl.ds(0, tm)], gbuf.at[j], sem).wait()

    def token(t, carry):
        a_lo = jnp.zeros((n_slab, LANES), F32)
        a_hi = jnp.zeros((n_slab, LANES), F32)
        for j in range(TOP_K):
            y_lo, y_hi = _unpack_halves(gbuf[j, t])
            w = w_ref[j * tm + t]
            a_lo = a_lo + w * y_lo
            a_hi = a_hi + w * y_hi
        acc[t, 0:n_slab, :] = a_lo
        acc[t, n_slab:2 * n_slab, :] = a_hi
        return carry

    lax.fori_loop(0, tm, token, 0, unroll=2)

    ga = ga_ref[pl.ds(r, 1), :]
    for c in range(2 * n_slab):
        cols = slice(c * LANES, (c + 1) * LANES)
        o_ref[:, cols] = x_ref[:, cols] + ga[:, cols] * (shared[:, cols] + acc[:, c, :])


def combine(pos, x, h2p, wsel_flat, ga, wgs_bf, wus_bf, wds_bf, ys, *, L, B, n_rows, tm):
    D = x.shape[1]
    W = D // 2
    F = wgs_bf.shape[1]
    n_slab = W // LANES
    return pl.pallas_call(
        functools.partial(_combine_kernel, tm=tm, L=L, B=B),
        grid=(n_rows // tm,),
        in_specs=[
            pl.BlockSpec((TOP_K * tm,), lambda i: (i,), memory_space=pltpu.SMEM),
            pl.BlockSpec((TOP_K * tm,), lambda i: (i,), memory_space=pltpu.SMEM),
            pl.BlockSpec((tm, D), lambda i: (i, 0)),
            pl.BlockSpec((tm // ROW_GROUP, ROW_GROUP, W), lambda i: (i, 0, 0)),
            pl.BlockSpec((COND_ROWS, D), lambda i: (0, 0)),
            pl.BlockSpec((D, F), lambda i: (0, 0)),
            pl.BlockSpec((D, F), lambda i: (0, 0)),
            pl.BlockSpec((F, D), lambda i: (0, 0)),
            pl.BlockSpec(memory_space=pl.ANY),
        ],
        out_specs=pl.BlockSpec((tm, D), lambda i: (i, 0)),
        out_shape=jax.ShapeDtypeStruct((n_rows, D), F32),
        scratch_shapes=[pltpu.VMEM((TOP_K, tm, n_slab, LANES), I32), pltpu.VMEM((tm, 2 * n_slab, LANES), F32),
                        pltpu.SemaphoreType.DMA(())],
        compiler_params=_cp(("arbitrary",)),
        name="combine",
    )(pos, wsel_flat, x, h2p, ga, wgs_bf, wus_bf, wds_bf, ys)


def _tile_major(a, tile):
    k, T = a.shape
    return a.reshape(k, T // tile, tile).transpose(1, 0, 2).reshape(-1)


def _dispatch_plan(cnt, eidx, rank, *, tr, n_tiles):
    n_exp = cnt.shape[0]
    e_ids = jnp.arange(n_exp, dtype=I32)
    tiles_e = (cnt + tr - 1) // tr
    tile_end = jnp.cumsum(tiles_e)
    tile_start = tile_end - tiles_e

    def lookup(table, idx, keys=e_ids):
        return jnp.sum(jnp.where(idx[..., None] == keys, table, 0), axis=-1)

    pos = lookup(tile_start * tr, eidx) + rank
    n_used = tile_end[-1]
    ti = jnp.arange(n_tiles, dtype=I32)
    ti_c = jnp.minimum(ti, n_used - 1)
    gid = jnp.sum((ti_c[:, None] >= tile_end[None, :]).astype(I32), axis=1)
    left = lookup(cnt, gid) - (ti - lookup(tile_start, gid)) * tr
    nvalid = jnp.where(ti < n_used, jnp.clip(left, 0, tr), 0)
    changed = jnp.concatenate([jnp.ones((1,), I32), (gid[1:] != gid[:-1]).astype(I32)])
    wslot = (jnp.cumsum(changed) - 1) % 2
    after = lookup(tile_end, gid)
    nxt = jnp.where(after < n_used, lookup(gid, jnp.minimum(after, n_tiles - 1), ti), -1)
    as_i32 = lambda a: a.astype(I32)
    return (as_i32(pos), as_i32(gid), as_i32(nvalid), n_used.reshape(1).astype(I32), as_i32(wslot), as_i32(nxt))


def kernel(x, c, ctx, c_ctx, w_ada, b_ada, g_norm1, g_norm2, w_in, g_q, g_k, w_fourier, w_dw, b_dw, g_conv_ln, b_conv_ln, w_pw, b_pw, g_sgu, w_spatial, b_spatial, w_out, w_router, b_router, w_gate_e, w_up_e, w_down_e, w_gate_s, w_up_s, w_down_s):
    B, L, D = x.shape
    Lc = ctx.shape[1]
    depth = w_ada.shape[0]
    assert B + 1 <= COND_ROWS and L % Lc == 0 and L % GRID_W == 0
    d_fourier = w_fourier.shape[1] * w_fourier.shape[2]
    d_conv = w_pw.shape[1]
    d_sgu = g_sgu.shape[1]
    d_attn = D - d_fourier - d_conv - d_sgu
    kv_w = d_attn // GQA_GROUP
    v_end = d_attn + 2 * kv_w
    f_end = v_end + d_fourier
    c_end = f_end + 2 * d_conv
    d_in = c_end + 2 * d_sgu
    assert w_in.shape[2] == d_in
    n_exp = w_router.shape[2]
    n_lat = B * L
    T_all = n_lat + B * Lc
    cb = D // 8
    exp_tr = 256
    tok_tile = _pick(L, 256, 128)
    comb_tile = 128

    assert n_exp // EXPERTS_PER_GROUP == 8 and (B * Lc) % _pick(L, 512) == 0
    assert all(c % cb == 0 for c in (v_end, f_end, c_end, d_in))
    fh = w_gate_s.shape[2]

    cond = jnp.concatenate([c, c_ctx[None, :], jnp.zeros((COND_ROWS - B - 1, D), F32)], axis=0)
    mod = modulation(cond, w_ada, b_ada)
    mod = mod.reshape(depth, COND_ROWS, N_MOD, D).transpose(0, 2, 1, 3)

    xa = jnp.concatenate([x.reshape(n_lat, D), ctx.reshape(B * Lc, D)], axis=0)

    for l in range(depth):
        last = l == depth - 1
        n_rows = n_lat if last else T_all
        sh1, sc1, ga1, sh2, sc2, ga2 = (mod[l, m] for m in range(N_MOD))

        w_out_bf = cast_cols(w_out, l, 0, D, cb)
        w_pw_bf = cast_cols(w_pw, l, 0, d_conv, _pick(d_conv, 512, 128))
        wgs_bf = cast_cols(w_gate_s, l, 0, fh, fh)
        wus_bf = cast_cols(w_up_s, l, 0, fh, fh)
        wds_bf = cast_cols(w_down_s, l, 0, D, cb)

        h = norm_mod(xa, g_norm1[l], sh1, sc1, L=L, B=B)
        z_qkv = matmul(h, w_in, l, 0, v_end, T_all, cb)
        z_f = matmul(h, w_in, l, v_end, d_fourier, n_rows, cb)
        z_c = matmul(h, w_in, l, f_end, 2 * d_conv, n_rows, cb)
        z_s = matmul(h, w_in, l, c_end, 2 * d_sgu, n_rows, cb)
        qn, kn = qk_norm_rope(z_qkv, g_q[l], g_k[l], L=L, B=B, d_attn=d_attn, kv_w=kv_w)
        y_a = attention(qn, kn, z_qkv, q_row0=0, Lq=L, segs=[(0, L), (n_lat, Lc)],
                        B=B, d_attn=d_attn, kv_w=kv_w)
        y_f = fourier(z_f, w_fourier[l], row0=0, Ls=L, B=B)
        if not last:
            y_a = (y_a, attention(qn, kn, z_qkv, q_row0=n_lat, Lq=Lc, segs=[(n_lat, Lc)],
                                  B=B, d_attn=d_attn, kv_w=kv_w))
            y_f = (y_f, fourier(z_f, w_fourier[l], row0=n_lat, Ls=Lc, B=B))
        y_c = conv_module(z_c, w_dw[l], b_dw[l], g_conv_ln[l], b_conv_ln[l], w_pw_bf, b_pw[l],
                          L=L, Lc=Lc, B=B, n_rows=n_rows)
        y_s = sgu_mixer(z_s, g_sgu[l], w_spatial[l], b_spatial[l], n_rows=n_rows)
        xa = out_proj([y_a, y_f, y_c, y_s], w_out_bf, xa, ga1, L=L, B=B, n_rows=n_rows)

        h2p, eidx, wsel, rank, cnt = router(xa, g_norm2[l], sh2, sc2, w_router[l], b_router[l],
                                            L=L, B=B, n_rows=n_rows)
        n_tiles = -(-n_rows * TOP_K // exp_tr) + n_exp
        pos, gid, nvalid, nused, wslot, nxt = _dispatch_plan(
            cnt[:, 0].astype(I32), eidx, rank, tr=exp_tr, n_tiles=n_tiles)
        xs = dispatch(_tile_major(pos, tok_tile), h2p, n_tiles * exp_tr, tok_tile)
        ys = experts(xs, gid, nvalid, nused, wslot, nxt, w_gate_e, w_up_e, w_down_e, l, tr=exp_tr)
        xa = combine(_tile_major(pos, comb_tile), xa, h2p, _tile_major(wsel, comb_tile), ga2,
                     wgs_bf, wus_bf, wds_bf, ys, L=L, B=B, n_rows=n_rows, tm=comb_tile)

    return xa.reshape(B, L, D)
```

```python
import functools

import jax
import jax.numpy as jnp
import numpy as np
from jax import lax
from jax.experimental import pallas as pl
from jax.experimental.pallas import tpu as pltpu

F32 = jnp.float32
BF16 = jnp.bfloat16
I32 = jnp.int32

HEAD_DIM = 128
GRID_W = 64
ROPE_THETA = 10000.0
NORM_EPS = 1e-6
GQA_GROUP = 4
N_FOURIER_GROUPS = 4
CONV_WIDTH = 31
CONV_PAD = CONV_WIDTH // 2
CONV_HALO = 16
SGU_CHUNK = 128
N_MOD = 6
TOP_K = 8
EXPERTS_PER_GROUP = 8
TOPK_GROUPS = 4
ROUTED_SCALE = 2.5
COND_ROWS = 8
ROW_GROUP = 8
LANES = 128
CONV_ROW_BLOCK = 64

VMEM_LIMIT = 56 * 1024 * 1024


def _cp(sem, vmem=VMEM_LIMIT, **kw):
    return pltpu.CompilerParams(dimension_semantics=sem, vmem_limit_bytes=vmem, **kw)


def _pick(n, pref, mult=8):
    if n <= pref:
        return n
    for t in range(pref - pref % mult, 0, -mult):
        if n % t == 0:
            return t
    raise ValueError(f"no tile for {n} <= {pref}")


def _rms(t, eps=NORM_EPS):
    return t * lax.rsqrt(jnp.mean(t * t, axis=-1, keepdims=True) + eps)


_HI16 = -65536


def _pack_halves(v):
    half = v.shape[1] // 2
    lo = lax.bitcast_convert_type(v[:, :half].astype(BF16).astype(F32), I32)
    hi = lax.bitcast_convert_type(v[:, half:].astype(BF16).astype(F32), I32)
    return lax.shift_right_logical(lo, 16) | (hi & _HI16)


def _unpack_halves(p):
    lo = lax.bitcast_convert_type(lax.shift_left(p, 16), F32)
    hi = lax.bitcast_convert_type(p & _HI16, F32)
    return lo, hi


def _cast_kernel(x_ref, o_ref):
    o_ref[...] = x_ref[...].astype(o_ref.dtype)


def cast_cols(w, l, col0, ncols, cb):
    _, R, _ = w.shape
    tr = _pick(R, 512)
    return pl.pallas_call(
        _cast_kernel,
        grid=(R // tr, ncols // cb),
        in_specs=[pl.BlockSpec((None, tr, cb), lambda i, j: (l, i, col0 // cb + j))],
        out_specs=pl.BlockSpec((tr, cb), lambda i, j: (i, j)),
        out_shape=jax.ShapeDtypeStruct((R, ncols), BF16),
        compiler_params=_cp(("parallel", "parallel")),
        name="cast_bf16",
    )(w)


def _modulation_kernel(c_ref, w_ref, b_ref, o_ref):
    s = jax.nn.silu(c_ref[...]).astype(BF16)
    o_ref[...] = jnp.dot(s, w_ref[...].astype(BF16), preferred_element_type=F32) + b_ref[...]


def modulation(cond, w_ada, b_ada):
    depth, D, N = w_ada.shape
    tn = _pick(N, 512, 128)
    return pl.pallas_call(
        _modulation_kernel,
        grid=(depth, N // tn),
        in_specs=[
            pl.BlockSpec((COND_ROWS, D), lambda l, j: (0, 0)),
            pl.BlockSpec((None, D, tn), lambda l, j: (l, 0, j)),
            pl.BlockSpec((None, 1, tn), lambda l, j: (l, 0, j)),
        ],
        out_specs=pl.BlockSpec((None, COND_ROWS, tn), lambda l, j: (l, 0, j)),
        out_shape=jax.ShapeDtypeStruct((depth, COND_ROWS, N), F32),
        compiler_params=_cp(("parallel", "parallel")),
        name="modulation",
    )(cond, w_ada, b_ada.reshape(depth, 1, N))


def _cond_row(i, tm, L, B):
    return jnp.minimum((i * tm) // L, B)


def _norm_mod_kernel(x_ref, g_ref, sh_ref, sc_ref, o_ref, *, tm, L, B):
    r = _cond_row(pl.program_id(0), tm, L, B)
    xn = _rms(x_ref[...]) * g_ref[...]
    o_ref[...] = (xn * (1.0 + sc_ref[pl.ds(r, 1), :]) + sh_ref[pl.ds(r, 1), :]).astype(o_ref.dtype)


def norm_mod(x, g, sh, sc, *, L, B):
    T, D = x.shape
    tm = _pick(L, 512)
    return pl.pallas_call(
        functools.partial(_norm_mod_kernel, tm=tm, L=L, B=B),
        grid=(T // tm,),
        in_specs=[
            pl.BlockSpec((tm, D), lambda i: (i, 0)),
            pl.BlockSpec((1, D), lambda i: (0, 0)),
            pl.BlockSpec((COND_ROWS, D), lambda i: (0, 0)),
            pl.BlockSpec((COND_ROWS, D), lambda i: (0, 0)),
        ],
        out_specs=pl.BlockSpec((tm, D), lambda i: (i, 0)),
        out_shape=jax.ShapeDtypeStruct((T, D), BF16),
        compiler_params=_cp(("parallel",)),
        name="norm_mod",
    )(x, g.reshape(1, D), sh, sc)


def _mm_kernel(a_ref, w_ref, o_ref, w_bf):
    @pl.when(pl.program_id(1) == 0)
    def _():
        w_bf[...] = w_ref[...].astype(BF16)

    o_ref[...] = jnp.dot(a_ref[...], w_bf[...], preferred_element_type=F32).astype(o_ref.dtype)


def matmul(a, w, l, col0, ncols, n_rows, tn, out_dtype=BF16):
    _, K = a.shape
    tm = _pick(n_rows, 1024)
    return pl.pallas_call(
        _mm_kernel,
        grid=(ncols // tn, n_rows // tm),
        in_specs=[
            pl.BlockSpec((tm, K), lambda j, i: (i, 0)),
            pl.BlockSpec((None, K, tn), lambda j, i: (l, 0, col0 // tn + j)),
        ],
        out_specs=pl.BlockSpec((tm, tn), lambda j, i: (i, j)),
        out_shape=jax.ShapeDtypeStruct((n_rows, ncols), out_dtype),
        scratch_shapes=[pltpu.VMEM((K, tn), BF16)],
        compiler_params=_cp(("arbitrary", "arbitrary")),
        name="in_proj",
    )(a, w)


def _rope_tables(L, tm):
    rows = L // GRID_W
    row = np.repeat(np.arange(rows), GRID_W).astype(np.float32)
    col = np.tile(np.arange(GRID_W), rows).astype(np.float32)
    axis_dim = HEAD_DIM // 2
    inv = (np.float32(ROPE_THETA) ** (-np.arange(0, axis_dim, 2, dtype=np.float32) / axis_dim)).astype(np.float32)
    ang = np.concatenate([row[:, None] * inv, col[:, None] * inv], axis=-1).astype(np.float64)
    cos = np.repeat(np.cos(ang), 2, axis=-1)
    sin = np.repeat(np.sin(ang), 2, axis=-1)
    sign = np.tile(np.array([-1.0, 1.0]), HEAD_DIM // 2)
    cos = np.concatenate([cos, np.ones((tm, HEAD_DIM))], axis=0)
    sin = np.concatenate([sin * sign, np.zeros((tm, HEAD_DIM))], axis=0)
    return jnp.asarray(cos, F32), jnp.asarray(sin, F32)


def _qk_kernel(q_ref, k_ref, cos_ref, sin_ref, gq_ref, gk_ref, qo_ref, ko_ref, *, n_q, n_kv):
    cos = cos_ref[...]
    sin = sin_ref[...]
    even = (lax.broadcasted_iota(I32, cos.shape, 1) & 1) == 0

    def norm_rope(t, g):
        n = _rms(t.astype(F32)) * g
        partner = jnp.where(even, pltpu.roll(n, HEAD_DIM - 1, 1), pltpu.roll(n, 1, 1))
        return n * cos + partner * sin

    q_scale = HEAD_DIM ** -0.5
    for h in range(n_q):
        sl = slice(h * HEAD_DIM, (h + 1) * HEAD_DIM)
        qo_ref[:, sl] = (norm_rope(q_ref[:, sl], gq_ref[...]) * q_scale).astype(qo_ref.dtype)
    for h in range(n_kv):
        sl = slice(h * HEAD_DIM, (h + 1) * HEAD_DIM)
        ko_ref[:, sl] = norm_rope(k_ref[:, sl], gk_ref[...]).astype(ko_ref.dtype)


def qk_norm_rope(z_qkv, g_q, g_k, *, L, B, d_attn, kv_w):
    T = z_qkv.shape[0]
    tm = _pick(L, 256)
    cos, sin = _rope_tables(L, tm)
    n_lat = B * L // tm
    per_seq = L // tm

    def tab_idx(i):
        return (jnp.where(i < n_lat, i % per_seq, per_seq), 0)

    return pl.pallas_call(
        functools.partial(_qk_kernel, n_q=d_attn // HEAD_DIM, n_kv=kv_w // HEAD_DIM),
        grid=(T // tm,),
        in_specs=[
            pl.BlockSpec((tm, d_attn), lambda i: (i, 0)),
            pl.BlockSpec((tm, kv_w), lambda i: (i, d_attn // kv_w)),
            pl.BlockSpec((tm, HEAD_DIM), tab_idx),
            pl.BlockSpec((tm, HEAD_DIM), tab_idx),
            pl.BlockSpec((1, HEAD_DIM), lambda i: (0, 0)),
            pl.BlockSpec((1, HEAD_DIM), lambda i: (0, 0)),
        ],
        out_specs=[
            pl.BlockSpec((tm, d_attn), lambda i: (i, 0)),
            pl.BlockSpec((tm, kv_w), lambda i: (i, 0)),
        ],
        out_shape=[
            jax.ShapeDtypeStruct((T, d_attn), BF16),
            jax.ShapeDtypeStruct((T, kv_w), BF16),
        ],
        compiler_params=_cp(("parallel",)),
        name="qk_norm_rope",
    )(z_qkv, z_qkv, cos, sin, g_q.reshape(1, HEAD_DIM), g_k.reshape(1, HEAD_DIM))


def _attn_kernel(*refs, n_seg):
    q_ref = refs[0]
    k_refs = refs[1:1 + n_seg]
    v_refs = refs[1 + n_seg:1 + 2 * n_seg]
    o_ref = refs[-1]
    v_ext = [jnp.concatenate([v[...], jnp.ones(v.shape, BF16)], axis=1) for v in v_refs]
    for h in range(GQA_GROUP):
        sl = slice(h * HEAD_DIM, (h + 1) * HEAD_DIM)
        q = q_ref[:, sl]
        s = [lax.dot_general(q, k[...], (((1,), (1,)), ((), ())), preferred_element_type=F32)
             for k in k_refs]
        m = functools.reduce(jnp.maximum, [jnp.max(x, axis=-1, keepdims=True) for x in s])
        p = [jnp.exp((x - m).astype(BF16)) for x in s]
        oe = functools.reduce(jnp.add, [jnp.dot(x, v, preferred_element_type=F32) for x, v in zip(p, v_ext)])
        o_ref[:, sl] = (oe[:, :HEAD_DIM] / oe[:, HEAD_DIM:HEAD_DIM + 1]).astype(o_ref.dtype)


def attention(qn, kn, z_qkv, *, q_row0, Lq, segs, B, d_attn, kv_w):
    n_kv = kv_w // HEAD_DIM
    gw = GQA_GROUP * HEAD_DIM
    tq = _pick(Lq, 512)
    nq = Lq // tq
    v_col0 = (d_attn + kv_w) // HEAD_DIM

    def q_idx(b, g, qi):
        return (q_row0 // tq + b * nq + qi, g)

    in_specs = [pl.BlockSpec((tq, gw), q_idx)]
    args = [qn]
    for row0, ln in segs:
        in_specs.append(pl.BlockSpec((ln, HEAD_DIM), lambda b, g, qi, row0=row0, ln=ln: (row0 // ln + b, g)))
        args.append(kn)
    for row0, ln in segs:
        in_specs.append(
            pl.BlockSpec((ln, HEAD_DIM), lambda b, g, qi, row0=row0, ln=ln: (row0 // ln + b, v_col0 + g)))
        args.append(z_qkv)
    return pl.pallas_call(
        functools.partial(_attn_kernel, n_seg=len(segs)),
        grid=(B, n_kv, nq),
        in_specs=in_specs,
        out_specs=pl.BlockSpec((tq, gw), lambda b, g, qi: (b * nq + qi, g)),
        out_shape=jax.ShapeDtypeStruct((B * Lq, d_attn), BF16),
        compiler_params=_cp(("parallel", "parallel", "parallel")),
        name="attention",
    )(*args)


def _dft_tables(n, dtype):
    k = np.arange(n, dtype=np.int64)
    ang = ((k[:, None] * k[None, :]) % n).astype(np.float64) * (2.0 * np.pi / n)
    return jnp.asarray(np.cos(ang), dtype), jnp.asarray(np.sin(ang), dtype)


def _fourier_kernel(t_ref, cl_ref, sl_ref, cc_ref, sc_ref, w_ref, o_ref, *, scale):
    w = w_ref[...]
    hi = lax.Precision.HIGHEST
    mc = jnp.dot(cc_ref[...], w, precision=hi, preferred_element_type=F32).astype(BF16)
    ms = jnp.dot(sc_ref[...], w, precision=hi, preferred_element_type=F32).astype(BF16)
    t = t_ref[...]
    bc = jnp.dot(t, mc, preferred_element_type=F32).astype(BF16)
    bs = jnp.dot(t, ms, preferred_element_type=F32).astype(BF16)
    y = (jnp.dot(cl_ref[...], bc, preferred_element_type=F32)
         - jnp.dot(sl_ref[...], bs, preferred_element_type=F32))
    o_ref[...] = (y * scale).astype(o_ref.dtype)


def fourier(z_f, w_f, *, row0, Ls, B):
    ng, fg, _ = w_f.shape
    cl, sl = _dft_tables(Ls, BF16)
    cc, sc = _dft_tables(fg, F32)
    scale = 1.0 / float(Ls * fg) ** 0.5
    return pl.pallas_call(
        functools.partial(_fourier_kernel, scale=scale),
        grid=(B, ng),
        in_specs=[
            pl.BlockSpec((Ls, fg), lambda b, g: (row0 // Ls + b, g)),
            pl.BlockSpec((Ls, Ls), lambda b, g: (0, 0)),
            pl.BlockSpec((Ls, Ls), lambda b, g: (0, 0)),
            pl.BlockSpec((fg, fg), lambda b, g: (0, 0)),
            pl.BlockSpec((fg, fg), lambda b, g: (0, 0)),
            pl.BlockSpec((None, fg, fg), lambda b, g: (g, 0, 0)),
        ],
        out_specs=pl.BlockSpec((Ls, fg), lambda b, g: (b, g)),
        out_shape=jax.ShapeDtypeStruct((B * Ls, ng * fg), BF16),
        compiler_params=_cp(("parallel", "parallel")),
        name="fourier",
    )(z_f, cl, sl, cc, sc, w_f)


def _conv_kernel(cur_ref, prev_ref, next_ref, wdw_ref, bdw_ref, gln_ref, bln_ref, wpw_ref, bpw_ref,
                 o_ref, gbuf, shifted, cbuf, *, tr, dc, n_lat, per_lat, per_ctx):
    i = pl.program_id(0)
    j = jnp.where(i < n_lat, i % per_lat, (i - n_lat) % per_ctx)
    per = jnp.where(i < n_lat, per_lat, per_ctx)
    first = j == 0
    last = j == per - 1

    def glu(zb):
        return zb[:, :dc].astype(F32) * jax.nn.sigmoid(zb[:, dc:].astype(F32))

    gbuf[0:CONV_HALO, :] = jnp.where(first, 0.0, glu(prev_ref[...]))
    gbuf[CONV_HALO:CONV_HALO + tr, :] = glu(cur_ref[...])
    gbuf[CONV_HALO + tr:, :] = jnp.where(last, 0.0, glu(next_ref[...]))
    n_sh = tr + CONV_HALO + ROW_GROUP
    for s in range(1, ROW_GROUP):
        shifted[s - 1, :, :] = gbuf[s:s + n_sh, :]
    first_off = CONV_HALO - CONV_PAD
    rb = min(tr, CONV_ROW_BLOCK)
    for c in range(dc // LANES):
        cols = slice(c * LANES, (c + 1) * LANES)
        for r0 in range(0, tr, rb):
            a = jnp.broadcast_to(bdw_ref[:, cols], (rb, LANES))
            for k in range(CONV_WIDTH):
                off = first_off + k + r0
                s = off % ROW_GROUP
                src = gbuf[off:off + rb, cols] if s == 0 else shifted[s - 1, off - s:off - s + rb, cols]
                a = a + src * wdw_ref[k:k + 1, cols]
            cbuf[r0:r0 + rb, cols] = a
    acc = cbuf[...]
    mu = jnp.mean(acc, axis=-1, keepdims=True)
    cen = acc - mu
    var = jnp.mean(cen * cen, axis=-1, keepdims=True)
    hn = cen * lax.rsqrt(var + NORM_EPS) * gln_ref[...] + bln_ref[...]
    act = jax.nn.silu(hn).astype(BF16)
    o_ref[...] = (jnp.dot(act, wpw_ref[...], preferred_element_type=F32) + bpw_ref[...]).astype(o_ref.dtype)


def conv_module(z_c, w_dw, b_dw, g_ln, b_ln, w_pw_bf, b_pw, *, L, Lc, B, n_rows):
    dc = w_pw_bf.shape[0]
    tr = _pick(Lc, 256, CONV_HALO)
    hpt = tr // CONV_HALO
    n_halo = z_c.shape[0] // CONV_HALO
    w_pad = jnp.concatenate([w_dw, jnp.zeros((32 - CONV_WIDTH, dc), F32)], axis=0)
    row = lambda v: v.reshape(1, dc)
    return pl.pallas_call(
        functools.partial(_conv_kernel, tr=tr, dc=dc, n_lat=B * L // tr, per_lat=L // tr, per_ctx=Lc // tr),
        grid=(n_rows // tr,),
        in_specs=[
            pl.BlockSpec((tr, 2 * dc), lambda i: (i, 0)),
            pl.BlockSpec((CONV_HALO, 2 * dc), lambda i: (jnp.maximum(i * hpt - 1, 0), 0)),
            pl.BlockSpec((CONV_HALO, 2 * dc), lambda i: (jnp.minimum((i + 1) * hpt, n_halo - 1), 0)),
            pl.BlockSpec((32, dc), lambda i: (0, 0)),
            pl.BlockSpec((1, dc), lambda i: (0, 0)),
            pl.BlockSpec((1, dc), lambda i: (0, 0)),
            pl.BlockSpec((1, dc), lambda i: (0, 0)),
            pl.BlockSpec((dc, dc), lambda i: (0, 0)),
            pl.BlockSpec((1, dc), lambda i: (0, 0)),
        ],
        out_specs=pl.BlockSpec((tr, dc), lambda i: (i, 0)),
        out_shape=jax.ShapeDtypeStruct((n_rows, dc), BF16),
        scratch_shapes=[
            pltpu.VMEM((tr + 2 * CONV_HALO, dc), F32),
            pltpu.VMEM((ROW_GROUP - 1, tr + CONV_HALO + ROW_GROUP, dc), F32),
            pltpu.VMEM((tr, dc), F32),
        ],
        compiler_params=_cp(("parallel",)),
        name="conv_module",
    )(z_c, z_c, z_c, w_pad, row(b_dw), row(g_ln), row(b_ln), w_pw_bf, row(b_pw))


def _sgu_kernel(z_ref, g_ref, ws_ref, bs_ref, o_ref, *, tm, ds, n_heads):
    for h in range(n_heads):
        sl = slice(h * HEAD_DIM, (h + 1) * HEAD_DIM)
        u = jax.nn.gelu(z_ref[:, sl].astype(F32))
        v = jax.nn.gelu(z_ref[:, ds + h * HEAD_DIM:ds + (h + 1) * HEAD_DIM].astype(F32))
        vn = (_rms(v) * g_ref[:, sl]).astype(BF16)
        w = ws_ref[h].astype(BF16)
        b = bs_ref[:, h:h + 1]
        for c in range(tm // SGU_CHUNK):
            rows = slice(c * SGU_CHUNK, (c + 1) * SGU_CHUNK)
            s = jnp.dot(w, vn[rows, :], preferred_element_type=F32) + b
            o_ref[rows, sl] = (u[rows, :] * s).astype(o_ref.dtype)


def sgu_mixer(z_s, g_sgu, w_s, b_s, *, n_rows):
    n_heads = w_s.shape[0]
    ds = n_heads * HEAD_DIM
    tm = _pick(n_rows, 256, SGU_CHUNK)
    return pl.pallas_call(
        functools.partial(_sgu_kernel, tm=tm, ds=ds, n_heads=n_heads),
        grid=(n_rows // tm,),
        in_specs=[
            pl.BlockSpec((tm, 2 * ds), lambda i: (i, 0)),
            pl.BlockSpec((1, ds), lambda i: (0, 0)),
            pl.BlockSpec((n_heads, SGU_CHUNK, SGU_CHUNK), lambda i: (0, 0, 0)),
            pl.BlockSpec((SGU_CHUNK, n_heads), lambda i: (0, 0)),
        ],
        out_specs=pl.BlockSpec((tm, ds), lambda i: (i, 0)),
        out_shape=jax.ShapeDtypeStruct((n_rows, ds), BF16),
        compiler_params=_cp(("parallel",)),
        name="sgu_mixer",
    )(z_s, g_sgu.reshape(1, ds), w_s, b_s.T)


def _out_proj_kernel(*refs, tm, L, B, widths, split, n_lat_tiles):
    n_y = sum(2 if sp else 1 for sp in split)
    y_refs = refs[:n_y]
    w_ref, x_ref, ga_ref, o_ref = refs[n_y:]
    i = pl.program_id(1)
    r = _cond_row(i, tm, L, B)
    acc = None
    k0 = 0
    p = 0
    for wd, sp in zip(widths, split):
        if sp:
            y = jnp.where(i < n_lat_tiles, y_refs[p][...], y_refs[p + 1][...])
            p += 2
        else:
            y = y_refs[p][...]
            p += 1
        part = jnp.dot(y, w_ref[k0:k0 + wd, :], preferred_element_type=F32)
        acc = part if acc is None else acc + part
        k0 += wd
    o_ref[...] = x_ref[...] + ga_ref[pl.ds(r, 1), :] * acc


def out_proj(ys, w_out_bf, x, ga, *, L, B, n_rows):
    D = x.shape[1]
    tm = _pick(L, 512)
    tn = _pick(D, 1024, 128)
    n_lat_tiles = B * L // tm
    split = tuple(isinstance(y, tuple) for y in ys)
    widths = tuple((y[0] if sp else y).shape[1] for y, sp in zip(ys, split))
    in_specs, args = [], []
    for y, wd, sp in zip(ys, widths, split):
        if sp:
            in_specs.append(pl.BlockSpec((tm, wd), lambda j, i: (jnp.minimum(i, n_lat_tiles - 1), 0)))
            in_specs.append(pl.BlockSpec((tm, wd), lambda j, i: (jnp.maximum(i - n_lat_tiles, 0), 0)))
            args += [y[0], y[1]]
        else:
            in_specs.append(pl.BlockSpec((tm, wd), lambda j, i: (i, 0)))
            args.append(y)
    in_specs += [
        pl.BlockSpec((sum(widths), tn), lambda j, i: (0, j)),
        pl.BlockSpec((tm, tn), lambda j, i: (i, j)),
        pl.BlockSpec((COND_ROWS, tn), lambda j, i: (0, j)),
    ]
    return pl.pallas_call(
        functools.partial(_out_proj_kernel, tm=tm, L=L, B=B, widths=widths, split=split,
                          n_lat_tiles=n_lat_tiles),
        grid=(D // tn, n_rows // tm),
        in_specs=in_specs,
        out_specs=pl.BlockSpec((tm, tn), lambda j, i: (i, j)),
        out_shape=jax.ShapeDtypeStruct((n_rows, D), F32),
        compiler_params=_cp(("parallel", "parallel")),
        name="out_proj",
    )(*args, w_out_bf, x, ga)


def _router_kernel(x_ref, g_ref, sh_ref, sc_ref, wr_ref, br_ref,
                   h2_ref, eidx_ref, wsel_ref, rank_ref, cnt_ref, *, tm, L, B, n_exp):
    i = pl.program_id(0)
    r = _cond_row(i, tm, L, B)
    xn = _rms(x_ref[...]) * g_ref[...]
    h2 = xn * (1.0 + sc_ref[pl.ds(r, 1), :]) + sh_ref[pl.ds(r, 1), :]
    h2_ref[...] = _pack_halves(h2).reshape(h2_ref.shape)
    logits = lax.dot_general(wr_ref[...], h2, (((1,), (1,)), ((), ())),
                             precision=lax.Precision.HIGHEST, preferred_element_type=F32)
    s = jax.nn.sigmoid(logits)
    sel = s + br_ref[...]
    shape = (n_exp, tm)
    e_io = lax.broadcasted_iota(I32, shape, 0)
    e_f = e_io.astype(F32)
    k_io = e_io & (EXPERTS_PER_GROUP - 1)
    g_io = e_io >> 3
    n_groups = n_exp // EXPERTS_PER_GROUP
    neg = -jnp.inf

    def partner(v, d):
        up = pltpu.roll(v, n_exp - d, 0)
        dn = pltpu.roll(v, d, 0)
        return jnp.where((k_io & d) == 0, up, dn)

    def group_all(v, op):
        d = 1
        while d < EXPERTS_PER_GROUP:
            v = op(v, partner(v, d))
            d *= 2
        return v

    kf = k_io.astype(F32)
    m1 = group_all(sel, jnp.maximum)
    i1 = group_all(jnp.where(sel == m1, kf, float(EXPERTS_PER_GROUP)), jnp.minimum)
    m2 = group_all(jnp.where(kf == i1, neg, sel), jnp.maximum)
    gs = m1 + m2
    beaten = jnp.zeros(shape, I32)
    for d in range(1, n_groups):
        other = pltpu.roll(gs, n_exp - EXPERTS_PER_GROUP * d, 0)
        og = (g_io + d) & (n_groups - 1)
        beats = (other > gs) | ((other == gs) & (og < g_io))
        beaten = beaten + beats.astype(I32)
    val = jnp.where(beaten < TOPK_GROUPS, sel, neg)

    assign = jnp.zeros(shape, F32)
    w_rows = []
    idx_rows = []
    for j in range(TOP_K):
        m = jnp.max(val, axis=0, keepdims=True)
        idx = jnp.min(jnp.where(val == m, e_f, float(n_exp)), axis=0, keepdims=True)
        hit = e_f == idx
        w_rows.append(jnp.sum(jnp.where(hit, s, 0.0), axis=0, keepdims=True))
        val = jnp.where(hit, neg, val)
        assign = assign + hit.astype(F32)
        idx_rows.append(idx)
        eidx_ref[j:j + 1, :] = idx.astype(I32)
    w_sum = functools.reduce(jnp.add, w_rows)
    for j in range(TOP_K):
        wsel_ref[j:j + 1, :] = w_rows[j] / w_sum * ROUTED_SCALE

    @pl.when(i == 0)
    def _():
        cnt_ref[...] = jnp.zeros_like(cnt_ref)

    earlier = (lax.broadcasted_iota(I32, (tm, tm), 0) < lax.broadcasted_iota(I32, (tm, tm), 1)).astype(BF16)
    before = jnp.dot(assign.astype(BF16), earlier, preferred_element_type=F32) + cnt_ref[...]
    for j in range(TOP_K):
        hit = e_f == idx_rows[j]
        rank_ref[j:j + 1, :] = jnp.sum(jnp.where(hit, before, 0.0), axis=0, keepdims=True).astype(I32)
    cnt_ref[...] = cnt_ref[...] + jnp.sum(assign, axis=1, keepdims=True)


def router(x, g, sh, sc, w_router, b_router, *, L, B, n_rows):
    D = x.shape[1]
    n_exp = w_router.shape[1]
    tm = _pick(L, 512, 128)
    assert n_rows % tm == 0
    return pl.pallas_call(
        functools.partial(_router_kernel, tm=tm, L=L, B=B, n_exp=n_exp),
        grid=(n_rows // tm,),
        in_specs=[
            pl.BlockSpec((tm, D), lambda i: (i, 0)),
            pl.BlockSpec((1, D), lambda i: (0, 0)),
            pl.BlockSpec((COND_ROWS, D), lambda i: (0, 0)),
            pl.BlockSpec((COND_ROWS, D), lambda i: (0, 0)),
            pl.BlockSpec((n_exp, D), lambda i: (0, 0)),
            pl.BlockSpec((n_exp, 1), lambda i: (0, 0)),
        ],
        out_specs=[
            pl.BlockSpec((tm // ROW_GROUP, ROW_GROUP, D // 2), lambda i: (i, 0, 0)),
            pl.BlockSpec((TOP_K, tm), lambda i: (0, i)),
            pl.BlockSpec((TOP_K, tm), lambda i: (0, i)),
            pl.BlockSpec((TOP_K, tm), lambda i: (0, i)),
            pl.BlockSpec((n_exp, 1), lambda i: (0, 0)),
        ],
        out_shape=[
            jax.ShapeDtypeStruct((n_rows // ROW_GROUP, ROW_GROUP, D // 2), I32),
            jax.ShapeDtypeStruct((TOP_K, n_rows), I32),
            jax.ShapeDtypeStruct((TOP_K, n_rows), F32),
            jax.ShapeDtypeStruct((TOP_K, n_rows), I32),
            jax.ShapeDtypeStruct((n_exp, 1), F32),
        ],
        compiler_params=_cp(("arbitrary",)),
        name="router",
    )(x, g.reshape(1, D), sh, sc, w_router.T, b_router.reshape(n_exp, 1))


def _vrow(ref, group, sub):
    return ref.at[group, pl.ds(sub, 1)]


def _hrow(ref, row):
    return ref.at[pl.ds(row, 1)]


def _dispatch_kernel(pos_ref, h_ref, xs_ref, sem, *, tm):
    def issue(g, carry):
        base = g * ROW_GROUP
        for u in range(ROW_GROUP):
            for j in range(TOP_K):
                pltpu.make_async_copy(_vrow(h_ref, g, u), _hrow(xs_ref, pos_ref[j * tm + base + u]),
                                      sem).start(priority=j % 2)
        return carry

    lax.fori_loop(0, tm // ROW_GROUP, issue, 0)
    for j in range(TOP_K):
        pltpu.make_async_copy(xs_ref.at[pl.ds(0, tm)], xs_ref.at[pl.ds(0, tm)], sem).wait()


def dispatch(pos, h2p, n_slots, tm):
    G, _, W = h2p.shape
    T = G * ROW_GROUP
    return pl.pallas_call(
        functools.partial(_dispatch_kernel, tm=tm),
        grid=(T // tm,),
        in_specs=[
            pl.BlockSpec((TOP_K * tm,), lambda i: (i,), memory_space=pltpu.SMEM),
            pl.BlockSpec((tm // ROW_GROUP, ROW_GROUP, W), lambda i: (i, 0, 0)),
        ],
        out_specs=pl.BlockSpec(memory_space=pl.ANY),
        out_shape=jax.ShapeDtypeStruct((n_slots, W), I32),
        scratch_shapes=[pltpu.SemaphoreType.DMA(())],
        compiler_params=_cp(("arbitrary",), has_side_effects=True),
        name="dispatch",
    )(pos, h2p)


def _experts_kernel(gid_ref, nvalid_ref, nused_ref, wslot_ref, nxt_ref, xs_ref, wg_hbm, wu_hbm, wd_hbm, ys_ref,
                    wg_st, wu_st, wd_st, wg_bf, wu_bf, wd_bf, obuf, wsem, osem, *, tr, l, n_tiles):
    i = pl.program_id(0)
    n_used = nused_ref[0]
    first = jnp.logical_or(i == 0, gid_ref[i] != gid_ref[jnp.maximum(i - 1, 0)])
    n_lane_tiles = obuf.shape[2] // LANES

    def weight_copies(e, slot):
        return [pltpu.make_async_copy(src.at[l, e], dst.at[slot], wsem.at[slot])
                for src, dst in ((wg_hbm, wg_st), (wu_hbm, wu_st), (wd_hbm, wd_st))]

    def out_copies(tile, slot):
        return [pltpu.make_async_copy(obuf.at[slot, :, pl.ds(c * LANES, LANES)],
                                      ys_ref.at[pl.ds(tile * tr, tr), c], osem.at[slot])
                for c in range(n_lane_tiles)]

    @pl.when(i == 0)
    def _():
        for cp in weight_copies(gid_ref[0], wslot_ref[0]):
            cp.start(priority=1)

    @pl.when(first)
    def _():
        slot = wslot_ref[i]
        for cp in weight_copies(gid_ref[i], slot):
            cp.wait()
        wg_bf[...] = wg_st[slot].astype(BF16)
        wu_bf[...] = wu_st[slot].astype(BF16)
        wd_bf[...] = wd_st[slot].astype(BF16)

        @pl.when(nxt_ref[i] >= 0)
        def _():
            for cp in weight_copies(nxt_ref[i], 1 - slot):
                cp.start(priority=1)

    @pl.when(jnp.logical_and(i >= 2, i - 2 < n_used))
    def _():
        for cp in out_copies(i - 2, i % 2):
            cp.wait()

    @pl.when(i < n_used)
    def _():
        half = xs_ref.shape[1]
        valid = lax.broadcasted_iota(I32, (tr, 1), 0) < nvalid_ref[i]
        lo, hi = _unpack_halves(xs_ref[...])
        lo = jnp.where(valid, lo, 0.0).astype(BF16)
        hi = jnp.where(valid, hi, 0.0).astype(BF16)

        def proj(w_bf):
            return (jnp.dot(lo, w_bf[:half, :], preferred_element_type=F32)
                    + jnp.dot(hi, w_bf[half:, :], preferred_element_type=F32))

        hh = (jax.nn.silu(proj(wg_bf)) * proj(wu_bf)).astype(BF16)
        slot = i % 2
        obuf[slot] = _pack_halves(jnp.dot(hh, wd_bf[...], preferred_element_type=F32))
        for cp in out_copies(i, slot):
            cp.start()

    @pl.when(i == n_tiles - 1)
    def _():
        @pl.when(jnp.logical_and(i >= 1, i - 1 < n_used))
        def _():
            for cp in out_copies(i - 1, (i - 1) % 2):
                cp.wait()

        @pl.when(i < n_used)
        def _():
            for cp in out_copies(i, i % 2):
                cp.wait()


def experts(xs, gid, nvalid, nused, wslot, nxt, w_ge, w_ue, w_de, l, *, tr):
    P, W = xs.shape
    D = 2 * W
    F = w_ge.shape[3]
    n_tiles = P // tr

    def row_idx(i, gid, nvalid, nused, wslot, nxt):
        return (jnp.minimum(i, nused[0] - 1), 0)

    return pl.pallas_call(
        functools.partial(_experts_kernel, tr=tr, l=l, n_tiles=n_tiles),
        grid_spec=pltpu.PrefetchScalarGridSpec(
            num_scalar_prefetch=5,
            grid=(n_tiles,),
            in_specs=[
                pl.BlockSpec((tr, W), row_idx),
                pl.BlockSpec(memory_space=pl.ANY),
                pl.BlockSpec(memory_space=pl.ANY),
                pl.BlockSpec(memory_space=pl.ANY),
            ],
            out_specs=pl.BlockSpec(memory_space=pl.ANY),
            scratch_shapes=[
                pltpu.VMEM((2, D, F), F32), pltpu.VMEM((2, D, F), F32), pltpu.VMEM((2, F, D), F32),
                pltpu.VMEM((D, F), BF16), pltpu.VMEM((D, F), BF16), pltpu.VMEM((F, D), BF16),
                pltpu.VMEM((2, tr, W), I32),
                pltpu.SemaphoreType.DMA((2,)), pltpu.SemaphoreType.DMA((2,)),
            ],
        ),
        out_shape=jax.ShapeDtypeStruct((P, W // LANES, LANES), I32),
        compiler_params=_cp(("arbitrary",), has_side_effects=True),
        name="experts",
    )(gid, nvalid, nused, wslot, nxt, xs, w_ge, w_ue, w_de)


def _combine_kernel(pos_ref, posn_ref, w_ref, x_ref, h_ref, ga_ref, wgs_ref, wus_ref, wds_ref, ys_ref,
                    o_ref, gbuf2, acc, sems, *, tm, L, B):
    i = pl.program_id(0)
    r = _cond_row(i, tm, L, B)
    half = h_ref.shape[2]
    n_slab = gbuf2.shape[3]
    slot = i % 2
    gbuf = gbuf2.at[slot]
    sem = sems.at[slot]

    def issue_tile(p_ref, dst, dsem):
        def issue(t, carry):
            for j in range(TOP_K):
                pltpu.make_async_copy(ys_ref.at[p_ref[j * tm + t]], dst.at[j, t], dsem).start(priority=j % 2)
            return carry

        lax.fori_loop(0, tm, issue, 0, unroll=2)

    @pl.when(i == 0)
    def _():
        issue_tile(pos_ref, gbuf, sem)

    lo, hi = _unpack_halves(h_ref[...].reshape(tm, half))
    lo = lo.astype(BF16)
    hi = hi.astype(BF16)

    def proj(w_bf):
        return (jnp.dot(lo, w_bf[:half, :], preferred_element_type=F32)
                + jnp.dot(hi, w_bf[half:, :], preferred_element_type=F32))

    hh = (jax.nn.silu(proj(wgs_ref)) * proj(wus_ref)).astype(BF16)
    shared = jnp.dot(hh, wds_ref[...], preferred_element_type=F32)

    for j in range(TOP_K):
        pltpu.make_async_copy(ys_ref.at[pl.ds(0, tm)], gbuf.at[j], sem).wait()

    @pl.when(i + 1 < pl.num_programs(0))
    def _():
        issue_tile(posn_ref, gbuf2.at[1 - slot], sems.at[1 - slot])

    def token(t, carry):
        a_lo = jnp.zeros((n_slab, LANES), F32)
        a_hi = jnp.zeros((n_slab, LANES), F32)
        for j in range(TOP_K):
            y_lo, y_hi = _unpack_halves(gbuf[j, t])
            w = w_ref[j * tm + t]
            a_lo = a_lo + w * y_lo
            a_hi = a_hi + w * y_hi
        acc[t, 0:n_slab, :] = a_lo
        acc[t, n_slab:2 * n_slab, :] = a_hi
        return carry

    lax.fori_loop(0, tm, token, 0, unroll=2)

    ga = ga_ref[pl.ds(r, 1), :]
    for c in range(2 * n_slab):
        cols = slice(c * LANES, (c + 1) * LANES)
        o_ref[:, cols] = x_ref[:, cols] + ga[:, cols] * (shared[:, cols] + acc[:, c, :])


def combine(pos, x, h2p, wsel_flat, ga, wgs_bf, wus_bf, wds_bf, ys, *, L, B, n_rows, tm):
    D = x.shape[1]
    W = D // 2
    F = wgs_bf.shape[1]
    n_slab = W // LANES
    n_steps = n_rows // tm
    return pl.pallas_call(
        functools.partial(_combine_kernel, tm=tm, L=L, B=B),
        grid=(n_steps,),
        in_specs=[
            pl.BlockSpec((TOP_K * tm,), lambda i: (i,), memory_space=pltpu.SMEM),
            pl.BlockSpec((TOP_K * tm,), lambda i: (jnp.minimum(i + 1, n_steps - 1),), memory_space=pltpu.SMEM),
            pl.BlockSpec((TOP_K * tm,), lambda i: (i,), memory_space=pltpu.SMEM),
            pl.BlockSpec((tm, D), lambda i: (i, 0)),
            pl.BlockSpec((tm // ROW_GROUP, ROW_GROUP, W), lambda i: (i, 0, 0)),
            pl.BlockSpec((COND_ROWS, D), lambda i: (0, 0)),
            pl.BlockSpec((D, F), lambda i: (0, 0)),
            pl.BlockSpec((D, F), lambda i: (0, 0)),
            pl.BlockSpec((F, D), lambda i: (0, 0)),
            pl.BlockSpec(memory_space=pl.ANY),
        ],
        out_specs=pl.BlockSpec((tm, D), lambda i: (i, 0)),
        out_shape=jax.ShapeDtypeStruct((n_rows, D), F32),
        scratch_shapes=[pltpu.VMEM((2, TOP_K, tm, n_slab, LANES), I32),
                        pltpu.VMEM((tm, 2 * n_slab, LANES), F32),
                        pltpu.SemaphoreType.DMA((2,))],
        compiler_params=_cp(("arbitrary",)),
        name="combine",
    )(pos, pos, wsel_flat, x, h2p, ga, wgs_bf, wus_bf, wds_bf, ys)


def _tile_major(a, tile):
    k, T = a.shape
    return a.reshape(k, T // tile, tile).transpose(1, 0, 2).reshape(-1)


def _dispatch_plan(cnt, eidx, rank, *, tr, n_tiles):
    n_exp = cnt.shape[0]
    e_ids = jnp.arange(n_exp, dtype=I32)
    tiles_e = (cnt + tr - 1) // tr
    tile_end = jnp.cumsum(tiles_e)
    tile_start = tile_end - tiles_e

    def lookup(table, idx, keys=e_ids):
        return jnp.sum(jnp.where(idx[..., None] == keys, table, 0), axis=-1)

    pos = lookup(tile_start * tr, eidx) + rank
    n_used = tile_end[-1]
    ti = jnp.arange(n_tiles, dtype=I32)
    ti_c = jnp.minimum(ti, n_used - 1)
    gid = jnp.sum((ti_c[:, None] >= tile_end[None, :]).astype(I32), axis=1)
    left = lookup(cnt, gid) - (ti - lookup(tile_start, gid)) * tr
    nvalid = jnp.where(ti < n_used, jnp.clip(left, 0, tr), 0)
    changed = jnp.concatenate([jnp.ones((1,), I32), (gid[1:] != gid[:-1]).astype(I32)])
    wslot = (jnp.cumsum(changed) - 1) % 2
    after = lookup(tile_end, gid)
    nxt = jnp.where(after < n_used, lookup(gid, jnp.minimum(after, n_tiles - 1), ti), -1)
    as_i32 = lambda a: a.astype(I32)
    return (as_i32(pos), as_i32(gid), as_i32(nvalid), n_used.reshape(1).astype(I32), as_i32(wslot), as_i32(nxt))


def kernel(x, c, ctx, c_ctx, w_ada, b_ada, g_norm1, g_norm2, w_in, g_q, g_k, w_fourier, w_dw, b_dw, g_conv_ln, b_conv_ln, w_pw, b_pw, g_sgu, w_spatial, b_spatial, w_out, w_router, b_router, w_gate_e, w_up_e, w_down_e, w_gate_s, w_up_s, w_down_s):
    B, L, D = x.shape
    Lc = ctx.shape[1]
    depth = w_ada.shape[0]
    assert B + 1 <= COND_ROWS and L % Lc == 0 and L % GRID_W == 0
    d_fourier = w_fourier.shape[1] * w_fourier.shape[2]
    d_conv = w_pw.shape[1]
    d_sgu = g_sgu.shape[1]
    d_attn = D - d_fourier - d_conv - d_sgu
    kv_w = d_attn // GQA_GROUP
    v_end = d_attn + 2 * kv_w
    f_end = v_end + d_fourier
    c_end = f_end + 2 * d_conv
    d_in = c_end + 2 * d_sgu
    assert w_in.shape[2] == d_in
    n_exp = w_router.shape[2]
    n_lat = B * L
    T_all = n_lat + B * Lc
    cb = D // 8
    exp_tr = 256
    tok_tile = _pick(L, 256, 128)
    comb_tile = 128

    assert n_exp // EXPERTS_PER_GROUP == 8 and (B * Lc) % _pick(L, 512) == 0
    assert all(c % cb == 0 for c in (v_end, f_end, c_end, d_in))
    fh = w_gate_s.shape[2]

    cond = jnp.concatenate([c, c_ctx[None, :], jnp.zeros((COND_ROWS - B - 1, D), F32)], axis=0)
    mod = modulation(cond, w_ada, b_ada)
    mod = mod.reshape(depth, COND_ROWS, N_MOD, D).transpose(0, 2, 1, 3)

    xa = jnp.concatenate([x.reshape(n_lat, D), ctx.reshape(B * Lc, D)], axis=0)

    for l in range(depth):
        last = l == depth - 1
        n_rows = n_lat if last else T_all
        sh1, sc1, ga1, sh2, sc2, ga2 = (mod[l, m] for m in range(N_MOD))

        w_out_bf = cast_cols(w_out, l, 0, D, cb)
        w_pw_bf = cast_cols(w_pw, l, 0, d_conv, _pick(d_conv, 512, 128))
        wgs_bf = cast_cols(w_gate_s, l, 0, fh, fh)
        wus_bf = cast_cols(w_up_s, l, 0, fh, fh)
        wds_bf = cast_cols(w_down_s, l, 0, D, cb)

        h = norm_mod(xa, g_norm1[l], sh1, sc1, L=L, B=B)
        z_qkv = matmul(h, w_in, l, 0, v_end, T_all, cb)
        z_f = matmul(h, w_in, l, v_end, d_fourier, n_rows, cb)
        z_c = matmul(h, w_in, l, f_end, 2 * d_conv, n_rows, cb)
        z_s = matmul(h, w_in, l, c_end, 2 * d_sgu, n_rows, cb)
        qn, kn = qk_norm_rope(z_qkv, g_q[l], g_k[l], L=L, B=B, d_attn=d_attn, kv_w=kv_w)
        y_a = attention(qn, kn, z_qkv, q_row0=0, Lq=L, segs=[(0, L), (n_lat, Lc)],
                        B=B, d_attn=d_attn, kv_w=kv_w)
        y_f = fourier(z_f, w_fourier[l], row0=0, Ls=L, B=B)
        if not last:
            y_a = (y_a, attention(qn, kn, z_qkv, q_row0=n_lat, Lq=Lc, segs=[(n_lat, Lc)],
                                  B=B, d_attn=d_attn, kv_w=kv_w))
            y_f = (y_f, fourier(z_f, w_fourier[l], row0=n_lat, Ls=Lc, B=B))
        y_c = conv_module(z_c, w_dw[l], b_dw[l], g_conv_ln[l], b_conv_ln[l], w_pw_bf, b_pw[l],
                          L=L, Lc=Lc, B=B, n_rows=n_rows)
        y_s = sgu_mixer(z_s, g_sgu[l], w_spatial[l], b_spatial[l], n_rows=n_rows)
        xa = out_proj([y_a, y_f, y_c, y_s], w_out_bf, xa, ga1, L=L, B=B, n_rows=n_rows)

        h2p, eidx, wsel, rank, cnt = router(xa, g_norm2[l], sh2, sc2, w_router[l], b_router[l],
                                            L=L, B=B, n_rows=n_rows)
        n_tiles = -(-n_rows * TOP_K // exp_tr) + n_exp
        pos, gid, nvalid, nused, wslot, nxt = _dispatch_plan(
            cnt[:, 0].astype(I32), eidx, rank, tr=exp_tr, n_tiles=n_tiles)
        xs = dispatch(_tile_major(pos, tok_tile), h2p, n_tiles * exp_tr, tok_tile)
        ys = experts(xs, gid, nvalid, nused, wslot, nxt, w_gate_e, w_up_e, w_down_e, l, tr=exp_tr)
        xa = combine(_tile_major(pos, comb_tile), xa, h2p, _tile_major(wsel, comb_tile), ga2,
                     wgs_bf, wus_bf, wds_bf, ys, L=L, B=B, n_rows=n_rows, tm=comb_tile)

    return xa.reshape(B, L, D)
```

```python
import functools

import jax
import jax.numpy as jnp
import numpy as np
from jax import lax
from jax.experimental import pallas as pl
from jax.experimental.pallas import tpu as pltpu

F32 = jnp.float32
BF16 = jnp.bfloat16
I32 = jnp.int32

HEAD_DIM = 128
GRID_W = 64
ROPE_THETA = 10000.0
NORM_EPS = 1e-6
GQA_GROUP = 4
N_FOURIER_GROUPS = 4
CONV_WIDTH = 31
CONV_PAD = CONV_WIDTH // 2
CONV_HALO = 16
SGU_CHUNK = 128
N_MOD = 6
TOP_K = 8
EXPERTS_PER_GROUP = 8
TOPK_GROUPS = 4
ROUTED_SCALE = 2.5
COND_ROWS = 8
ROW_GROUP = 8
LANES = 128
CONV_ROW_BLOCK = 64

VMEM_LIMIT = 56 * 1024 * 1024


def _cp(sem, vmem=VMEM_LIMIT, **kw):
    return pltpu.CompilerParams(dimension_semantics=sem, vmem_limit_bytes=vmem, **kw)


def _pick(n, pref, mult=8):
    if n <= pref:
        return n
    for t in range(pref - pref % mult, 0, -mult):
        if n % t == 0:
            return t
    raise ValueError(f"no tile for {n} <= {pref}")


def _rms(t, eps=NORM_EPS):
    return t * lax.rsqrt(jnp.mean(t * t, axis=-1, keepdims=True) + eps)


_HI16 = -65536


def _pack_halves(v):
    half = v.shape[1] // 2
    lo = lax.bitcast_convert_type(v[:, :half].astype(BF16).astype(F32), I32)
    hi = lax.bitcast_convert_type(v[:, half:].astype(BF16).astype(F32), I32)
    return lax.shift_right_logical(lo, 16) | (hi & _HI16)


def _unpack_halves(p):
    lo = lax.bitcast_convert_type(lax.shift_left(p, 16), F32)
    hi = lax.bitcast_convert_type(p & _HI16, F32)
    return lo, hi


def _cast_kernel(x_ref, o_ref):
    o_ref[...] = x_ref[...].astype(o_ref.dtype)


def cast_cols(w, l, col0, ncols, cb):
    _, R, _ = w.shape
    tr = _pick(R, 512)
    return pl.pallas_call(
        _cast_kernel,
        grid=(R // tr, ncols // cb),
        in_specs=[pl.BlockSpec((None, tr, cb), lambda i, j: (l, i, col0 // cb + j))],
        out_specs=pl.BlockSpec((tr, cb), lambda i, j: (i, j)),
        out_shape=jax.ShapeDtypeStruct((R, ncols), BF16),
        compiler_params=_cp(("parallel", "parallel")),
        name="cast_bf16",
    )(w)


def _modulation_kernel(c_ref, w_ref, b_ref, o_ref):
    s = jax.nn.silu(c_ref[...]).astype(BF16)
    o_ref[...] = jnp.dot(s, w_ref[...].astype(BF16), preferred_element_type=F32) + b_ref[...]


def modulation(cond, w_ada, b_ada):
    depth, D, N = w_ada.shape
    tn = _pick(N, 512, 128)
    return pl.pallas_call(
        _modulation_kernel,
        grid=(depth, N // tn),
        in_specs=[
            pl.BlockSpec((COND_ROWS, D), lambda l, j: (0, 0)),
            pl.BlockSpec((None, D, tn), lambda l, j: (l, 0, j)),
            pl.BlockSpec((None, 1, tn), lambda l, j: (l, 0, j)),
        ],
        out_specs=pl.BlockSpec((None, COND_ROWS, tn), lambda l, j: (l, 0, j)),
        out_shape=jax.ShapeDtypeStruct((depth, COND_ROWS, N), F32),
        compiler_params=_cp(("parallel", "parallel")),
        name="modulation",
    )(cond, w_ada, b_ada.reshape(depth, 1, N))


def _cond_row(i, tm, L, B):
    return jnp.minimum((i * tm) // L, B)


def _norm_mod_kernel(x_ref, g_ref, sh_ref, sc_ref, o_ref, *, tm, L, B):
    r = _cond_row(pl.program_id(0), tm, L, B)
    xn = _rms(x_ref[...]) * g_ref[...]
    o_ref[...] = (xn * (1.0 + sc_ref[pl.ds(r, 1), :]) + sh_ref[pl.ds(r, 1), :]).astype(o_ref.dtype)


def norm_mod(x, g, sh, sc, *, L, B):
    T, D = x.shape
    tm = _pick(L, 512)
    return pl.pallas_call(
        functools.partial(_norm_mod_kernel, tm=tm, L=L, B=B),
        grid=(T // tm,),
        in_specs=[
            pl.BlockSpec((tm, D), lambda i: (i, 0)),
            pl.BlockSpec((1, D), lambda i: (0, 0)),
            pl.BlockSpec((COND_ROWS, D), lambda i: (0, 0)),
            pl.BlockSpec((COND_ROWS, D), lambda i: (0, 0)),
        ],
        out_specs=pl.BlockSpec((tm, D), lambda i: (i, 0)),
        out_shape=jax.ShapeDtypeStruct((T, D), BF16),
        compiler_params=_cp(("parallel",)),
        name="norm_mod",
    )(x, g.reshape(1, D), sh, sc)


def _mm_kernel(a_ref, w_ref, o_ref, w_bf):
    @pl.when(pl.program_id(1) == 0)
    def _():
        w_bf[...] = w_ref[...].astype(BF16)

    o_ref[...] = jnp.dot(a_ref[...], w_bf[...], preferred_element_type=F32).astype(o_ref.dtype)


def matmul(a, w, l, col0, ncols, n_rows, tn, out_dtype=BF16):
    _, K = a.shape
    tm = _pick(n_rows, 1024)
    return pl.pallas_call(
        _mm_kernel,
        grid=(ncols // tn, n_rows // tm),
        in_specs=[
            pl.BlockSpec((tm, K), lambda j, i: (i, 0)),
            pl.BlockSpec((None, K, tn), lambda j, i: (l, 0, col0 // tn + j)),
        ],
        out_specs=pl.BlockSpec((tm, tn), lambda j, i: (i, j)),
        out_shape=jax.ShapeDtypeStruct((n_rows, ncols), out_dtype),
        scratch_shapes=[pltpu.VMEM((K, tn), BF16)],
        compiler_params=_cp(("arbitrary", "arbitrary")),
        name="in_proj",
    )(a, w)


def _rope_tables(L, tm):
    rows = L // GRID_W
    row = np.repeat(np.arange(rows), GRID_W).astype(np.float32)
    col = np.tile(np.arange(GRID_W), rows).astype(np.float32)
    axis_dim = HEAD_DIM // 2
    inv = (np.float32(ROPE_THETA) ** (-np.arange(0, axis_dim, 2, dtype=np.float32) / axis_dim)).astype(np.float32)
    ang = np.concatenate([row[:, None] * inv, col[:, None] * inv], axis=-1).astype(np.float64)
    cos = np.repeat(np.cos(ang), 2, axis=-1)
    sin = np.repeat(np.sin(ang), 2, axis=-1)
    sign = np.tile(np.array([-1.0, 1.0]), HEAD_DIM // 2)
    cos = np.concatenate([cos, np.ones((tm, HEAD_DIM))], axis=0)
    sin = np.concatenate([sin * sign, np.zeros((tm, HEAD_DIM))], axis=0)
    return jnp.asarray(cos, F32), jnp.asarray(sin, F32)


def _qk_kernel(q_ref, k_ref, cos_ref, sin_ref, gq_ref, gk_ref, qo_ref, ko_ref, *, n_q, n_kv):
    cos = cos_ref[...]
    sin = sin_ref[...]
    even = (lax.broadcasted_iota(I32, cos.shape, 1) & 1) == 0

    def norm_rope(t, g):
        n = _rms(t.astype(F32)) * g
        partner = jnp.where(even, pltpu.roll(n, HEAD_DIM - 1, 1), pltpu.roll(n, 1, 1))
        return n * cos + partner * sin

    q_scale = HEAD_DIM ** -0.5
    for h in range(n_q):
        sl = slice(h * HEAD_DIM, (h + 1) * HEAD_DIM)
        qo_ref[:, sl] = (norm_rope(q_ref[:, sl], gq_ref[...]) * q_scale).astype(qo_ref.dtype)
    for h in range(n_kv):
        sl = slice(h * HEAD_DIM, (h + 1) * HEAD_DIM)
        ko_ref[:, sl] = norm_rope(k_ref[:, sl], gk_ref[...]).astype(ko_ref.dtype)


def qk_norm_rope(z_qkv, g_q, g_k, *, L, B, d_attn, kv_w):
    T = z_qkv.shape[0]
    tm = _pick(L, 256)
    cos, sin = _rope_tables(L, tm)
    n_lat = B * L // tm
    per_seq = L // tm

    def tab_idx(i):
        return (jnp.where(i < n_lat, i % per_seq, per_seq), 0)

    return pl.pallas_call(
        functools.partial(_qk_kernel, n_q=d_attn // HEAD_DIM, n_kv=kv_w // HEAD_DIM),
        grid=(T // tm,),
        in_specs=[
            pl.BlockSpec((tm, d_attn), lambda i: (i, 0)),
            pl.BlockSpec((tm, kv_w), lambda i: (i, d_attn // kv_w)),
            pl.BlockSpec((tm, HEAD_DIM), tab_idx),
            pl.BlockSpec((tm, HEAD_DIM), tab_idx),
            pl.BlockSpec((1, HEAD_DIM), lambda i: (0, 0)),
            pl.BlockSpec((1, HEAD_DIM), lambda i: (0, 0)),
        ],
        out_specs=[
            pl.BlockSpec((tm, d_attn), lambda i: (i, 0)),
            pl.BlockSpec((tm, kv_w), lambda i: (i, 0)),
        ],
        out_shape=[
            jax.ShapeDtypeStruct((T, d_attn), BF16),
            jax.ShapeDtypeStruct((T, kv_w), BF16),
        ],
        compiler_params=_cp(("parallel",)),
        name="qk_norm_rope",
    )(z_qkv, z_qkv, cos, sin, g_q.reshape(1, HEAD_DIM), g_k.reshape(1, HEAD_DIM))


def _attn_kernel(*refs, n_seg):
    q_ref = refs[0]
    k_refs = refs[1:1 + n_seg]
    v_refs = refs[1 + n_seg:1 + 2 * n_seg]
    o_ref = refs[-1]
    v_ext = [jnp.concatenate([v[...], jnp.ones(v.shape, BF16)], axis=1) for v in v_refs]
    for h in range(GQA_GROUP):
        sl = slice(h * HEAD_DIM, (h + 1) * HEAD_DIM)
        q = q_ref[:, sl]
        s = [lax.dot_general(q, k[...], (((1,), (1,)), ((), ())), preferred_element_type=F32)
             for k in k_refs]
        m = functools.reduce(jnp.maximum, [jnp.max(x, axis=-1, keepdims=True) for x in s])
        p = [jnp.exp((x - m).astype(BF16)) for x in s]
        oe = functools.reduce(jnp.add, [jnp.dot(x, v, preferred_element_type=F32) for x, v in zip(p, v_ext)])
        o_ref[:, sl] = (oe[:, :HEAD_DIM] / oe[:, HEAD_DIM:HEAD_DIM + 1]).astype(o_ref.dtype)


def attention(qn, kn, z_qkv, *, q_row0, Lq, segs, B, d_attn, kv_w):
    n_kv = kv_w // HEAD_DIM
    gw = GQA_GROUP * HEAD_DIM
    tq = _pick(Lq, 512)
    nq = Lq // tq
    v_col0 = (d_attn + kv_w) // HEAD_DIM

    def q_idx(b, g, qi):
        return (q_row0 // tq + b * nq + qi, g)

    in_specs = [pl.BlockSpec((tq, gw), q_idx)]
    args = [qn]
    for row0, ln in segs:
        in_specs.append(pl.BlockSpec((ln, HEAD_DIM), lambda b, g, qi, row0=row0, ln=ln: (row0 // ln + b, g)))
        args.append(kn)
    for row0, ln in segs:
        in_specs.append(
            pl.BlockSpec((ln, HEAD_DIM), lambda b, g, qi, row0=row0, ln=ln: (row0 // ln + b, v_col0 + g)))
        args.append(z_qkv)
    return pl.pallas_call(
        functools.partial(_attn_kernel, n_seg=len(segs)),
        grid=(B, n_kv, nq),
        in_specs=in_specs,
        out_specs=pl.BlockSpec((tq, gw), lambda b, g, qi: (b * nq + qi, g)),
        out_shape=jax.ShapeDtypeStruct((B * Lq, d_attn), BF16),
        compiler_params=_cp(("parallel", "parallel", "parallel")),
        name="attention",
    )(*args)


def _dft_tables(n, dtype):
    k = np.arange(n, dtype=np.int64)
    ang = ((k[:, None] * k[None, :]) % n).astype(np.float64) * (2.0 * np.pi / n)
    return jnp.asarray(np.cos(ang), dtype), jnp.asarray(np.sin(ang), dtype)


def _fourier_kernel(t_ref, cl_ref, sl_ref, cc_ref, sc_ref, w_ref, o_ref, *, scale):
    w = w_ref[...]
    hi = lax.Precision.HIGHEST
    mc = jnp.dot(cc_ref[...], w, precision=hi, preferred_element_type=F32).astype(BF16)
    ms = jnp.dot(sc_ref[...], w, precision=hi, preferred_element_type=F32).astype(BF16)
    t = t_ref[...]
    bc = jnp.dot(t, mc, preferred_element_type=F32).astype(BF16)
    bs = jnp.dot(t, ms, preferred_element_type=F32).astype(BF16)
    y = (jnp.dot(cl_ref[...], bc, preferred_element_type=F32)
         - jnp.dot(sl_ref[...], bs, preferred_element_type=F32))
    o_ref[...] = (y * scale).astype(o_ref.dtype)


def fourier(z_f, w_f, *, row0, Ls, B):
    ng, fg, _ = w_f.shape
    cl, sl = _dft_tables(Ls, BF16)
    cc, sc = _dft_tables(fg, F32)
    scale = 1.0 / float(Ls * fg) ** 0.5
    return pl.pallas_call(
        functools.partial(_fourier_kernel, scale=scale),
        grid=(B, ng),
        in_specs=[
            pl.BlockSpec((Ls, fg), lambda b, g: (row0 // Ls + b, g)),
            pl.BlockSpec((Ls, Ls), lambda b, g: (0, 0)),
            pl.BlockSpec((Ls, Ls), lambda b, g: (0, 0)),
            pl.BlockSpec((fg, fg), lambda b, g: (0, 0)),
            pl.BlockSpec((fg, fg), lambda b, g: (0, 0)),
            pl.BlockSpec((None, fg, fg), lambda b, g: (g, 0, 0)),
        ],
        out_specs=pl.BlockSpec((Ls, fg), lambda b, g: (b, g)),
        out_shape=jax.ShapeDtypeStruct((B * Ls, ng * fg), BF16),
        compiler_params=_cp(("parallel", "parallel")),
        name="fourier",
    )(z_f, cl, sl, cc, sc, w_f)


def _conv_kernel(cur_ref, prev_ref, next_ref, wdw_ref, bdw_ref, gln_ref, bln_ref, wpw_ref, bpw_ref,
                 o_ref, gbuf, shifted, cbuf, *, tr, dc, n_lat, per_lat, per_ctx):
    i = pl.program_id(0)
    j = jnp.where(i < n_lat, i % per_lat, (i - n_lat) % per_ctx)
    per = jnp.where(i < n_lat, per_lat, per_ctx)
    first = j == 0
    last = j == per - 1

    def glu(zb):
        return zb[:, :dc].astype(F32) * jax.nn.sigmoid(zb[:, dc:].astype(F32))

    gbuf[0:CONV_HALO, :] = jnp.where(first, 0.0, glu(prev_ref[...]))
    gbuf[CONV_HALO:CONV_HALO + tr, :] = glu(cur_ref[...])
    gbuf[CONV_HALO + tr:, :] = jnp.where(last, 0.0, glu(next_ref[...]))
    n_sh = tr + CONV_HALO + ROW_GROUP
    for s in range(1, ROW_GROUP):
        shifted[s - 1, :, :] = gbuf[s:s + n_sh, :]
    first_off = CONV_HALO - CONV_PAD
    rb = min(tr, CONV_ROW_BLOCK)
    for c in range(dc // LANES):
        cols = slice(c * LANES, (c + 1) * LANES)
        for r0 in range(0, tr, rb):
            a = jnp.broadcast_to(bdw_ref[:, cols], (rb, LANES))
            for k in range(CONV_WIDTH):
                off = first_off + k + r0
                s = off % ROW_GROUP
                src = gbuf[off:off + rb, cols] if s == 0 else shifted[s - 1, off - s:off - s + rb, cols]
                a = a + src * wdw_ref[k:k + 1, cols]
            cbuf[r0:r0 + rb, cols] = a
    acc = cbuf[...]
    mu = jnp.mean(acc, axis=-1, keepdims=True)
    cen = acc - mu
    var = jnp.mean(cen * cen, axis=-1, keepdims=True)
    hn = cen * lax.rsqrt(var + NORM_EPS) * gln_ref[...] + bln_ref[...]
    act = jax.nn.silu(hn).astype(BF16)
    o_ref[...] = (jnp.dot(act, wpw_ref[...], preferred_element_type=F32) + bpw_ref[...]).astype(o_ref.dtype)


def conv_module(z_c, w_dw, b_dw, g_ln, b_ln, w_pw_bf, b_pw, *, L, Lc, B, n_rows):
    dc = w_pw_bf.shape[0]
    tr = _pick(Lc, 256, CONV_HALO)
    hpt = tr // CONV_HALO
    n_halo = z_c.shape[0] // CONV_HALO
    w_pad = jnp.concatenate([w_dw, jnp.zeros((32 - CONV_WIDTH, dc), F32)], axis=0)
    row = lambda v: v.reshape(1, dc)
    return pl.pallas_call(
        functools.partial(_conv_kernel, tr=tr, dc=dc, n_lat=B * L // tr, per_lat=L // tr, per_ctx=Lc // tr),
        grid=(n_rows // tr,),
        in_specs=[
            pl.BlockSpec((tr, 2 * dc), lambda i: (i, 0)),
            pl.BlockSpec((CONV_HALO, 2 * dc), lambda i: (jnp.maximum(i * hpt - 1, 0), 0)),
            pl.BlockSpec((CONV_HALO, 2 * dc), lambda i: (jnp.minimum((i + 1) * hpt, n_halo - 1), 0)),
            pl.BlockSpec((32, dc), lambda i: (0, 0)),
            pl.BlockSpec((1, dc), lambda i: (0, 0)),
            pl.BlockSpec((1, dc), lambda i: (0, 0)),
            pl.BlockSpec((1, dc), lambda i: (0, 0)),
            pl.BlockSpec((dc, dc), lambda i: (0, 0)),
            pl.BlockSpec((1, dc), lambda i: (0, 0)),
        ],
        out_specs=pl.BlockSpec((tr, dc), lambda i: (i, 0)),
        out_shape=jax.ShapeDtypeStruct((n_rows, dc), BF16),
        scratch_shapes=[
            pltpu.VMEM((tr + 2 * CONV_HALO, dc), F32),
            pltpu.VMEM((ROW_GROUP - 1, tr + CONV_HALO + ROW_GROUP, dc), F32),
            pltpu.VMEM((tr, dc), F32),
        ],
        compiler_params=_cp(("parallel",)),
        name="conv_module",
    )(z_c, z_c, z_c, w_pad, row(b_dw), row(g_ln), row(b_ln), w_pw_bf, row(b_pw))


def _sgu_kernel(z_ref, g_ref, ws_ref, bs_ref, o_ref, *, tm, ds, n_heads):
    for h in range(n_heads):
        sl = slice(h * HEAD_DIM, (h + 1) * HEAD_DIM)
        u = jax.nn.gelu(z_ref[:, sl].astype(F32))
        v = jax.nn.gelu(z_ref[:, ds + h * HEAD_DIM:ds + (h + 1) * HEAD_DIM].astype(F32))
        vn = (_rms(v) * g_ref[:, sl]).astype(BF16)
        w = ws_ref[h].astype(BF16)
        b = bs_ref[:, h:h + 1]
        for c in range(tm // SGU_CHUNK):
            rows = slice(c * SGU_CHUNK, (c + 1) * SGU_CHUNK)
            s = jnp.dot(w, vn[rows, :], preferred_element_type=F32) + b
            o_ref[rows, sl] = (u[rows, :] * s).astype(o_ref.dtype)


def sgu_mixer(z_s, g_sgu, w_s, b_s, *, n_rows):
    n_heads = w_s.shape[0]
    ds = n_heads * HEAD_DIM
    tm = _pick(n_rows, 256, SGU_CHUNK)
    return pl.pallas_call(
        functools.partial(_sgu_kernel, tm=tm, ds=ds, n_heads=n_heads),
        grid=(n_rows // tm,),
        in_specs=[
            pl.BlockSpec((tm, 2 * ds), lambda i: (i, 0)),
            pl.BlockSpec((1, ds), lambda i: (0, 0)),
            pl.BlockSpec((n_heads, SGU_CHUNK, SGU_CHUNK), lambda i: (0, 0, 0)),
            pl.BlockSpec((SGU_CHUNK, n_heads), lambda i: (0, 0)),
        ],
        out_specs=pl.BlockSpec((tm, ds), lambda i: (i, 0)),
        out_shape=jax.ShapeDtypeStruct((n_rows, ds), BF16),
        compiler_params=_cp(("parallel",)),
        name="sgu_mixer",
    )(z_s, g_sgu.reshape(1, ds), w_s, b_s.T)


def _out_proj_kernel(*refs, tm, L, B, widths, split, n_lat_tiles):
    n_y = sum(2 if sp else 1 for sp in split)
    y_refs = refs[:n_y]
    w_ref, x_ref, ga_ref, o_ref = refs[n_y:]
    i = pl.program_id(1)
    r = _cond_row(i, tm, L, B)
    acc = None
    k0 = 0
    p = 0
    for wd, sp in zip(widths, split):
        if sp:
            y = jnp.where(i < n_lat_tiles, y_refs[p][...], y_refs[p + 1][...])
            p += 2
        else:
            y = y_refs[p][...]
            p += 1
        part = jnp.dot(y, w_ref[k0:k0 + wd, :], preferred_element_type=F32)
        acc = part if acc is None else acc + part
        k0 += wd
    o_ref[...] = x_ref[...] + ga_ref[pl.ds(r, 1), :] * acc


def out_proj(ys, w_out_bf, x, ga, *, L, B, n_rows):
    D = x.shape[1]
    tm = _pick(L, 512)
    tn = _pick(D, 1024, 128)
    n_lat_tiles = B * L // tm
    split = tuple(isinstance(y, tuple) for y in ys)
    widths = tuple((y[0] if sp else y).shape[1] for y, sp in zip(ys, split))
    in_specs, args = [], []
    for y, wd, sp in zip(ys, widths, split):
        if sp:
            in_specs.append(pl.BlockSpec((tm, wd), lambda j, i: (jnp.minimum(i, n_lat_tiles - 1), 0)))
            in_specs.append(pl.BlockSpec((tm, wd), lambda j, i: (jnp.maximum(i - n_lat_tiles, 0), 0)))
            args += [y[0], y[1]]
        else:
            in_specs.append(pl.BlockSpec((tm, wd), lambda j, i: (i, 0)))
            args.append(y)
    in_specs += [
        pl.BlockSpec((sum(widths), tn), lambda j, i: (0, j)),
        pl.BlockSpec((tm, tn), lambda j, i: (i, j)),
        pl.BlockSpec((COND_ROWS, tn), lambda j, i: (0, j)),
    ]
    return pl.pallas_call(
        functools.partial(_out_proj_kernel, tm=tm, L=L, B=B, widths=widths, split=split,
                          n_lat_tiles=n_lat_tiles),
        grid=(D // tn, n_rows // tm),
        in_specs=in_specs,
        out_specs=pl.BlockSpec((tm, tn), lambda j, i: (i, j)),
        out_shape=jax.ShapeDtypeStruct((n_rows, D), F32),
        compiler_params=_cp(("parallel", "parallel")),
        name="out_proj",
    )(*args, w_out_bf, x, ga)


def _router_kernel(x_ref, g_ref, sh_ref, sc_ref, wr_ref, br_ref,
                   h2_ref, eidx_ref, wsel_ref, rank_ref, cnt_ref, *, tm, L, B, n_exp):
    i = pl.program_id(0)
    r = _cond_row(i, tm, L, B)
    xn = _rms(x_ref[...]) * g_ref[...]
    h2 = xn * (1.0 + sc_ref[pl.ds(r, 1), :]) + sh_ref[pl.ds(r, 1), :]
    h2_ref[...] = _pack_halves(h2).reshape(h2_ref.shape)
    def split(v):
        hi_part = v.astype(BF16)
        return hi_part, (v - hi_part.astype(F32)).astype(BF16)

    def nt_dot(a, b):
        return lax.dot_general(a, b, (((1,), (1,)), ((), ())), preferred_element_type=F32)

    w_hi, w_lo = split(wr_ref[...])
    h_hi, h_lo = split(h2)
    logits = nt_dot(w_hi, h_hi) + (nt_dot(w_lo, h_hi) + nt_dot(w_hi, h_lo))
    s = jax.nn.sigmoid(logits)
    sel = s + br_ref[...]
    shape = (n_exp, tm)
    e_io = lax.broadcasted_iota(I32, shape, 0)
    e_f = e_io.astype(F32)
    k_io = e_io & (EXPERTS_PER_GROUP - 1)
    g_io = e_io >> 3
    n_groups = n_exp // EXPERTS_PER_GROUP
    neg = -jnp.inf

    def partner(v, d):
        up = pltpu.roll(v, n_exp - d, 0)
        dn = pltpu.roll(v, d, 0)
        return jnp.where((k_io & d) == 0, up, dn)

    def group_all(v, op):
        d = 1
        while d < EXPERTS_PER_GROUP:
            v = op(v, partner(v, d))
            d *= 2
        return v

    kf = k_io.astype(F32)
    m1 = group_all(sel, jnp.maximum)
    i1 = group_all(jnp.where(sel == m1, kf, float(EXPERTS_PER_GROUP)), jnp.minimum)
    m2 = group_all(jnp.where(kf == i1, neg, sel), jnp.maximum)
    gs = m1 + m2
    beaten = jnp.zeros(shape, I32)
    for d in range(1, n_groups):
        other = pltpu.roll(gs, n_exp - EXPERTS_PER_GROUP * d, 0)
        og = (g_io + d) & (n_groups - 1)
        beats = (other > gs) | ((other == gs) & (og < g_io))
        beaten = beaten + beats.astype(I32)
    val = jnp.where(beaten < TOPK_GROUPS, sel, neg)

    assign = jnp.zeros(shape, F32)
    w_rows = []
    idx_rows = []
    for j in range(TOP_K):
        m = jnp.max(val, axis=0, keepdims=True)
        idx = jnp.min(jnp.where(val == m, e_f, float(n_exp)), axis=0, keepdims=True)
        hit = e_f == idx
        w_rows.append(jnp.sum(jnp.where(hit, s, 0.0), axis=0, keepdims=True))
        val = jnp.where(hit, neg, val)
        assign = assign + hit.astype(F32)
        idx_rows.append(idx)
        eidx_ref[j:j + 1, :] = idx.astype(I32)
    w_sum = functools.reduce(jnp.add, w_rows)
    for j in range(TOP_K):
        wsel_ref[j:j + 1, :] = w_rows[j] / w_sum * ROUTED_SCALE

    @pl.when(i == 0)
    def _():
        cnt_ref[...] = jnp.zeros_like(cnt_ref)

    earlier = (lax.broadcasted_iota(I32, (tm, tm), 0) < lax.broadcasted_iota(I32, (tm, tm), 1)).astype(BF16)
    before = jnp.dot(assign.astype(BF16), earlier, preferred_element_type=F32) + cnt_ref[...]
    for j in range(TOP_K):
        hit = e_f == idx_rows[j]
        rank_ref[j:j + 1, :] = jnp.sum(jnp.where(hit, before, 0.0), axis=0, keepdims=True).astype(I32)
    cnt_ref[...] = cnt_ref[...] + jnp.sum(assign, axis=1, keepdims=True)


def router(x, g, sh, sc, w_router, b_router, *, L, B, n_rows):
    D = x.shape[1]
    n_exp = w_router.shape[1]
    tm = _pick(L, 512, 128)
    assert n_rows % tm == 0
    return pl.pallas_call(
        functools.partial(_router_kernel, tm=tm, L=L, B=B, n_exp=n_exp),
        grid=(n_rows // tm,),
        in_specs=[
            pl.BlockSpec((tm, D), lambda i: (i, 0)),
            pl.BlockSpec((1, D), lambda i: (0, 0)),
            pl.BlockSpec((COND_ROWS, D), lambda i: (0, 0)),
            pl.BlockSpec((COND_ROWS, D), lambda i: (0, 0)),
            pl.BlockSpec((n_exp, D), lambda i: (0, 0)),
            pl.BlockSpec((n_exp, 1), lambda i: (0, 0)),
        ],
        out_specs=[
            pl.BlockSpec((tm // ROW_GROUP, ROW_GROUP, D // 2), lambda i: (i, 0, 0)),
            pl.BlockSpec((TOP_K, tm), lambda i: (0, i)),
            pl.BlockSpec((TOP_K, tm), lambda i: (0, i)),
            pl.BlockSpec((TOP_K, tm), lambda i: (0, i)),
            pl.BlockSpec((n_exp, 1), lambda i: (0, 0)),
        ],
        out_shape=[
            jax.ShapeDtypeStruct((n_rows // ROW_GROUP, ROW_GROUP, D // 2), I32),
            jax.ShapeDtypeStruct((TOP_K, n_rows), I32),
            jax.ShapeDtypeStruct((TOP_K, n_rows), F32),
            jax.ShapeDtypeStruct((TOP_K, n_rows), I32),
            jax.ShapeDtypeStruct((n_exp, 1), F32),
        ],
        compiler_params=_cp(("arbitrary",)),
        name="router",
    )(x, g.reshape(1, D), sh, sc, w_router.T, b_router.reshape(n_exp, 1))


def _vrow(ref, group, sub):
    return ref.at[group, pl.ds(sub, 1)]


def _hrow(ref, row):
    return ref.at[pl.ds(row, 1)]


def _dispatch_kernel(pos_ref, h_ref, xs_ref, sem, *, tm):
    def issue(g, carry):
        base = g * ROW_GROUP
        for u in range(ROW_GROUP):
            for j in range(TOP_K):
                pltpu.make_async_copy(_vrow(h_ref, g, u), _hrow(xs_ref, pos_ref[j * tm + base + u]),
                                      sem).start(priority=j % 2)
        return carry

    lax.fori_loop(0, tm // ROW_GROUP, issue, 0)
    for j in range(TOP_K):
        pltpu.make_async_copy(xs_ref.at[pl.ds(0, tm)], xs_ref.at[pl.ds(0, tm)], sem).wait()


def dispatch(pos, h2p, n_slots, tm):
    G, _, W = h2p.shape
    T = G * ROW_GROUP
    return pl.pallas_call(
        functools.partial(_dispatch_kernel, tm=tm),
        grid=(T // tm,),
        in_specs=[
            pl.BlockSpec((TOP_K * tm,), lambda i: (i,), memory_space=pltpu.SMEM),
            pl.BlockSpec((tm // ROW_GROUP, ROW_GROUP, W), lambda i: (i, 0, 0)),
        ],
        out_specs=pl.BlockSpec(memory_space=pl.ANY),
        out_shape=jax.ShapeDtypeStruct((n_slots, W), I32),
        scratch_shapes=[pltpu.SemaphoreType.DMA(())],
        compiler_params=_cp(("arbitrary",), has_side_effects=True),
        name="dispatch",
    )(pos, h2p)


def _experts_kernel(gid_ref, nvalid_ref, nused_ref, wslot_ref, nxt_ref, xs_ref, wg_hbm, wu_hbm, wd_hbm, ys_ref,
                    wg_st, wu_st, wd_st, wg_bf, wu_bf, wd_bf, obuf, wsem, osem, *, tr, l, n_tiles):
    i = pl.program_id(0)
    n_used = nused_ref[0]
    first = jnp.logical_or(i == 0, gid_ref[i] != gid_ref[jnp.maximum(i - 1, 0)])
    n_lane_tiles = obuf.shape[2] // LANES

    def weight_copies(e, slot):
        return [pltpu.make_async_copy(src.at[l, e], dst.at[slot], wsem.at[slot])
                for src, dst in ((wg_hbm, wg_st), (wu_hbm, wu_st), (wd_hbm, wd_st))]

    def out_copies(tile, slot):
        return [pltpu.make_async_copy(obuf.at[slot, :, pl.ds(c * LANES, LANES)],
                                      ys_ref.at[pl.ds(tile * tr, tr), c], osem.at[slot])
                for c in range(n_lane_tiles)]

    @pl.when(i == 0)
    def _():
        for cp in weight_copies(gid_ref[0], wslot_ref[0]):
            cp.start(priority=1)

    @pl.when(first)
    def _():
        slot = wslot_ref[i]
        for cp in weight_copies(gid_ref[i], slot):
            cp.wait()
        wg_bf[...] = wg_st[slot].astype(BF16)
        wu_bf[...] = wu_st[slot].astype(BF16)
        wd_bf[...] = wd_st[slot].astype(BF16)

        @pl.when(nxt_ref[i] >= 0)
        def _():
            for cp in weight_copies(nxt_ref[i], 1 - slot):
                cp.start(priority=1)

    @pl.when(jnp.logical_and(i >= 2, i - 2 < n_used))
    def _():
        for cp in out_copies(i - 2, i % 2):
            cp.wait()

    @pl.when(i < n_used)
    def _():
        half = xs_ref.shape[1]
        valid = lax.broadcasted_iota(I32, (tr, 1), 0) < nvalid_ref[i]
        lo, hi = _unpack_halves(xs_ref[...])
        lo = jnp.where(valid, lo, 0.0).astype(BF16)
        hi = jnp.where(valid, hi, 0.0).astype(BF16)

        def proj(w_bf):
            return (jnp.dot(lo, w_bf[:half, :], preferred_element_type=F32)
                    + jnp.dot(hi, w_bf[half:, :], preferred_element_type=F32))

        hh = (jax.nn.silu(proj(wg_bf)) * proj(wu_bf)).astype(BF16)
        slot = i % 2
        obuf[slot] = _pack_halves(jnp.dot(hh, wd_bf[...], preferred_element_type=F32))
        for cp in out_copies(i, slot):
            cp.start()

    @pl.when(i == n_tiles - 1)
    def _():
        @pl.when(jnp.logical_and(i >= 1, i - 1 < n_used))
        def _():
            for cp in out_copies(i - 1, (i - 1) % 2):
                cp.wait()

        @pl.when(i < n_used)
        def _():
            for cp in out_copies(i, i % 2):
                cp.wait()


def experts(xs, gid, nvalid, nused, wslot, nxt, w_ge, w_ue, w_de, l, *, tr):
    P, W = xs.shape
    D = 2 * W
    F = w_ge.shape[3]
    n_tiles = P // tr

    def row_idx(i, gid, nvalid, nused, wslot, nxt):
        return (jnp.minimum(i, nused[0] - 1), 0)

    return pl.pallas_call(
        functools.partial(_experts_kernel, tr=tr, l=l, n_tiles=n_tiles),
        grid_spec=pltpu.PrefetchScalarGridSpec(
            num_scalar_prefetch=5,
            grid=(n_tiles,),
            in_specs=[
                pl.BlockSpec((tr, W), row_idx),
                pl.BlockSpec(memory_space=pl.ANY),
                pl.BlockSpec(memory_space=pl.ANY),
                pl.BlockSpec(memory_space=pl.ANY),
            ],
            out_specs=pl.BlockSpec(memory_space=pl.ANY),
            scratch_shapes=[
                pltpu.VMEM((2, D, F), F32), pltpu.VMEM((2, D, F), F32), pltpu.VMEM((2, F, D), F32),
                pltpu.VMEM((D, F), BF16), pltpu.VMEM((D, F), BF16), pltpu.VMEM((F, D), BF16),
                pltpu.VMEM((2, tr, W), I32),
                pltpu.SemaphoreType.DMA((2,)), pltpu.SemaphoreType.DMA((2,)),
            ],
        ),
        out_shape=jax.ShapeDtypeStruct((P, W // LANES, LANES), I32),
        compiler_params=_cp(("arbitrary",), has_side_effects=True),
        name="experts",
    )(gid, nvalid, nused, wslot, nxt, xs, w_ge, w_ue, w_de)


def _combine_kernel(pos_ref, posn_ref, w_ref, x_ref, h_ref, ga_ref, wgs_ref, wus_ref, wds_ref, ys_ref,
                    o_ref, gbuf2, acc, sems, *, tm, L, B):
    i = pl.program_id(0)
    r = _cond_row(i, tm, L, B)
    half = h_ref.shape[2]
    n_slab = gbuf2.shape[3]
    slot = i % 2
    gbuf = gbuf2.at[slot]
    sem = sems.at[slot]

    def issue_tile(p_ref, dst, dsem):
        def issue(t, carry):
            for j in range(TOP_K):
                pltpu.make_async_copy(ys_ref.at[p_ref[j * tm + t]], dst.at[j, t], dsem).start(priority=j % 2)
            return carry

        lax.fori_loop(0, tm, issue, 0, unroll=2)

    @pl.when(i == 0)
    def _():
        issue_tile(pos_ref, gbuf, sem)

    lo, hi = _unpack_halves(h_ref[...].reshape(tm, half))
    lo = lo.astype(BF16)
    hi = hi.astype(BF16)

    def proj(w_bf):
        return (jnp.dot(lo, w_bf[:half, :], preferred_element_type=F32)
                + jnp.dot(hi, w_bf[half:, :], preferred_element_type=F32))

    hh = (jax.nn.silu(proj(wgs_ref)) * proj(wus_ref)).astype(BF16)
    shared = jnp.dot(hh, wds_ref[...], preferred_element_type=F32)

    for j in range(TOP_K):
        pltpu.make_async_copy(ys_ref.at[pl.ds(0, tm)], gbuf.at[j], sem).wait()

    def token_loop(prefetch):
        nbuf = gbuf2.at[1 - slot]
        nsem = sems.at[1 - slot]

        def token(t, carry):
            a_lo = jnp.zeros((n_slab, LANES), F32)
            a_hi = jnp.zeros((n_slab, LANES), F32)
            for j in range(TOP_K):
                y_lo, y_hi = _unpack_halves(gbuf[j, t])
                w = w_ref[j * tm + t]
                a_lo = a_lo + w * y_lo
                a_hi = a_hi + w * y_hi
            acc[t, 0:n_slab, :] = a_lo
            acc[t, n_slab:2 * n_slab, :] = a_hi
            if prefetch:
                for j in range(TOP_K):
                    pltpu.make_async_copy(ys_ref.at[posn_ref[j * tm + t]], nbuf.at[j, t],
                                          nsem).start(priority=j % 2)
            return carry

        lax.fori_loop(0, tm, token, 0, unroll=2)

    not_last = i + 1 < pl.num_programs(0)

    @pl.when(not_last)
    def _():
        token_loop(True)

    @pl.when(jnp.logical_not(not_last))
    def _():
        token_loop(False)

    ga = ga_ref[pl.ds(r, 1), :]
    for c in range(2 * n_slab):
        cols = slice(c * LANES, (c + 1) * LANES)
        o_ref[:, cols] = x_ref[:, cols] + ga[:, cols] * (shared[:, cols] + acc[:, c, :])


def combine(pos, x, h2p, wsel_flat, ga, wgs_bf, wus_bf, wds_bf, ys, *, L, B, n_rows, tm):
    D = x.shape[1]
    W = D // 2
    F = wgs_bf.shape[1]
    n_slab = W // LANES
    n_steps = n_rows // tm
    return pl.pallas_call(
        functools.partial(_combine_kernel, tm=tm, L=L, B=B),
        grid=(n_steps,),
        in_specs=[
            pl.BlockSpec((TOP_K * tm,), lambda i: (i,), memory_space=pltpu.SMEM),
            pl.BlockSpec((TOP_K * tm,), lambda i: (jnp.minimum(i + 1, n_steps - 1),), memory_space=pltpu.SMEM),
            pl.BlockSpec((TOP_K * tm,), lambda i: (i,), memory_space=pltpu.SMEM),
            pl.BlockSpec((tm, D), lambda i: (i, 0)),
            pl.BlockSpec((tm // ROW_GROUP, ROW_GROUP, W), lambda i: (i, 0, 0)),
            pl.BlockSpec((COND_ROWS, D), lambda i: (0, 0)),
            pl.BlockSpec((D, F), lambda i: (0, 0)),
            pl.BlockSpec((D, F), lambda i: (0, 0)),
            pl.BlockSpec((F, D), lambda i: (0, 0)),
            pl.BlockSpec(memory_space=pl.ANY),
        ],
        out_specs=pl.BlockSpec((tm, D), lambda i: (i, 0)),
        out_shape=jax.ShapeDtypeStruct((n_rows, D), F32),
        scratch_shapes=[pltpu.VMEM((2, TOP_K, tm, n_slab, LANES), I32),
                        pltpu.VMEM((tm, 2 * n_slab, LANES), F32),
                        pltpu.SemaphoreType.DMA((2,))],
        compiler_params=_cp(("arbitrary",)),
        name="combine",
    )(pos, pos, wsel_flat, x, h2p, ga, wgs_bf, wus_bf, wds_bf, ys)


def _tile_major(a, tile):
    k, T = a.shape
    return a.reshape(k, T // tile, tile).transpose(1, 0, 2).reshape(-1)


def _dispatch_plan(cnt, eidx, rank, *, tr, n_tiles):
    n_exp = cnt.shape[0]
    e_ids = jnp.arange(n_exp, dtype=I32)
    tiles_e = (cnt + tr - 1) // tr
    tile_end = jnp.cumsum(tiles_e)
    tile_start = tile_end - tiles_e

    def lookup(table, idx, keys=e_ids):
        return jnp.sum(jnp.where(idx[..., None] == keys, table, 0), axis=-1)

    pos = lookup(tile_start * tr, eidx) + rank
    n_used = tile_end[-1]
    ti = jnp.arange(n_tiles, dtype=I32)
    ti_c = jnp.minimum(ti, n_used - 1)
    gid = jnp.sum((ti_c[:, None] >= tile_end[None, :]).astype(I32), axis=1)
    left = lookup(cnt, gid) - (ti - lookup(tile_start, gid)) * tr
    nvalid = jnp.where(ti < n_used, jnp.clip(left, 0, tr), 0)
    changed = jnp.concatenate([jnp.ones((1,), I32), (gid[1:] != gid[:-1]).astype(I32)])
    wslot = (jnp.cumsum(changed) - 1) % 2
    after = lookup(tile_end, gid)
    nxt = jnp.where(after < n_used, lookup(gid, jnp.minimum(after, n_tiles - 1), ti), -1)
    as_i32 = lambda a: a.astype(I32)
    return (as_i32(pos), as_i32(gid), as_i32(nvalid), n_used.reshape(1).astype(I32), as_i32(wslot), as_i32(nxt))


def kernel(x, c, ctx, c_ctx, w_ada, b_ada, g_norm1, g_norm2, w_in, g_q, g_k, w_fourier, w_dw, b_dw, g_conv_ln, b_conv_ln, w_pw, b_pw, g_sgu, w_spatial, b_spatial, w_out, w_router, b_router, w_gate_e, w_up_e, w_down_e, w_gate_s, w_up_s, w_down_s):
    B, L, D = x.shape
    Lc = ctx.shape[1]
    depth = w_ada.shape[0]
    assert B + 1 <= COND_ROWS and L % Lc == 0 and L % GRID_W == 0
    d_fourier = w_fourier.shape[1] * w_fourier.shape[2]
    d_conv = w_pw.shape[1]
    d_sgu = g_sgu.shape[1]
    d_attn = D - d_fourier - d_conv - d_sgu
    kv_w = d_attn // GQA_GROUP
    v_end = d_attn + 2 * kv_w
    f_end = v_end + d_fourier
    c_end = f_end + 2 * d_conv
    d_in = c_end + 2 * d_sgu
    assert w_in.shape[2] == d_in
    n_exp = w_router.shape[2]
    n_lat = B * L
    T_all = n_lat + B * Lc
    cb = D // 8
    exp_tr = 256
    tok_tile = _pick(L, 256, 128)
    comb_tile = 128

    assert n_exp // EXPERTS_PER_GROUP == 8 and (B * Lc) % _pick(L, 512) == 0
    assert all(c % cb == 0 for c in (v_end, f_end, c_end, d_in))
    fh = w_gate_s.shape[2]

    cond = jnp.concatenate([c, c_ctx[None, :], jnp.zeros((COND_ROWS - B - 1, D), F32)], axis=0)
    mod = modulation(cond, w_ada, b_ada)
    mod = mod.reshape(depth, COND_ROWS, N_MOD, D).transpose(0, 2, 1, 3)

    xa = jnp.concatenate([x.reshape(n_lat, D), ctx.reshape(B * Lc, D)], axis=0)

    for l in range(depth):
        last = l == depth - 1
        n_rows = n_lat if last else T_all
        sh1, sc1, ga1, sh2, sc2, ga2 = (mod[l, m] for m in range(N_MOD))

        w_out_bf = cast_cols(w_out, l, 0, D, cb)
        w_pw_bf = cast_cols(w_pw, l, 0, d_conv, _pick(d_conv, 512, 128))
        wgs_bf = cast_cols(w_gate_s, l, 0, fh, fh)
        wus_bf = cast_cols(w_up_s, l, 0, fh, fh)
        wds_bf = cast_cols(w_down_s, l, 0, D, cb)

        h = norm_mod(xa, g_norm1[l], sh1, sc1, L=L, B=B)
        z_qkv = matmul(h, w_in, l, 0, v_end, T_all, cb)
        z_f = matmul(h, w_in, l, v_end, d_fourier, n_rows, cb)
        z_c = matmul(h, w_in, l, f_end, 2 * d_conv, n_rows, cb)
        z_s = matmul(h, w_in, l, c_end, 2 * d_sgu, n_rows, cb)
        qn, kn = qk_norm_rope(z_qkv, g_q[l], g_k[l], L=L, B=B, d_attn=d_attn, kv_w=kv_w)
        y_a = attention(qn, kn, z_qkv, q_row0=0, Lq=L, segs=[(0, L), (n_lat, Lc)],
                        B=B, d_attn=d_attn, kv_w=kv_w)
        y_f = fourier(z_f, w_fourier[l], row0=0, Ls=L, B=B)
        if not last:
            y_a = (y_a, attention(qn, kn, z_qkv, q_row0=n_lat, Lq=Lc, segs=[(n_lat, Lc)],
                                  B=B, d_attn=d_attn, kv_w=kv_w))
            y_f = (y_f, fourier(z_f, w_fourier[l], row0=n_lat, Ls=Lc, B=B))
        y_c = conv_module(z_c, w_dw[l], b_dw[l], g_conv_ln[l], b_conv_ln[l], w_pw_bf, b_pw[l],
                          L=L, Lc=Lc, B=B, n_rows=n_rows)
        y_s = sgu_mixer(z_s, g_sgu[l], w_spatial[l], b_spatial[l], n_rows=n_rows)
        xa = out_proj([y_a, y_f, y_c, y_s], w_out_bf, xa, ga1, L=L, B=B, n_rows=n_rows)

        h2p, eidx, wsel, rank, cnt = router(xa, g_norm2[l], sh2, sc2, w_router[l], b_router[l],
                                            L=L, B=B, n_rows=n_rows)
        n_tiles = -(-n_rows * TOP_K // exp_tr) + n_exp
        pos, gid, nvalid, nused, wslot, nxt = _dispatch_plan(
            cnt[:, 0].astype(I32), eidx, rank, tr=exp_tr, n_tiles=n_tiles)
        xs = dispatch(_tile_major(pos, tok_tile), h2p, n_tiles * exp_tr, tok_tile)
        ys = experts(xs, gid, nvalid, nused, wslot, nxt, w_gate_e, w_up_e, w_down_e, l, tr=exp_tr)
        xa = combine(_tile_major(pos, comb_tile), xa, h2p, _tile_major(wsel, comb_tile), ga2,
                     wgs_bf, wus_bf, wds_bf, ys, L=L, B=B, n_rows=n_rows, tm=comb_tile)

    return xa.reshape(B, L, D)
```

```python
import functools

import jax
import jax.numpy as jnp
import numpy as np
from jax import lax
from jax.experimental import pallas as pl
from jax.experimental.pallas import tpu as pltpu

F32 = jnp.float32
BF16 = jnp.bfloat16
I32 = jnp.int32

HEAD_DIM = 128
GRID_W = 64
ROPE_THETA = 10000.0
NORM_EPS = 1e-6
GQA_GROUP = 4
N_FOURIER_GROUPS = 4
CONV_WIDTH = 31
CONV_PAD = CONV_WIDTH // 2
CONV_HALO = 16
SGU_CHUNK = 128
N_MOD = 6
TOP_K = 8
EXPERTS_PER_GROUP = 8
TOPK_GROUPS = 4
ROUTED_SCALE = 2.5
COND_ROWS = 8
ROW_GROUP = 8
LANES = 128
CONV_ROW_BLOCK = 64

VMEM_LIMIT = 56 * 1024 * 1024


def _cp(sem, vmem=VMEM_LIMIT, **kw):
    return pltpu.CompilerParams(dimension_semantics=sem, vmem_limit_bytes=vmem, **kw)


def _pick(n, pref, mult=8):
    if n <= pref:
        return n
    for t in range(pref - pref % mult, 0, -mult):
        if n % t == 0:
            return t
    raise ValueError(f"no tile for {n} <= {pref}")


def _rms(t, eps=NORM_EPS):
    return t * lax.rsqrt(jnp.mean(t * t, axis=-1, keepdims=True) + eps)


_HI16 = -65536


def _pack_halves(v):
    half = v.shape[1] // 2
    lo = lax.bitcast_convert_type(v[:, :half].astype(BF16).astype(F32), I32)
    hi = lax.bitcast_convert_type(v[:, half:].astype(BF16).astype(F32), I32)
    return lax.shift_right_logical(lo, 16) | (hi & _HI16)


def _unpack_halves(p):
    lo = lax.bitcast_convert_type(lax.shift_left(p, 16), F32)
    hi = lax.bitcast_convert_type(p & _HI16, F32)
    return lo, hi


def _cast_kernel(x_ref, o_ref):
    o_ref[...] = x_ref[...].astype(o_ref.dtype)


def cast_cols(w, l, col0, ncols, cb):
    _, R, _ = w.shape
    tr = _pick(R, 512)
    return pl.pallas_call(
        _cast_kernel,
        grid=(R // tr, ncols // cb),
        in_specs=[pl.BlockSpec((None, tr, cb), lambda i, j: (l, i, col0 // cb + j))],
        out_specs=pl.BlockSpec((tr, cb), lambda i, j: (i, j)),
        out_shape=jax.ShapeDtypeStruct((R, ncols), BF16),
        compiler_params=_cp(("parallel", "parallel")),
        name="cast_bf16",
    )(w)


def _modulation_kernel(c_ref, w_ref, b_ref, o_ref):
    s = jax.nn.silu(c_ref[...]).astype(BF16)
    o_ref[...] = jnp.dot(s, w_ref[...].astype(BF16), preferred_element_type=F32) + b_ref[...]


def modulation(cond, w_ada, b_ada):
    depth, D, N = w_ada.shape
    tn = _pick(N, 512, 128)
    return pl.pallas_call(
        _modulation_kernel,
        grid=(depth, N // tn),
        in_specs=[
            pl.BlockSpec((COND_ROWS, D), lambda l, j: (0, 0)),
            pl.BlockSpec((None, D, tn), lambda l, j: (l, 0, j)),
            pl.BlockSpec((None, 1, tn), lambda l, j: (l, 0, j)),
        ],
        out_specs=pl.BlockSpec((None, COND_ROWS, tn), lambda l, j: (l, 0, j)),
        out_shape=jax.ShapeDtypeStruct((depth, COND_ROWS, N), F32),
        compiler_params=_cp(("parallel", "parallel")),
        name="modulation",
    )(cond, w_ada, b_ada.reshape(depth, 1, N))


def _cond_row(i, tm, L, B):
    return jnp.minimum((i * tm) // L, B)


def _split_specs(arr, block, n_lat_tiles, col_of):
    if isinstance(arr, tuple):
        return ([pl.BlockSpec(block, lambda *ids: (jnp.minimum(ids[-1], n_lat_tiles - 1), col_of(ids))),
                 pl.BlockSpec(block, lambda *ids: (jnp.maximum(ids[-1] - n_lat_tiles, 0), col_of(ids)))],
                list(arr))
    return [pl.BlockSpec(block, lambda *ids: (ids[-1], col_of(ids)))], [arr]


def _split_read(refs, i, n_lat_tiles):
    if len(refs) == 2:
        return jnp.where(i < n_lat_tiles, refs[0][...], refs[1][...])
    return refs[0][...]


def _norm_mod_kernel(*refs, tm, L, B, n_x, n_lat_tiles):
    x_refs = refs[:n_x]
    g_ref, sh_ref, sc_ref, o_ref = refs[n_x:]
    i = pl.program_id(0)
    r = _cond_row(i, tm, L, B)
    xn = _rms(_split_read(x_refs, i, n_lat_tiles)) * g_ref[...]
    o_ref[...] = (xn * (1.0 + sc_ref[pl.ds(r, 1), :]) + sh_ref[pl.ds(r, 1), :]).astype(o_ref.dtype)


def norm_mod(x, g, sh, sc, *, L, B, T):
    D = g.shape[0]
    tm = _pick(L, 512)
    n_lat_tiles = B * L // tm
    x_specs, x_args = _split_specs(x, (tm, D), n_lat_tiles, lambda ids: 0)
    return pl.pallas_call(
        functools.partial(_norm_mod_kernel, tm=tm, L=L, B=B, n_x=len(x_args), n_lat_tiles=n_lat_tiles),
        grid=(T // tm,),
        in_specs=x_specs + [
            pl.BlockSpec((1, D), lambda i: (0, 0)),
            pl.BlockSpec((COND_ROWS, D), lambda i: (0, 0)),
            pl.BlockSpec((COND_ROWS, D), lambda i: (0, 0)),
        ],
        out_specs=pl.BlockSpec((tm, D), lambda i: (i, 0)),
        out_shape=jax.ShapeDtypeStruct((T, D), BF16),
        compiler_params=_cp(("parallel",)),
        name="norm_mod",
    )(*x_args, g.reshape(1, D), sh, sc)


def _mm_kernel(a_ref, w_ref, o_ref, w_bf):
    @pl.when(pl.program_id(1) == 0)
    def _():
        w_bf[...] = w_ref[...].astype(BF16)

    o_ref[...] = jnp.dot(a_ref[...], w_bf[...], preferred_element_type=F32).astype(o_ref.dtype)


def matmul(a, w, l, col0, ncols, n_rows, tn, out_dtype=BF16):
    _, K = a.shape
    tm = _pick(n_rows, 1024)
    return pl.pallas_call(
        _mm_kernel,
        grid=(ncols // tn, n_rows // tm),
        in_specs=[
            pl.BlockSpec((tm, K), lambda j, i: (i, 0)),
            pl.BlockSpec((None, K, tn), lambda j, i: (l, 0, col0 // tn + j)),
        ],
        out_specs=pl.BlockSpec((tm, tn), lambda j, i: (i, j)),
        out_shape=jax.ShapeDtypeStruct((n_rows, ncols), out_dtype),
        scratch_shapes=[pltpu.VMEM((K, tn), BF16)],
        compiler_params=_cp(("arbitrary", "arbitrary")),
        name="in_proj",
    )(a, w)


def _rope_tables(L, tm):
    rows = L // GRID_W
    row = np.repeat(np.arange(rows), GRID_W).astype(np.float32)
    col = np.tile(np.arange(GRID_W), rows).astype(np.float32)
    axis_dim = HEAD_DIM // 2
    inv = (np.float32(ROPE_THETA) ** (-np.arange(0, axis_dim, 2, dtype=np.float32) / axis_dim)).astype(np.float32)
    ang = np.concatenate([row[:, None] * inv, col[:, None] * inv], axis=-1).astype(np.float64)
    cos = np.repeat(np.cos(ang), 2, axis=-1)
    sin = np.repeat(np.sin(ang), 2, axis=-1)
    sign = np.tile(np.array([-1.0, 1.0]), HEAD_DIM // 2)
    cos = np.concatenate([cos, np.ones((tm, HEAD_DIM))], axis=0)
    sin = np.concatenate([sin * sign, np.zeros((tm, HEAD_DIM))], axis=0)
    return jnp.asarray(cos, F32), jnp.asarray(sin, F32)


def _qk_kernel(q_ref, k_ref, cos_ref, sin_ref, gq_ref, gk_ref, qo_ref, ko_ref, *, n_q, n_kv):
    cos = cos_ref[...]
    sin = sin_ref[...]
    even = (lax.broadcasted_iota(I32, cos.shape, 1) & 1) == 0

    def norm_rope(t, g):
        n = _rms(t.astype(F32)) * g
        partner = jnp.where(even, pltpu.roll(n, HEAD_DIM - 1, 1), pltpu.roll(n, 1, 1))
        return n * cos + partner * sin

    q_scale = HEAD_DIM ** -0.5
    for h in range(n_q):
        sl = slice(h * HEAD_DIM, (h + 1) * HEAD_DIM)
        qo_ref[:, sl] = (norm_rope(q_ref[:, sl], gq_ref[...]) * q_scale).astype(qo_ref.dtype)
    for h in range(n_kv):
        sl = slice(h * HEAD_DIM, (h + 1) * HEAD_DIM)
        ko_ref[:, sl] = norm_rope(k_ref[:, sl], gk_ref[...]).astype(ko_ref.dtype)


def qk_norm_rope(z_qkv, g_q, g_k, *, L, B, d_attn, kv_w):
    T = z_qkv.shape[0]
    tm = _pick(L, 256)
    cos, sin = _rope_tables(L, tm)
    n_lat = B * L // tm
    per_seq = L // tm

    def tab_idx(i):
        return (jnp.where(i < n_lat, i % per_seq, per_seq), 0)

    return pl.pallas_call(
        functools.partial(_qk_kernel, n_q=d_attn // HEAD_DIM, n_kv=kv_w // HEAD_DIM),
        grid=(T // tm,),
        in_specs=[
            pl.BlockSpec((tm, d_attn), lambda i: (i, 0)),
            pl.BlockSpec((tm, kv_w), lambda i: (i, d_attn // kv_w)),
            pl.BlockSpec((tm, HEAD_DIM), tab_idx),
            pl.BlockSpec((tm, HEAD_DIM), tab_idx),
            pl.BlockSpec((1, HEAD_DIM), lambda i: (0, 0)),
            pl.BlockSpec((1, HEAD_DIM), lambda i: (0, 0)),
        ],
        out_specs=[
            pl.BlockSpec((tm, d_attn), lambda i: (i, 0)),
            pl.BlockSpec((tm, kv_w), lambda i: (i, 0)),
        ],
        out_shape=[
            jax.ShapeDtypeStruct((T, d_attn), BF16),
            jax.ShapeDtypeStruct((T, kv_w), BF16),
        ],
        compiler_params=_cp(("parallel",)),
        name="qk_norm_rope",
    )(z_qkv, z_qkv, cos, sin, g_q.reshape(1, HEAD_DIM), g_k.reshape(1, HEAD_DIM))


def _attn_kernel(*refs, n_seg):
    q_ref = refs[0]
    k_refs = refs[1:1 + n_seg]
    v_refs = refs[1 + n_seg:1 + 2 * n_seg]
    o_ref = refs[-1]
    v_ext = [jnp.concatenate([v[...], jnp.ones(v.shape, BF16)], axis=1) for v in v_refs]
    for h in range(GQA_GROUP):
        sl = slice(h * HEAD_DIM, (h + 1) * HEAD_DIM)
        q = q_ref[:, sl]
        s = [lax.dot_general(q, k[...], (((1,), (1,)), ((), ())), preferred_element_type=F32)
             for k in k_refs]
        m = functools.reduce(jnp.maximum, [jnp.max(x, axis=-1, keepdims=True) for x in s])
        p = [jnp.exp((x - m).astype(BF16)) for x in s]
        oe = functools.reduce(jnp.add, [jnp.dot(x, v, preferred_element_type=F32) for x, v in zip(p, v_ext)])
        o_ref[:, sl] = (oe[:, :HEAD_DIM] / oe[:, HEAD_DIM:HEAD_DIM + 1]).astype(o_ref.dtype)


def attention(qn, kn, z_qkv, *, q_row0, Lq, segs, B, d_attn, kv_w):
    n_kv = kv_w // HEAD_DIM
    gw = GQA_GROUP * HEAD_DIM
    tq = _pick(Lq, 512)
    nq = Lq // tq
    v_col0 = (d_attn + kv_w) // HEAD_DIM

    def q_idx(b, g, qi):
        return (q_row0 // tq + b * nq + qi, g)

    in_specs = [pl.BlockSpec((tq, gw), q_idx)]
    args = [qn]
    for row0, ln in segs:
        in_specs.append(pl.BlockSpec((ln, HEAD_DIM), lambda b, g, qi, row0=row0, ln=ln: (row0 // ln + b, g)))
        args.append(kn)
    for row0, ln in segs:
        in_specs.append(
            pl.BlockSpec((ln, HEAD_DIM), lambda b, g, qi, row0=row0, ln=ln: (row0 // ln + b, v_col0 + g)))
        args.append(z_qkv)
    return pl.pallas_call(
        functools.partial(_attn_kernel, n_seg=len(segs)),
        grid=(B, n_kv, nq),
        in_specs=in_specs,
        out_specs=pl.BlockSpec((tq, gw), lambda b, g, qi: (b * nq + qi, g)),
        out_shape=jax.ShapeDtypeStruct((B * Lq, d_attn), BF16),
        compiler_params=_cp(("parallel", "parallel", "parallel")),
        name="attention",
    )(*args)


def _dft_tables(n, dtype):
    k = np.arange(n, dtype=np.int64)
    ang = ((k[:, None] * k[None, :]) % n).astype(np.float64) * (2.0 * np.pi / n)
    return jnp.asarray(np.cos(ang), dtype), jnp.asarray(np.sin(ang), dtype)


def _fourier_kernel(t_ref, cl_ref, sl_ref, cc_ref, sc_ref, w_ref, o_ref, *, scale):
    w = w_ref[...]
    hi = lax.Precision.HIGHEST
    mc = jnp.dot(cc_ref[...], w, precision=hi, preferred_element_type=F32).astype(BF16)
    ms = jnp.dot(sc_ref[...], w, precision=hi, preferred_element_type=F32).astype(BF16)
    t = t_ref[...]
    bc = jnp.dot(t, mc, preferred_element_type=F32).astype(BF16)
    bs = jnp.dot(t, ms, preferred_element_type=F32).astype(BF16)
    y = (jnp.dot(cl_ref[...], bc, preferred_element_type=F32)
         - jnp.dot(sl_ref[...], bs, preferred_element_type=F32))
    o_ref[...] = (y * scale).astype(o_ref.dtype)


def fourier(z_f, w_f, *, row0, Ls, B):
    ng, fg, _ = w_f.shape
    cl, sl = _dft_tables(Ls, BF16)
    cc, sc = _dft_tables(fg, F32)
    scale = 1.0 / float(Ls * fg) ** 0.5
    return pl.pallas_call(
        functools.partial(_fourier_kernel, scale=scale),
        grid=(B, ng),
        in_specs=[
            pl.BlockSpec((Ls, fg), lambda b, g: (row0 // Ls + b, g)),
            pl.BlockSpec((Ls, Ls), lambda b, g: (0, 0)),
            pl.BlockSpec((Ls, Ls), lambda b, g: (0, 0)),
            pl.BlockSpec((fg, fg), lambda b, g: (0, 0)),
            pl.BlockSpec((fg, fg), lambda b, g: (0, 0)),
            pl.BlockSpec((None, fg, fg), lambda b, g: (g, 0, 0)),
        ],
        out_specs=pl.BlockSpec((Ls, fg), lambda b, g: (b, g)),
        out_shape=jax.ShapeDtypeStruct((B * Ls, ng * fg), BF16),
        compiler_params=_cp(("parallel", "parallel")),
        name="fourier",
    )(z_f, cl, sl, cc, sc, w_f)


def _conv_kernel(cur_ref, prev_ref, next_ref, wdw_ref, bdw_ref, gln_ref, bln_ref, wpw_ref, bpw_ref,
                 o_ref, gbuf, shifted, cbuf, *, tr, dc, n_lat, per_lat, per_ctx):
    i = pl.program_id(0)
    j = jnp.where(i < n_lat, i % per_lat, (i - n_lat) % per_ctx)
    per = jnp.where(i < n_lat, per_lat, per_ctx)
    first = j == 0
    last = j == per - 1

    def glu(zb):
        return zb[:, :dc].astype(F32) * jax.nn.sigmoid(zb[:, dc:].astype(F32))

    gbuf[0:CONV_HALO, :] = jnp.where(first, 0.0, glu(prev_ref[...]))
    gbuf[CONV_HALO:CONV_HALO + tr, :] = glu(cur_ref[...])
    gbuf[CONV_HALO + tr:, :] = jnp.where(last, 0.0, glu(next_ref[...]))
    n_sh = tr + CONV_HALO + ROW_GROUP
    for s in range(1, ROW_GROUP):
        shifted[s - 1, :, :] = gbuf[s:s + n_sh, :]
    first_off = CONV_HALO - CONV_PAD
    rb = min(tr, CONV_ROW_BLOCK)
    for c in range(dc // LANES):
        cols = slice(c * LANES, (c + 1) * LANES)
        for r0 in range(0, tr, rb):
            a = jnp.broadcast_to(bdw_ref[:, cols], (rb, LANES))
            for k in range(CONV_WIDTH):
                off = first_off + k + r0
                s = off % ROW_GROUP
                src = gbuf[off:off + rb, cols] if s == 0 else shifted[s - 1, off - s:off - s + rb, cols]
                a = a + src * wdw_ref[k:k + 1, cols]
            cbuf[r0:r0 + rb, cols] = a
    acc = cbuf[...]
    mu = jnp.mean(acc, axis=-1, keepdims=True)
    cen = acc - mu
    var = jnp.mean(cen * cen, axis=-1, keepdims=True)
    hn = cen * lax.rsqrt(var + NORM_EPS) * gln_ref[...] + bln_ref[...]
    act = jax.nn.silu(hn).astype(BF16)
    o_ref[...] = (jnp.dot(act, wpw_ref[...], preferred_element_type=F32) + bpw_ref[...]).astype(o_ref.dtype)


def conv_module(z_c, w_dw, b_dw, g_ln, b_ln, w_pw_bf, b_pw, *, L, Lc, B, n_rows):
    dc = w_pw_bf.shape[0]
    tr = _pick(Lc, 256, CONV_HALO)
    hpt = tr // CONV_HALO
    n_halo = z_c.shape[0] // CONV_HALO
    w_pad = jnp.concatenate([w_dw, jnp.zeros((32 - CONV_WIDTH, dc), F32)], axis=0)
    row = lambda v: v.reshape(1, dc)
    return pl.pallas_call(
        functools.partial(_conv_kernel, tr=tr, dc=dc, n_lat=B * L // tr, per_lat=L // tr, per_ctx=Lc // tr),
        grid=(n_rows // tr,),
        in_specs=[
            pl.BlockSpec((tr, 2 * dc), lambda i: (i, 0)),
            pl.BlockSpec((CONV_HALO, 2 * dc), lambda i: (jnp.maximum(i * hpt - 1, 0), 0)),
            pl.BlockSpec((CONV_HALO, 2 * dc), lambda i: (jnp.minimum((i + 1) * hpt, n_halo - 1), 0)),
            pl.BlockSpec((32, dc), lambda i: (0, 0)),
            pl.BlockSpec((1, dc), lambda i: (0, 0)),
            pl.BlockSpec((1, dc), lambda i: (0, 0)),
            pl.BlockSpec((1, dc), lambda i: (0, 0)),
            pl.BlockSpec((dc, dc), lambda i: (0, 0)),
            pl.BlockSpec((1, dc), lambda i: (0, 0)),
        ],
        out_specs=pl.BlockSpec((tr, dc), lambda i: (i, 0)),
        out_shape=jax.ShapeDtypeStruct((n_rows, dc), BF16),
        scratch_shapes=[
            pltpu.VMEM((tr + 2 * CONV_HALO, dc), F32),
            pltpu.VMEM((ROW_GROUP - 1, tr + CONV_HALO + ROW_GROUP, dc), F32),
            pltpu.VMEM((tr, dc), F32),
        ],
        compiler_params=_cp(("parallel",)),
        name="conv_module",
    )(z_c, z_c, z_c, w_pad, row(b_dw), row(g_ln), row(b_ln), w_pw_bf, row(b_pw))


def _sgu_kernel(z_ref, g_ref, ws_ref, bs_ref, o_ref, *, tm, ds, n_heads):
    for h in range(n_heads):
        sl = slice(h * HEAD_DIM, (h + 1) * HEAD_DIM)
        u = jax.nn.gelu(z_ref[:, sl].astype(F32))
        v = jax.nn.gelu(z_ref[:, ds + h * HEAD_DIM:ds + (h + 1) * HEAD_DIM].astype(F32))
        vn = (_rms(v) * g_ref[:, sl]).astype(BF16)
        w = ws_ref[h].astype(BF16)
        b = bs_ref[:, h:h + 1]
        for c in range(tm // SGU_CHUNK):
            rows = slice(c * SGU_CHUNK, (c + 1) * SGU_CHUNK)
            s = jnp.dot(w, vn[rows, :], preferred_element_type=F32) + b
            o_ref[rows, sl] = (u[rows, :] * s).astype(o_ref.dtype)


def sgu_mixer(z_s, g_sgu, w_s, b_s, *, n_rows):
    n_heads = w_s.shape[0]
    ds = n_heads * HEAD_DIM
    tm = _pick(n_rows, 256, SGU_CHUNK)
    return pl.pallas_call(
        functools.partial(_sgu_kernel, tm=tm, ds=ds, n_heads=n_heads),
        grid=(n_rows // tm,),
        in_specs=[
            pl.BlockSpec((tm, 2 * ds), lambda i: (i, 0)),
            pl.BlockSpec((1, ds), lambda i: (0, 0)),
            pl.BlockSpec((n_heads, SGU_CHUNK, SGU_CHUNK), lambda i: (0, 0, 0)),
            pl.BlockSpec((SGU_CHUNK, n_heads), lambda i: (0, 0)),
        ],
        out_specs=pl.BlockSpec((tm, ds), lambda i: (i, 0)),
        out_shape=jax.ShapeDtypeStruct((n_rows, ds), BF16),
        compiler_params=_cp(("parallel",)),
        name="sgu_mixer",
    )(z_s, g_sgu.reshape(1, ds), w_s, b_s.T)


def _out_proj_kernel(*refs, tm, L, B, widths, split, n_lat_tiles, n_x):
    n_y = sum(2 if sp else 1 for sp in split)
    y_refs = refs[:n_y]
    w_ref = refs[n_y]
    x_refs = refs[n_y + 1:n_y + 1 + n_x]
    ga_ref, o_ref = refs[n_y + 1 + n_x:]
    i = pl.program_id(1)
    r = _cond_row(i, tm, L, B)
    acc = None
    k0 = 0
    p = 0
    for wd, sp in zip(widths, split):
        if sp:
            y = jnp.where(i < n_lat_tiles, y_refs[p][...], y_refs[p + 1][...])
            p += 2
        else:
            y = y_refs[p][...]
            p += 1
        part = jnp.dot(y, w_ref[k0:k0 + wd, :], preferred_element_type=F32)
        acc = part if acc is None else acc + part
        k0 += wd
    o_ref[...] = _split_read(x_refs, i, n_lat_tiles) + ga_ref[pl.ds(r, 1), :] * acc


def out_proj(ys, w_out_bf, x, ga, *, L, B, n_rows):
    D = w_out_bf.shape[1]
    tm = _pick(L, 512)
    tn = _pick(D, 1024, 128)
    n_lat_tiles = B * L // tm
    split = tuple(isinstance(y, tuple) for y in ys)
    widths = tuple((y[0] if sp else y).shape[1] for y, sp in zip(ys, split))
    in_specs, args = [], []
    for y, wd, sp in zip(ys, widths, split):
        if sp:
            in_specs.append(pl.BlockSpec((tm, wd), lambda j, i: (jnp.minimum(i, n_lat_tiles - 1), 0)))
            in_specs.append(pl.BlockSpec((tm, wd), lambda j, i: (jnp.maximum(i - n_lat_tiles, 0), 0)))
            args += [y[0], y[1]]
        else:
            in_specs.append(pl.BlockSpec((tm, wd), lambda j, i: (i, 0)))
            args.append(y)
    x_specs, x_args = _split_specs(x, (tm, tn), n_lat_tiles, lambda ids: ids[0])
    in_specs += [pl.BlockSpec((sum(widths), tn), lambda j, i: (0, j))] + x_specs
    in_specs += [pl.BlockSpec((COND_ROWS, tn), lambda j, i: (0, j))]
    return pl.pallas_call(
        functools.partial(_out_proj_kernel, tm=tm, L=L, B=B, widths=widths, split=split,
                          n_lat_tiles=n_lat_tiles, n_x=len(x_args)),
        grid=(D // tn, n_rows // tm),
        in_specs=in_specs,
        out_specs=pl.BlockSpec((tm, tn), lambda j, i: (i, j)),
        out_shape=jax.ShapeDtypeStruct((n_rows, D), F32),
        compiler_params=_cp(("parallel", "parallel")),
        name="out_proj",
    )(*args, w_out_bf, *x_args, ga)


def _router_kernel(x_ref, g_ref, sh_ref, sc_ref, wr_ref, br_ref,
                   h2_ref, eidx_ref, wsel_ref, rank_ref, cnt_ref, *, tm, L, B, n_exp):
    i = pl.program_id(0)
    r = _cond_row(i, tm, L, B)
    xn = _rms(x_ref[...]) * g_ref[...]
    h2 = xn * (1.0 + sc_ref[pl.ds(r, 1), :]) + sh_ref[pl.ds(r, 1), :]
    h2_ref[...] = _pack_halves(h2).reshape(h2_ref.shape)
    def split(v):
        hi_part = v.astype(BF16)
        return hi_part, (v - hi_part.astype(F32)).astype(BF16)

    def nt_dot(a, b):
        return lax.dot_general(a, b, (((1,), (1,)), ((), ())), preferred_element_type=F32)

    w_hi, w_lo = split(wr_ref[...])
    h_hi, h_lo = split(h2)
    logits = nt_dot(w_hi, h_hi) + (nt_dot(w_lo, h_hi) + nt_dot(w_hi, h_lo))
    s = jax.nn.sigmoid(logits)
    sel = s + br_ref[...]
    shape = (n_exp, tm)
    e_io = lax.broadcasted_iota(I32, shape, 0)
    e_f = e_io.astype(F32)
    k_io = e_io & (EXPERTS_PER_GROUP - 1)
    g_io = e_io >> 3
    n_groups = n_exp // EXPERTS_PER_GROUP
    neg = -jnp.inf

    def partner(v, d):
        up = pltpu.roll(v, n_exp - d, 0)
        dn = pltpu.roll(v, d, 0)
        return jnp.where((k_io & d) == 0, up, dn)

    def group_all(v, op):
        d = 1
        while d < EXPERTS_PER_GROUP:
            v = op(v, partner(v, d))
            d *= 2
        return v

    kf = k_io.astype(F32)
    m1 = group_all(sel, jnp.maximum)
    i1 = group_all(jnp.where(sel == m1, kf, float(EXPERTS_PER_GROUP)), jnp.minimum)
    m2 = group_all(jnp.where(kf == i1, neg, sel), jnp.maximum)
    gs = m1 + m2
    beaten = jnp.zeros(shape, I32)
    for d in range(1, n_groups):
        other = pltpu.roll(gs, n_exp - EXPERTS_PER_GROUP * d, 0)
        og = (g_io + d) & (n_groups - 1)
        beats = (other > gs) | ((other == gs) & (og < g_io))
        beaten = beaten + beats.astype(I32)
    val = jnp.where(beaten < TOPK_GROUPS, sel, neg)

    assign = jnp.zeros(shape, F32)
    w_rows = []
    idx_rows = []
    for j in range(TOP_K):
        m = jnp.max(val, axis=0, keepdims=True)
        idx = jnp.min(jnp.where(val == m, e_f, float(n_exp)), axis=0, keepdims=True)
        hit = e_f == idx
        w_rows.append(jnp.sum(jnp.where(hit, s, 0.0), axis=0, keepdims=True))
        val = jnp.where(hit, neg, val)
        assign = assign + hit.astype(F32)
        idx_rows.append(idx)
        eidx_ref[j:j + 1, :] = idx.astype(I32)
    w_sum = functools.reduce(jnp.add, w_rows)
    for j in range(TOP_K):
        wsel_ref[j:j + 1, :] = w_rows[j] / w_sum * ROUTED_SCALE

    @pl.when(i == 0)
    def _():
        cnt_ref[...] = jnp.zeros_like(cnt_ref)

    earlier = (lax.broadcasted_iota(I32, (tm, tm), 0) < lax.broadcasted_iota(I32, (tm, tm), 1)).astype(BF16)
    before = jnp.dot(assign.astype(BF16), earlier, preferred_element_type=F32) + cnt_ref[...]
    for j in range(TOP_K):
        hit = e_f == idx_rows[j]
        rank_ref[j:j + 1, :] = jnp.sum(jnp.where(hit, before, 0.0), axis=0, keepdims=True).astype(I32)
    cnt_ref[...] = cnt_ref[...] + jnp.sum(assign, axis=1, keepdims=True)


def router(x, g, sh, sc, w_router, b_router, *, L, B, n_rows):
    D = x.shape[1]
    n_exp = w_router.shape[1]
    tm = _pick(L, 512, 128)
    assert n_rows % tm == 0
    return pl.pallas_call(
        functools.partial(_router_kernel, tm=tm, L=L, B=B, n_exp=n_exp),
        grid=(n_rows // tm,),
        in_specs=[
            pl.BlockSpec((tm, D), lambda i: (i, 0)),
            pl.BlockSpec((1, D), lambda i: (0, 0)),
            pl.BlockSpec((COND_ROWS, D), lambda i: (0, 0)),
            pl.BlockSpec((COND_ROWS, D), lambda i: (0, 0)),
            pl.BlockSpec((n_exp, D), lambda i: (0, 0)),
            pl.BlockSpec((n_exp, 1), lambda i: (0, 0)),
        ],
        out_specs=[
            pl.BlockSpec((tm // ROW_GROUP, ROW_GROUP, D // 2), lambda i: (i, 0, 0)),
            pl.BlockSpec((TOP_K, tm), lambda i: (0, i)),
            pl.BlockSpec((TOP_K, tm), lambda i: (0, i)),
            pl.BlockSpec((TOP_K, tm), lambda i: (0, i)),
            pl.BlockSpec((n_exp, 1), lambda i: (0, 0)),
        ],
        out_shape=[
            jax.ShapeDtypeStruct((n_rows // ROW_GROUP, ROW_GROUP, D // 2), I32),
            jax.ShapeDtypeStruct((TOP_K, n_rows), I32),
            jax.ShapeDtypeStruct((TOP_K, n_rows), F32),
            jax.ShapeDtypeStruct((TOP_K, n_rows), I32),
            jax.ShapeDtypeStruct((n_exp, 1), F32),
        ],
        compiler_params=_cp(("arbitrary",)),
        name="router",
    )(x, g.reshape(1, D), sh, sc, w_router.T, b_router.reshape(n_exp, 1))


def _vrow(ref, group, sub):
    return ref.at[group, pl.ds(sub, 1)]


def _hrow(ref, row):
    return ref.at[pl.ds(row, 1)]


def _dispatch_kernel(pos_ref, h_ref, xs_ref, sem, *, tm):
    def issue(g, carry):
        base = g * ROW_GROUP
        for u in range(ROW_GROUP):
            for j in range(TOP_K):
                pltpu.make_async_copy(_vrow(h_ref, g, u), _hrow(xs_ref, pos_ref[j * tm + base + u]),
                                      sem).start(priority=j % 2)
        return carry

    lax.fori_loop(0, tm // ROW_GROUP, issue, 0)
    for j in range(TOP_K):
        pltpu.make_async_copy(xs_ref.at[pl.ds(0, tm)], xs_ref.at[pl.ds(0, tm)], sem).wait()


def dispatch(pos, h2p, n_slots, tm):
    G, _, W = h2p.shape
    T = G * ROW_GROUP
    return pl.pallas_call(
        functools.partial(_dispatch_kernel, tm=tm),
        grid=(T // tm,),
        in_specs=[
            pl.BlockSpec((TOP_K * tm,), lambda i: (i,), memory_space=pltpu.SMEM),
            pl.BlockSpec((tm // ROW_GROUP, ROW_GROUP, W), lambda i: (i, 0, 0)),
        ],
        out_specs=pl.BlockSpec(memory_space=pl.ANY),
        out_shape=jax.ShapeDtypeStruct((n_slots, W), I32),
        scratch_shapes=[pltpu.SemaphoreType.DMA(())],
        compiler_params=_cp(("arbitrary",), has_side_effects=True),
        name="dispatch",
    )(pos, h2p)


def _experts_kernel(gid_ref, nvalid_ref, nused_ref, wslot_ref, nxt_ref, xs_ref, wg_hbm, wu_hbm, wd_hbm, ys_ref,
                    wg_st, wu_st, wd_st, wg_bf, wu_bf, wd_bf, obuf, wsem, osem, *, tr, l, n_tiles):
    i = pl.program_id(0)
    n_used = nused_ref[0]
    first = jnp.logical_or(i == 0, gid_ref[i] != gid_ref[jnp.maximum(i - 1, 0)])
    n_lane_tiles = obuf.shape[2] // LANES

    def weight_copies(e, slot):
        return [pltpu.make_async_copy(src.at[l, e], dst.at[slot], wsem.at[slot])
                for src, dst in ((wg_hbm, wg_st), (wu_hbm, wu_st), (wd_hbm, wd_st))]

    def out_copies(tile, slot):
        return [pltpu.make_async_copy(obuf.at[slot, :, pl.ds(c * LANES, LANES)],
                                      ys_ref.at[pl.ds(tile * tr, tr), c], osem.at[slot])
                for c in range(n_lane_tiles)]

    @pl.when(i == 0)
    def _():
        for cp in weight_copies(gid_ref[0], wslot_ref[0]):
            cp.start(priority=1)

    @pl.when(first)
    def _():
        slot = wslot_ref[i]
        for cp in weight_copies(gid_ref[i], slot):
            cp.wait()
        wg_bf[...] = wg_st[slot].astype(BF16)
        wu_bf[...] = wu_st[slot].astype(BF16)
        wd_bf[...] = wd_st[slot].astype(BF16)

        @pl.when(nxt_ref[i] >= 0)
        def _():
            for cp in weight_copies(nxt_ref[i], 1 - slot):
                cp.start(priority=1)

    @pl.when(jnp.logical_and(i >= 2, i - 2 < n_used))
    def _():
        for cp in out_copies(i - 2, i % 2):
            cp.wait()

    @pl.when(i < n_used)
    def _():
        half = xs_ref.shape[1]
        valid = lax.broadcasted_iota(I32, (tr, 1), 0) < nvalid_ref[i]
        lo, hi = _unpack_halves(xs_ref[...])
        lo = jnp.where(valid, lo, 0.0).astype(BF16)
        hi = jnp.where(valid, hi, 0.0).astype(BF16)

        def proj(w_bf):
            return (jnp.dot(lo, w_bf[:half, :], preferred_element_type=F32)
                    + jnp.dot(hi, w_bf[half:, :], preferred_element_type=F32))

        hh = (jax.nn.silu(proj(wg_bf)) * proj(wu_bf)).astype(BF16)
        slot = i % 2
        obuf[slot] = _pack_halves(jnp.dot(hh, wd_bf[...], preferred_element_type=F32))
        for cp in out_copies(i, slot):
            cp.start()

    @pl.when(i == n_tiles - 1)
    def _():
        @pl.when(jnp.logical_and(i >= 1, i - 1 < n_used))
        def _():
            for cp in out_copies(i - 1, (i - 1) % 2):
                cp.wait()

        @pl.when(i < n_used)
        def _():
            for cp in out_copies(i, i % 2):
                cp.wait()


def experts(xs, gid, nvalid, nused, wslot, nxt, w_ge, w_ue, w_de, l, *, tr):
    P, W = xs.shape
    D = 2 * W
    F = w_ge.shape[3]
    n_tiles = P // tr

    def row_idx(i, gid, nvalid, nused, wslot, nxt):
        return (jnp.minimum(i, nused[0] - 1), 0)

    return pl.pallas_call(
        functools.partial(_experts_kernel, tr=tr, l=l, n_tiles=n_tiles),
        grid_spec=pltpu.PrefetchScalarGridSpec(
            num_scalar_prefetch=5,
            grid=(n_tiles,),
            in_specs=[
                pl.BlockSpec((tr, W), row_idx),
                pl.BlockSpec(memory_space=pl.ANY),
                pl.BlockSpec(memory_space=pl.ANY),
                pl.BlockSpec(memory_space=pl.ANY),
            ],
            out_specs=pl.BlockSpec(memory_space=pl.ANY),
            scratch_shapes=[
                pltpu.VMEM((2, D, F), F32), pltpu.VMEM((2, D, F), F32), pltpu.VMEM((2, F, D), F32),
                pltpu.VMEM((D, F), BF16), pltpu.VMEM((D, F), BF16), pltpu.VMEM((F, D), BF16),
                pltpu.VMEM((2, tr, W), I32),
                pltpu.SemaphoreType.DMA((2,)), pltpu.SemaphoreType.DMA((2,)),
            ],
        ),
        out_shape=jax.ShapeDtypeStruct((P, W // LANES, LANES), I32),
        compiler_params=_cp(("arbitrary",), has_side_effects=True),
        name="experts",
    )(gid, nvalid, nused, wslot, nxt, xs, w_ge, w_ue, w_de)


def _combine_kernel(pos_ref, posn_ref, w_ref, x_ref, h_ref, ga_ref, wgs_ref, wus_ref, wds_ref, ys_ref,
                    o_ref, gbuf2, acc, sems, *, tm, L, B):
    i = pl.program_id(0)
    r = _cond_row(i, tm, L, B)
    half = h_ref.shape[2]
    n_slab = gbuf2.shape[3]
    slot = i % 2
    gbuf = gbuf2.at[slot]
    sem = sems.at[slot]

    def issue_tile(p_ref, dst, dsem):
        def issue(t, carry):
            for j in range(TOP_K):
                pltpu.make_async_copy(ys_ref.at[p_ref[j * tm + t]], dst.at[j, t], dsem).start(priority=j % 2)
            return carry

        lax.fori_loop(0, tm, issue, 0, unroll=2)

    @pl.when(i == 0)
    def _():
        issue_tile(pos_ref, gbuf, sem)

    lo, hi = _unpack_halves(h_ref[...].reshape(tm, half))
    lo = lo.astype(BF16)
    hi = hi.astype(BF16)

    def proj(w_bf):
        return (jnp.dot(lo, w_bf[:half, :], preferred_element_type=F32)
                + jnp.dot(hi, w_bf[half:, :], preferred_element_type=F32))

    hh = (jax.nn.silu(proj(wgs_ref)) * proj(wus_ref)).astype(BF16)
    shared = jnp.dot(hh, wds_ref[...], preferred_element_type=F32)

    for j in range(TOP_K):
        pltpu.make_async_copy(ys_ref.at[pl.ds(0, tm)], gbuf.at[j], sem).wait()

    def token_loop(prefetch):
        nbuf = gbuf2.at[1 - slot]
        nsem = sems.at[1 - slot]

        def token(t, carry):
            a_lo = jnp.zeros((n_slab, LANES), F32)
            a_hi = jnp.zeros((n_slab, LANES), F32)
            for j in range(TOP_K):
                y_lo, y_hi = _unpack_halves(gbuf[j, t])
                w = w_ref[j * tm + t]
                a_lo = a_lo + w * y_lo
                a_hi = a_hi + w * y_hi
            acc[t, 0:n_slab, :] = a_lo
            acc[t, n_slab:2 * n_slab, :] = a_hi
            if prefetch:
                for j in range(TOP_K):
                    pltpu.make_async_copy(ys_ref.at[posn_ref[j * tm + t]], nbuf.at[j, t],
                                          nsem).start(priority=j % 2)
            return carry

        lax.fori_loop(0, tm, token, 0, unroll=2)

    not_last = i + 1 < pl.num_programs(0)

    @pl.when(not_last)
    def _():
        token_loop(True)

    @pl.when(jnp.logical_not(not_last))
    def _():
        token_loop(False)

    ga = ga_ref[pl.ds(r, 1), :]
    for c in range(2 * n_slab):
        cols = slice(c * LANES, (c + 1) * LANES)
        o_ref[:, cols] = x_ref[:, cols] + ga[:, cols] * (shared[:, cols] + acc[:, c, :])


def combine(pos, x, h2p, wsel_flat, ga, wgs_bf, wus_bf, wds_bf, ys, *, L, B, n_rows, tm):
    D = x.shape[1]
    W = D // 2
    F = wgs_bf.shape[1]
    n_slab = W // LANES
    n_steps = n_rows // tm
    return pl.pallas_call(
        functools.partial(_combine_kernel, tm=tm, L=L, B=B),
        grid=(n_steps,),
        in_specs=[
            pl.BlockSpec((TOP_K * tm,), lambda i: (i,), memory_space=pltpu.SMEM),
            pl.BlockSpec((TOP_K * tm,), lambda i: (jnp.minimum(i + 1, n_steps - 1),), memory_space=pltpu.SMEM),
            pl.BlockSpec((TOP_K * tm,), lambda i: (i,), memory_space=pltpu.SMEM),
            pl.BlockSpec((tm, D), lambda i: (i, 0)),
            pl.BlockSpec((tm // ROW_GROUP, ROW_GROUP, W), lambda i: (i, 0, 0)),
            pl.BlockSpec((COND_ROWS, D), lambda i: (0, 0)),
            pl.BlockSpec((D, F), lambda i: (0, 0)),
            pl.BlockSpec((D, F), lambda i: (0, 0)),
            pl.BlockSpec((F, D), lambda i: (0, 0)),
            pl.BlockSpec(memory_space=pl.ANY),
        ],
        out_specs=pl.BlockSpec((tm, D), lambda i: (i, 0)),
        out_shape=jax.ShapeDtypeStruct((n_rows, D), F32),
        scratch_shapes=[pltpu.VMEM((2, TOP_K, tm, n_slab, LANES), I32),
                        pltpu.VMEM((tm, 2 * n_slab, LANES), F32),
                        pltpu.SemaphoreType.DMA((2,))],
        compiler_params=_cp(("arbitrary",)),
        name="combine",
    )(pos, pos, wsel_flat, x, h2p, ga, wgs_bf, wus_bf, wds_bf, ys)


def _tile_major(a, tile):
    k, T = a.shape
    return a.reshape(k, T // tile, tile).transpose(1, 0, 2).reshape(-1)


def _dispatch_plan(cnt, eidx, rank, *, tr, n_tiles):
    n_exp = cnt.shape[0]
    e_ids = jnp.arange(n_exp, dtype=I32)
    tiles_e = (cnt + tr - 1) // tr
    tile_end = jnp.cumsum(tiles_e)
    tile_start = tile_end - tiles_e

    def lookup(table, idx, keys=e_ids):
        return jnp.sum(jnp.where(idx[..., None] == keys, table, 0), axis=-1)

    pos = lookup(tile_start * tr, eidx) + rank
    n_used = tile_end[-1]
    ti = jnp.arange(n_tiles, dtype=I32)
    ti_c = jnp.minimum(ti, n_used - 1)
    gid = jnp.sum((ti_c[:, None] >= tile_end[None, :]).astype(I32), axis=1)
    left = lookup(cnt, gid) - (ti - lookup(tile_start, gid)) * tr
    nvalid = jnp.where(ti < n_used, jnp.clip(left, 0, tr), 0)
    changed = jnp.concatenate([jnp.ones((1,), I32), (gid[1:] != gid[:-1]).astype(I32)])
    wslot = (jnp.cumsum(changed) - 1) % 2
    after = lookup(tile_end, gid)
    nxt = jnp.where(after < n_used, lookup(gid, jnp.minimum(after, n_tiles - 1), ti), -1)
    as_i32 = lambda a: a.astype(I32)
    return (as_i32(pos), as_i32(gid), as_i32(nvalid), n_used.reshape(1).astype(I32), as_i32(wslot), as_i32(nxt))


def kernel(x, c, ctx, c_ctx, w_ada, b_ada, g_norm1, g_norm2, w_in, g_q, g_k, w_fourier, w_dw, b_dw, g_conv_ln, b_conv_ln, w_pw, b_pw, g_sgu, w_spatial, b_spatial, w_out, w_router, b_router, w_gate_e, w_up_e, w_down_e, w_gate_s, w_up_s, w_down_s):
    B, L, D = x.shape
    Lc = ctx.shape[1]
    depth = w_ada.shape[0]
    assert B + 1 <= COND_ROWS and L % Lc == 0 and L % GRID_W == 0
    d_fourier = w_fourier.shape[1] * w_fourier.shape[2]
    d_conv = w_pw.shape[1]
    d_sgu = g_sgu.shape[1]
    d_attn = D - d_fourier - d_conv - d_sgu
    kv_w = d_attn // GQA_GROUP
    v_end = d_attn + 2 * kv_w
    f_end = v_end + d_fourier
    c_end = f_end + 2 * d_conv
    d_in = c_end + 2 * d_sgu
    assert w_in.shape[2] == d_in
    n_exp = w_router.shape[2]
    n_lat = B * L
    T_all = n_lat + B * Lc
    cb = D // 8
    exp_tr = 256
    tok_tile = _pick(L, 256, 128)
    comb_tile = 128

    assert n_exp // EXPERTS_PER_GROUP == 8 and (B * Lc) % _pick(L, 512) == 0
    assert all(c % cb == 0 for c in (v_end, f_end, c_end, d_in))
    fh = w_gate_s.shape[2]

    cond = jnp.concatenate([c, c_ctx[None, :], jnp.zeros((COND_ROWS - B - 1, D), F32)], axis=0)
    mod = modulation(cond, w_ada, b_ada)
    mod = mod.reshape(depth, COND_ROWS, N_MOD, D).transpose(0, 2, 1, 3)

    xa = (x.reshape(n_lat, D), ctx.reshape(B * Lc, D))

    for l in range(depth):
        last = l == depth - 1
        n_rows = n_lat if last else T_all
        sh1, sc1, ga1, sh2, sc2, ga2 = (mod[l, m] for m in range(N_MOD))

        w_out_bf = cast_cols(w_out, l, 0, D, cb)
        w_pw_bf = cast_cols(w_pw, l, 0, d_conv, _pick(d_conv, 512, 128))
        wgs_bf = cast_cols(w_gate_s, l, 0, fh, fh)
        wus_bf = cast_cols(w_up_s, l, 0, fh, fh)
        wds_bf = cast_cols(w_down_s, l, 0, D, cb)

        h = norm_mod(xa, g_norm1[l], sh1, sc1, L=L, B=B, T=T_all)
        z_qkv = matmul(h, w_in, l, 0, v_end, T_all, cb)
        z_f = matmul(h, w_in, l, v_end, d_fourier, n_rows, cb)
        z_c = matmul(h, w_in, l, f_end, 2 * d_conv, n_rows, cb)
        z_s = matmul(h, w_in, l, c_end, 2 * d_sgu, n_rows, cb)
        qn, kn = qk_norm_rope(z_qkv, g_q[l], g_k[l], L=L, B=B, d_attn=d_attn, kv_w=kv_w)
        y_a = attention(qn, kn, z_qkv, q_row0=0, Lq=L, segs=[(0, L), (n_lat, Lc)],
                        B=B, d_attn=d_attn, kv_w=kv_w)
        y_f = fourier(z_f, w_fourier[l], row0=0, Ls=L, B=B)
        if not last:
            y_a = (y_a, attention(qn, kn, z_qkv, q_row0=n_lat, Lq=Lc, segs=[(n_lat, Lc)],
                                  B=B, d_attn=d_attn, kv_w=kv_w))
            y_f = (y_f, fourier(z_f, w_fourier[l], row0=n_lat, Ls=Lc, B=B))
        y_c = conv_module(z_c, w_dw[l], b_dw[l], g_conv_ln[l], b_conv_ln[l], w_pw_bf, b_pw[l],
                          L=L, Lc=Lc, B=B, n_rows=n_rows)
        y_s = sgu_mixer(z_s, g_sgu[l], w_spatial[l], b_spatial[l], n_rows=n_rows)
        xa = out_proj([y_a, y_f, y_c, y_s], w_out_bf, xa, ga1, L=L, B=B, n_rows=n_rows)

        h2p, eidx, wsel, rank, cnt = router(xa, g_norm2[l], sh2, sc2, w_router[l], b_router[l],
                                            L=L, B=B, n_rows=n_rows)
        n_tiles = -(-n_rows * TOP_K // exp_tr) + n_exp
        pos, gid, nvalid, nused, wslot, nxt = _dispatch_plan(
            cnt[:, 0].astype(I32), eidx, rank, tr=exp_tr, n_tiles=n_tiles)
        xs = dispatch(_tile_major(pos, tok_tile), h2p, n_tiles * exp_tr, tok_tile)
        ys = experts(xs, gid, nvalid, nused, wslot, nxt, w_gate_e, w_up_e, w_down_e, l, tr=exp_tr)
        xa = combine(_tile_major(pos, comb_tile), xa, h2p, _tile_major(wsel, comb_tile), ga2,
                     wgs_bf, wus_bf, wds_bf, ys, L=L, B=B, n_rows=n_rows, tm=comb_tile)

    return xa.reshape(B, L, D)
```

```python
import functools

import jax
import jax.numpy as jnp
import numpy as np
from jax import lax
from jax.experimental import pallas as pl
from jax.experimental.pallas import tpu as pltpu

F32 = jnp.float32
BF16 = jnp.bfloat16
I32 = jnp.int32

HEAD_DIM = 128
GRID_W = 64
ROPE_THETA = 10000.0
NORM_EPS = 1e-6
GQA_GROUP = 4
N_FOURIER_GROUPS = 4
CONV_WIDTH = 31
CONV_PAD = CONV_WIDTH // 2
CONV_HALO = 16
SGU_CHUNK = 128
N_MOD = 6
TOP_K = 8
EXPERTS_PER_GROUP = 8
TOPK_GROUPS = 4
ROUTED_SCALE = 2.5
COND_ROWS = 8
ROW_GROUP = 8
LANES = 128
CONV_ROW_BLOCK = 64

VMEM_LIMIT = 56 * 1024 * 1024


def _cp(sem, vmem=VMEM_LIMIT, **kw):
    return pltpu.CompilerParams(dimension_semantics=sem, vmem_limit_bytes=vmem, **kw)


def _pick(n, pref, mult=8):
    if n <= pref:
        return n
    for t in range(pref - pref % mult, 0, -mult):
        if n % t == 0:
            return t
    raise ValueError(f"no tile for {n} <= {pref}")


def _rms(t, eps=NORM_EPS):
    return t * lax.rsqrt(jnp.mean(t * t, axis=-1, keepdims=True) + eps)


_HI16 = -65536


def _pack_halves(v):
    half = v.shape[1] // 2
    lo = lax.bitcast_convert_type(v[:, :half].astype(BF16).astype(F32), I32)
    hi = lax.bitcast_convert_type(v[:, half:].astype(BF16).astype(F32), I32)
    return lax.shift_right_logical(lo, 16) | (hi & _HI16)


def _unpack_halves(p):
    lo = lax.bitcast_convert_type(lax.shift_left(p, 16), F32)
    hi = lax.bitcast_convert_type(p & _HI16, F32)
    return lo, hi


def _cast_kernel(x_ref, o_ref):
    o_ref[...] = x_ref[...].astype(o_ref.dtype)


def cast_cols(w, l, col0, ncols, cb):
    _, R, _ = w.shape
    tr = _pick(R, 512)
    return pl.pallas_call(
        _cast_kernel,
        grid=(R // tr, ncols // cb),
        in_specs=[pl.BlockSpec((None, tr, cb), lambda i, j: (l, i, col0 // cb + j))],
        out_specs=pl.BlockSpec((tr, cb), lambda i, j: (i, j)),
        out_shape=jax.ShapeDtypeStruct((R, ncols), BF16),
        compiler_params=_cp(("parallel", "parallel")),
        name="cast_bf16",
    )(w)


def _modulation_kernel(c_ref, w_ref, b_ref, o_ref):
    s = jax.nn.silu(c_ref[...]).astype(BF16)
    o_ref[...] = jnp.dot(s, w_ref[...].astype(BF16), preferred_element_type=F32) + b_ref[...]


def modulation(cond, w_ada, b_ada):
    depth, D, N = w_ada.shape
    tn = _pick(N, 512, 128)
    return pl.pallas_call(
        _modulation_kernel,
        grid=(depth, N // tn),
        in_specs=[
            pl.BlockSpec((COND_ROWS, D), lambda l, j: (0, 0)),
            pl.BlockSpec((None, D, tn), lambda l, j: (l, 0, j)),
            pl.BlockSpec((None, 1, tn), lambda l, j: (l, 0, j)),
        ],
        out_specs=pl.BlockSpec((None, COND_ROWS, tn), lambda l, j: (l, 0, j)),
        out_shape=jax.ShapeDtypeStruct((depth, COND_ROWS, N), F32),
        compiler_params=_cp(("parallel", "parallel")),
        name="modulation",
    )(cond, w_ada, b_ada.reshape(depth, 1, N))


def _cond_row(i, tm, L, B):
    return jnp.minimum((i * tm) // L, B)


def _split_specs(arr, block, n_lat_tiles, col_of):
    if isinstance(arr, tuple):
        return ([pl.BlockSpec(block, lambda *ids: (jnp.minimum(ids[-1], n_lat_tiles - 1), col_of(ids))),
                 pl.BlockSpec(block, lambda *ids: (jnp.maximum(ids[-1] - n_lat_tiles, 0), col_of(ids)))],
                list(arr))
    return [pl.BlockSpec(block, lambda *ids: (ids[-1], col_of(ids)))], [arr]


def _split_read(refs, i, n_lat_tiles):
    if len(refs) == 2:
        return jnp.where(i < n_lat_tiles, refs[0][...], refs[1][...])
    return refs[0][...]


def _norm_mod_kernel(*refs, tm, L, B, n_x, n_lat_tiles):
    x_refs = refs[:n_x]
    g_ref, sh_ref, sc_ref, o_ref = refs[n_x:]
    i = pl.program_id(0)
    r = _cond_row(i, tm, L, B)
    xn = _rms(_split_read(x_refs, i, n_lat_tiles)) * g_ref[...]
    o_ref[...] = (xn * (1.0 + sc_ref[pl.ds(r, 1), :]) + sh_ref[pl.ds(r, 1), :]).astype(o_ref.dtype)


def norm_mod(x, g, sh, sc, *, L, B, T):
    D = g.shape[0]
    tm = _pick(L, 512)
    n_lat_tiles = B * L // tm
    x_specs, x_args = _split_specs(x, (tm, D), n_lat_tiles, lambda ids: 0)
    return pl.pallas_call(
        functools.partial(_norm_mod_kernel, tm=tm, L=L, B=B, n_x=len(x_args), n_lat_tiles=n_lat_tiles),
        grid=(T // tm,),
        in_specs=x_specs + [
            pl.BlockSpec((1, D), lambda i: (0, 0)),
            pl.BlockSpec((COND_ROWS, D), lambda i: (0, 0)),
            pl.BlockSpec((COND_ROWS, D), lambda i: (0, 0)),
        ],
        out_specs=pl.BlockSpec((tm, D), lambda i: (i, 0)),
        out_shape=jax.ShapeDtypeStruct((T, D), BF16),
        compiler_params=_cp(("parallel",)),
        name="norm_mod",
    )(*x_args, g.reshape(1, D), sh, sc)


def _mm_kernel(a_ref, w_ref, o_ref, w_bf):
    @pl.when(pl.program_id(1) == 0)
    def _():
        w_bf[...] = w_ref[...].astype(BF16)

    o_ref[...] = jnp.dot(a_ref[...], w_bf[...], preferred_element_type=F32).astype(o_ref.dtype)


def matmul(a, w, l, col0, ncols, n_rows, tn, out_dtype=BF16):
    _, K = a.shape
    tm = _pick(n_rows, 1024)
    return pl.pallas_call(
        _mm_kernel,
        grid=(ncols // tn, n_rows // tm),
        in_specs=[
            pl.BlockSpec((tm, K), lambda j, i: (i, 0)),
            pl.BlockSpec((None, K, tn), lambda j, i: (l, 0, col0 // tn + j)),
        ],
        out_specs=pl.BlockSpec((tm, tn), lambda j, i: (i, j)),
        out_shape=jax.ShapeDtypeStruct((n_rows, ncols), out_dtype),
        scratch_shapes=[pltpu.VMEM((K, tn), BF16)],
        compiler_params=_cp(("arbitrary", "arbitrary")),
        name="in_proj",
    )(a, w)


def _rope_tables(L, tm):
    rows = L // GRID_W
    row = np.repeat(np.arange(rows), GRID_W).astype(np.float32)
    col = np.tile(np.arange(GRID_W), rows).astype(np.float32)
    axis_dim = HEAD_DIM // 2
    inv = (np.float32(ROPE_THETA) ** (-np.arange(0, axis_dim, 2, dtype=np.float32) / axis_dim)).astype(np.float32)
    ang = np.concatenate([row[:, None] * inv, col[:, None] * inv], axis=-1).astype(np.float64)
    cos = np.repeat(np.cos(ang), 2, axis=-1)
    sin = np.repeat(np.sin(ang), 2, axis=-1)
    sign = np.tile(np.array([-1.0, 1.0]), HEAD_DIM // 2)
    cos = np.concatenate([cos, np.ones((tm, HEAD_DIM))], axis=0)
    sin = np.concatenate([sin * sign, np.zeros((tm, HEAD_DIM))], axis=0)
    return jnp.asarray(cos, F32), jnp.asarray(sin, F32)


def _qk_kernel(q_ref, k_ref, cos_ref, sin_ref, gq_ref, gk_ref, qo_ref, ko_ref, *, n_q, n_kv):
    cos = cos_ref[...]
    sin = sin_ref[...]
    even = (lax.broadcasted_iota(I32, cos.shape, 1) & 1) == 0

    def norm_rope(t, g):
        n = _rms(t.astype(F32)) * g
        partner = jnp.where(even, pltpu.roll(n, HEAD_DIM - 1, 1), pltpu.roll(n, 1, 1))
        return n * cos + partner * sin

    q_scale = HEAD_DIM ** -0.5
    for h in range(n_q):
        sl = slice(h * HEAD_DIM, (h + 1) * HEAD_DIM)
        qo_ref[:, sl] = (norm_rope(q_ref[:, sl], gq_ref[...]) * q_scale).astype(qo_ref.dtype)
    for h in range(n_kv):
        sl = slice(h * HEAD_DIM, (h + 1) * HEAD_DIM)
        ko_ref[:, sl] = norm_rope(k_ref[:, sl], gk_ref[...]).astype(ko_ref.dtype)


def qk_norm_rope(z_qkv, g_q, g_k, *, L, B, d_attn, kv_w):
    T = z_qkv.shape[0]
    tm = _pick(L, 256)
    cos, sin = _rope_tables(L, tm)
    n_lat = B * L // tm
    per_seq = L // tm

    def tab_idx(i):
        return (jnp.where(i < n_lat, i % per_seq, per_seq), 0)

    return pl.pallas_call(
        functools.partial(_qk_kernel, n_q=d_attn // HEAD_DIM, n_kv=kv_w // HEAD_DIM),
        grid=(T // tm,),
        in_specs=[
            pl.BlockSpec((tm, d_attn), lambda i: (i, 0)),
            pl.BlockSpec((tm, kv_w), lambda i: (i, d_attn // kv_w)),
            pl.BlockSpec((tm, HEAD_DIM), tab_idx),
            pl.BlockSpec((tm, HEAD_DIM), tab_idx),
            pl.BlockSpec((1, HEAD_DIM), lambda i: (0, 0)),
            pl.BlockSpec((1, HEAD_DIM), lambda i: (0, 0)),
        ],
        out_specs=[
            pl.BlockSpec((tm, d_attn), lambda i: (i, 0)),
            pl.BlockSpec((tm, kv_w), lambda i: (i, 0)),
        ],
        out_shape=[
            jax.ShapeDtypeStruct((T, d_attn), BF16),
            jax.ShapeDtypeStruct((T, kv_w), BF16),
        ],
        compiler_params=_cp(("parallel",)),
        name="qk_norm_rope",
    )(z_qkv, z_qkv, cos, sin, g_q.reshape(1, HEAD_DIM), g_k.reshape(1, HEAD_DIM))


def _attn_kernel(*refs, n_seg):
    q_ref = refs[0]
    k_refs = refs[1:1 + n_seg]
    v_refs = refs[1 + n_seg:1 + 2 * n_seg]
    o_ref = refs[-1]
    v_ext = [jnp.concatenate([v[...], jnp.ones(v.shape, BF16)], axis=1) for v in v_refs]
    for h in range(GQA_GROUP):
        sl = slice(h * HEAD_DIM, (h + 1) * HEAD_DIM)
        q = q_ref[:, sl]
        s = [lax.dot_general(q, k[...], (((1,), (1,)), ((), ())), preferred_element_type=F32)
             for k in k_refs]
        m = functools.reduce(jnp.maximum, [jnp.max(x, axis=-1, keepdims=True) for x in s])
        p = [jnp.exp((x - m).astype(BF16)) for x in s]
        oe = functools.reduce(jnp.add, [jnp.dot(x, v, preferred_element_type=F32) for x, v in zip(p, v_ext)])
        o_ref[:, sl] = (oe[:, :HEAD_DIM] / oe[:, HEAD_DIM:HEAD_DIM + 1]).astype(o_ref.dtype)


def attention(qn, kn, z_qkv, *, q_row0, Lq, segs, B, d_attn, kv_w):
    n_kv = kv_w // HEAD_DIM
    gw = GQA_GROUP * HEAD_DIM
    tq = _pick(Lq, 512)
    nq = Lq // tq
    v_col0 = (d_attn + kv_w) // HEAD_DIM

    def q_idx(b, g, qi):
        return (q_row0 // tq + b * nq + qi, g)

    in_specs = [pl.BlockSpec((tq, gw), q_idx)]
    args = [qn]
    for row0, ln in segs:
        in_specs.append(pl.BlockSpec((ln, HEAD_DIM), lambda b, g, qi, row0=row0, ln=ln: (row0 // ln + b, g)))
        args.append(kn)
    for row0, ln in segs:
        in_specs.append(
            pl.BlockSpec((ln, HEAD_DIM), lambda b, g, qi, row0=row0, ln=ln: (row0 // ln + b, v_col0 + g)))
        args.append(z_qkv)
    return pl.pallas_call(
        functools.partial(_attn_kernel, n_seg=len(segs)),
        grid=(B, n_kv, nq),
        in_specs=in_specs,
        out_specs=pl.BlockSpec((tq, gw), lambda b, g, qi: (b * nq + qi, g)),
        out_shape=jax.ShapeDtypeStruct((B * Lq, d_attn), BF16),
        compiler_params=_cp(("parallel", "parallel", "parallel")),
        name="attention",
    )(*args)


def _dft_tables(n, dtype):
    k = np.arange(n, dtype=np.int64)
    ang = ((k[:, None] * k[None, :]) % n).astype(np.float64) * (2.0 * np.pi / n)
    return jnp.asarray(np.cos(ang), dtype), jnp.asarray(np.sin(ang), dtype)


def _fourier_kernel(t_ref, cl_ref, sl_ref, cc_ref, sc_ref, w_ref, o_ref, *, scale):
    w = w_ref[...]
    hi = lax.Precision.HIGHEST
    mc = jnp.dot(cc_ref[...], w, precision=hi, preferred_element_type=F32).astype(BF16)
    ms = jnp.dot(sc_ref[...], w, precision=hi, preferred_element_type=F32).astype(BF16)
    t = t_ref[...]
    bc = jnp.dot(t, mc, preferred_element_type=F32).astype(BF16)
    bs = jnp.dot(t, ms, preferred_element_type=F32).astype(BF16)
    y = (jnp.dot(cl_ref[...], bc, preferred_element_type=F32)
         - jnp.dot(sl_ref[...], bs, preferred_element_type=F32))
    o_ref[...] = (y * scale).astype(o_ref.dtype)


def fourier(z_f, w_f, *, row0, Ls, B):
    ng, fg, _ = w_f.shape
    cl, sl = _dft_tables(Ls, BF16)
    cc, sc = _dft_tables(fg, F32)
    scale = 1.0 / float(Ls * fg) ** 0.5
    return pl.pallas_call(
        functools.partial(_fourier_kernel, scale=scale),
        grid=(B, ng),
        in_specs=[
            pl.BlockSpec((Ls, fg), lambda b, g: (row0 // Ls + b, g)),
            pl.BlockSpec((Ls, Ls), lambda b, g: (0, 0)),
            pl.BlockSpec((Ls, Ls), lambda b, g: (0, 0)),
            pl.BlockSpec((fg, fg), lambda b, g: (0, 0)),
            pl.BlockSpec((fg, fg), lambda b, g: (0, 0)),
            pl.BlockSpec((None, fg, fg), lambda b, g: (g, 0, 0)),
        ],
        out_specs=pl.BlockSpec((Ls, fg), lambda b, g: (b, g)),
        out_shape=jax.ShapeDtypeStruct((B * Ls, ng * fg), BF16),
        compiler_params=_cp(("parallel", "parallel")),
        name="fourier",
    )(z_f, cl, sl, cc, sc, w_f)


def _conv_kernel(cur_ref, prev_ref, next_ref, wdw_ref, bdw_ref, gln_ref, bln_ref, wpw_ref, bpw_ref,
                 o_ref, gbuf, shifted, cbuf, *, tr, dc, n_lat, per_lat, per_ctx):
    i = pl.program_id(0)
    j = jnp.where(i < n_lat, i % per_lat, (i - n_lat) % per_ctx)
    per = jnp.where(i < n_lat, per_lat, per_ctx)
    first = j == 0
    last = j == per - 1

    def glu(zb):
        return zb[:, :dc].astype(F32) * jax.nn.sigmoid(zb[:, dc:].astype(F32))

    gbuf[0:CONV_HALO, :] = jnp.where(first, 0.0, glu(prev_ref[...]))
    gbuf[CONV_HALO:CONV_HALO + tr, :] = glu(cur_ref[...])
    gbuf[CONV_HALO + tr:, :] = jnp.where(last, 0.0, glu(next_ref[...]))
    n_sh = tr + CONV_HALO + ROW_GROUP
    for s in range(1, ROW_GROUP):
        shifted[s - 1, :, :] = gbuf[s:s + n_sh, :]
    first_off = CONV_HALO - CONV_PAD
    rb = min(tr, CONV_ROW_BLOCK)
    for c in range(dc // LANES):
        cols = slice(c * LANES, (c + 1) * LANES)
        for r0 in range(0, tr, rb):
            a = jnp.broadcast_to(bdw_ref[:, cols], (rb, LANES))
            for k in range(CONV_WIDTH):
                off = first_off + k + r0
                s = off % ROW_GROUP
                src = gbuf[off:off + rb, cols] if s == 0 else shifted[s - 1, off - s:off - s + rb, cols]
                a = a + src * wdw_ref[k:k + 1, cols]
            cbuf[r0:r0 + rb, cols] = a
    acc = cbuf[...]
    mu = jnp.mean(acc, axis=-1, keepdims=True)
    cen = acc - mu
    var = jnp.mean(cen * cen, axis=-1, keepdims=True)
    hn = cen * lax.rsqrt(var + NORM_EPS) * gln_ref[...] + bln_ref[...]
    act = jax.nn.silu(hn).astype(BF16)
    o_ref[...] = (jnp.dot(act, wpw_ref[...], preferred_element_type=F32) + bpw_ref[...]).astype(o_ref.dtype)


def conv_module(z_c, w_dw, b_dw, g_ln, b_ln, w_pw_bf, b_pw, *, L, Lc, B, n_rows):
    dc = w_pw_bf.shape[0]
    tr = _pick(Lc, 256, CONV_HALO)
    hpt = tr // CONV_HALO
    n_halo = z_c.shape[0] // CONV_HALO
    w_pad = jnp.concatenate([w_dw, jnp.zeros((32 - CONV_WIDTH, dc), F32)], axis=0)
    row = lambda v: v.reshape(1, dc)
    return pl.pallas_call(
        functools.partial(_conv_kernel, tr=tr, dc=dc, n_lat=B * L // tr, per_lat=L // tr, per_ctx=Lc // tr),
        grid=(n_rows // tr,),
        in_specs=[
            pl.BlockSpec((tr, 2 * dc), lambda i: (i, 0)),
            pl.BlockSpec((CONV_HALO, 2 * dc), lambda i: (jnp.maximum(i * hpt - 1, 0), 0)),
            pl.BlockSpec((CONV_HALO, 2 * dc), lambda i: (jnp.minimum((i + 1) * hpt, n_halo - 1), 0)),
            pl.BlockSpec((32, dc), lambda i: (0, 0)),
            pl.BlockSpec((1, dc), lambda i: (0, 0)),
            pl.BlockSpec((1, dc), lambda i: (0, 0)),
            pl.BlockSpec((1, dc), lambda i: (0, 0)),
            pl.BlockSpec((dc, dc), lambda i: (0, 0)),
            pl.BlockSpec((1, dc), lambda i: (0, 0)),
        ],
        out_specs=pl.BlockSpec((tr, dc), lambda i: (i, 0)),
        out_shape=jax.ShapeDtypeStruct((n_rows, dc), BF16),
        scratch_shapes=[
            pltpu.VMEM((tr + 2 * CONV_HALO, dc), F32),
            pltpu.VMEM((ROW_GROUP - 1, tr + CONV_HALO + ROW_GROUP, dc), F32),
            pltpu.VMEM((tr, dc), F32),
        ],
        compiler_params=_cp(("parallel",)),
        name="conv_module",
    )(z_c, z_c, z_c, w_pad, row(b_dw), row(g_ln), row(b_ln), w_pw_bf, row(b_pw))


def _sgu_kernel(z_ref, g_ref, ws_ref, bs_ref, o_ref, *, tm, ds, n_heads):
    for h in range(n_heads):
        sl = slice(h * HEAD_DIM, (h + 1) * HEAD_DIM)
        u = jax.nn.gelu(z_ref[:, sl].astype(F32))
        v = jax.nn.gelu(z_ref[:, ds + h * HEAD_DIM:ds + (h + 1) * HEAD_DIM].astype(F32))
        vn = (_rms(v) * g_ref[:, sl]).astype(BF16)
        w = ws_ref[h].astype(BF16)
        b = bs_ref[:, h:h + 1]
        for c in range(tm // SGU_CHUNK):
            rows = slice(c * SGU_CHUNK, (c + 1) * SGU_CHUNK)
            s = jnp.dot(w, vn[rows, :], preferred_element_type=F32) + b
            o_ref[rows, sl] = (u[rows, :] * s).astype(o_ref.dtype)


def sgu_mixer(z_s, g_sgu, w_s, b_s, *, n_rows):
    n_heads = w_s.shape[0]
    ds = n_heads * HEAD_DIM
    tm = _pick(n_rows, 256, SGU_CHUNK)
    return pl.pallas_call(
        functools.partial(_sgu_kernel, tm=tm, ds=ds, n_heads=n_heads),
        grid=(n_rows // tm,),
        in_specs=[
            pl.BlockSpec((tm, 2 * ds), lambda i: (i, 0)),
            pl.BlockSpec((1, ds), lambda i: (0, 0)),
            pl.BlockSpec((n_heads, SGU_CHUNK, SGU_CHUNK), lambda i: (0, 0, 0)),
            pl.BlockSpec((SGU_CHUNK, n_heads), lambda i: (0, 0)),
        ],
        out_specs=pl.BlockSpec((tm, ds), lambda i: (i, 0)),
        out_shape=jax.ShapeDtypeStruct((n_rows, ds), BF16),
        compiler_params=_cp(("parallel",)),
        name="sgu_mixer",
    )(z_s, g_sgu.reshape(1, ds), w_s, b_s.T)


def _out_proj_kernel(*refs, tm, L, B, widths, split, n_lat_tiles, n_x):
    n_y = sum(2 if sp else 1 for sp in split)
    y_refs = refs[:n_y]
    w_ref = refs[n_y]
    x_refs = refs[n_y + 1:n_y + 1 + n_x]
    ga_ref, o_ref = refs[n_y + 1 + n_x:]
    i = pl.program_id(1)
    r = _cond_row(i, tm, L, B)
    acc = None
    k0 = 0
    p = 0
    for wd, sp in zip(widths, split):
        if sp:
            y = jnp.where(i < n_lat_tiles, y_refs[p][...], y_refs[p + 1][...])
            p += 2
        else:
            y = y_refs[p][...]
            p += 1
        part = jnp.dot(y, w_ref[k0:k0 + wd, :], preferred_element_type=F32)
        acc = part if acc is None else acc + part
        k0 += wd
    o_ref[...] = _split_read(x_refs, i, n_lat_tiles) + ga_ref[pl.ds(r, 1), :] * acc


def out_proj(ys, w_out_bf, x, ga, *, L, B, n_rows):
    D = w_out_bf.shape[1]
    tm = _pick(L, 512)
    tn = _pick(D, 1024, 128)
    n_lat_tiles = B * L // tm
    split = tuple(isinstance(y, tuple) for y in ys)
    widths = tuple((y[0] if sp else y).shape[1] for y, sp in zip(ys, split))
    in_specs, args = [], []
    for y, wd, sp in zip(ys, widths, split):
        if sp:
            in_specs.append(pl.BlockSpec((tm, wd), lambda j, i: (jnp.minimum(i, n_lat_tiles - 1), 0)))
            in_specs.append(pl.BlockSpec((tm, wd), lambda j, i: (jnp.maximum(i - n_lat_tiles, 0), 0)))
            args += [y[0], y[1]]
        else:
            in_specs.append(pl.BlockSpec((tm, wd), lambda j, i: (i, 0)))
            args.append(y)
    x_specs, x_args = _split_specs(x, (tm, tn), n_lat_tiles, lambda ids: ids[0])
    in_specs += [pl.BlockSpec((sum(widths), tn), lambda j, i: (0, j))] + x_specs
    in_specs += [pl.BlockSpec((COND_ROWS, tn), lambda j, i: (0, j))]
    return pl.pallas_call(
        functools.partial(_out_proj_kernel, tm=tm, L=L, B=B, widths=widths, split=split,
                          n_lat_tiles=n_lat_tiles, n_x=len(x_args)),
        grid=(D // tn, n_rows // tm),
        in_specs=in_specs,
        out_specs=pl.BlockSpec((tm, tn), lambda j, i: (i, j)),
        out_shape=jax.ShapeDtypeStruct((n_rows, D), F32),
        compiler_params=_cp(("parallel", "parallel")),
        name="out_proj",
    )(*args, w_out_bf, *x_args, ga)


def _router_kernel(x_ref, g_ref, sh_ref, sc_ref, wr_ref, br_ref,
                   h2_ref, eidx_ref, wsel_ref, rank_ref, cnt_ref, *, tm, L, B, n_exp):
    i = pl.program_id(0)
    r = _cond_row(i, tm, L, B)
    xn = _rms(x_ref[...]) * g_ref[...]
    h2 = xn * (1.0 + sc_ref[pl.ds(r, 1), :]) + sh_ref[pl.ds(r, 1), :]
    h2_ref[...] = _pack_halves(h2).reshape(h2_ref.shape)
    def split(v):
        hi_part = v.astype(BF16)
        return hi_part, (v - hi_part.astype(F32)).astype(BF16)

    def nt_dot(a, b):
        return lax.dot_general(a, b, (((1,), (1,)), ((), ())), preferred_element_type=F32)

    w_hi, w_lo = split(wr_ref[...])
    h_hi, h_lo = split(h2)
    logits = nt_dot(w_hi, h_hi) + (nt_dot(w_lo, h_hi) + nt_dot(w_hi, h_lo))
    s = jax.nn.sigmoid(logits)
    sel = s + br_ref[...]
    shape = (n_exp, tm)
    e_io = lax.broadcasted_iota(I32, shape, 0)
    e_f = e_io.astype(F32)
    k_io = e_io & (EXPERTS_PER_GROUP - 1)
    g_io = e_io >> 3
    n_groups = n_exp // EXPERTS_PER_GROUP
    neg = -jnp.inf

    def partner(v, d):
        up = pltpu.roll(v, n_exp - d, 0)
        dn = pltpu.roll(v, d, 0)
        return jnp.where((k_io & d) == 0, up, dn)

    def group_all(v, op):
        d = 1
        while d < EXPERTS_PER_GROUP:
            v = op(v, partner(v, d))
            d *= 2
        return v

    kf = k_io.astype(F32)
    m1 = group_all(sel, jnp.maximum)
    i1 = group_all(jnp.where(sel == m1, kf, float(EXPERTS_PER_GROUP)), jnp.minimum)
    m2 = group_all(jnp.where(kf == i1, neg, sel), jnp.maximum)
    gs = m1 + m2
    beaten = jnp.zeros(shape, I32)
    for d in range(1, n_groups):
        other = pltpu.roll(gs, n_exp - EXPERTS_PER_GROUP * d, 0)
        og = (g_io + d) & (n_groups - 1)
        beats = (other > gs) | ((other == gs) & (og < g_io))
        beaten = beaten + beats.astype(I32)
    val = jnp.where(beaten < TOPK_GROUPS, sel, neg)

    assign = jnp.zeros(shape, F32)
    w_rows = []
    idx_rows = []
    for j in range(TOP_K):
        m = jnp.max(val, axis=0, keepdims=True)
        idx = jnp.min(jnp.where(val == m, e_f, float(n_exp)), axis=0, keepdims=True)
        hit = e_f == idx
        w_rows.append(jnp.sum(jnp.where(hit, s, 0.0), axis=0, keepdims=True))
        val = jnp.where(hit, neg, val)
        assign = assign + hit.astype(F32)
        idx_rows.append(idx)
        eidx_ref[j:j + 1, :] = idx.astype(I32)
    w_sum = functools.reduce(jnp.add, w_rows)
    for j in range(TOP_K):
        wsel_ref[j:j + 1, :] = w_rows[j] / w_sum * ROUTED_SCALE

    @pl.when(i == 0)
    def _():
        cnt_ref[...] = jnp.zeros_like(cnt_ref)

    earlier = (lax.broadcasted_iota(I32, (tm, tm), 0) < lax.broadcasted_iota(I32, (tm, tm), 1)).astype(BF16)
    before = jnp.dot(assign.astype(BF16), earlier, preferred_element_type=F32) + cnt_ref[...]
    for j in range(TOP_K):
        hit = e_f == idx_rows[j]
        rank_ref[j:j + 1, :] = jnp.sum(jnp.where(hit, before, 0.0), axis=0, keepdims=True).astype(I32)
    cnt_ref[...] = cnt_ref[...] + jnp.sum(assign, axis=1, keepdims=True)


def router(x, g, sh, sc, w_router, b_router, *, L, B, n_rows):
    D = x.shape[1]
    n_exp = w_router.shape[1]
    tm = _pick(L, 512, 128)
    assert n_rows % tm == 0
    return pl.pallas_call(
        functools.partial(_router_kernel, tm=tm, L=L, B=B, n_exp=n_exp),
        grid=(n_rows // tm,),
        in_specs=[
            pl.BlockSpec((tm, D), lambda i: (i, 0)),
            pl.BlockSpec((1, D), lambda i: (0, 0)),
            pl.BlockSpec((COND_ROWS, D), lambda i: (0, 0)),
            pl.BlockSpec((COND_ROWS, D), lambda i: (0, 0)),
            pl.BlockSpec((n_exp, D), lambda i: (0, 0)),
            pl.BlockSpec((n_exp, 1), lambda i: (0, 0)),
        ],
        out_specs=[
            pl.BlockSpec((tm // ROW_GROUP, ROW_GROUP, D // 2), lambda i: (i, 0, 0)),
            pl.BlockSpec((TOP_K, tm), lambda i: (0, i)),
            pl.BlockSpec((TOP_K, tm), lambda i: (0, i)),
            pl.BlockSpec((TOP_K, tm), lambda i: (0, i)),
            pl.BlockSpec((n_exp, 1), lambda i: (0, 0)),
        ],
        out_shape=[
            jax.ShapeDtypeStruct((n_rows // ROW_GROUP, ROW_GROUP, D // 2), I32),
            jax.ShapeDtypeStruct((TOP_K, n_rows), I32),
            jax.ShapeDtypeStruct((TOP_K, n_rows), F32),
            jax.ShapeDtypeStruct((TOP_K, n_rows), I32),
            jax.ShapeDtypeStruct((n_exp, 1), F32),
        ],
        compiler_params=_cp(("arbitrary",)),
        name="router",
    )(x, g.reshape(1, D), sh, sc, w_router.T, b_router.reshape(n_exp, 1))


def _vrow(ref, group, sub):
    return ref.at[group, pl.ds(sub, 1)]


def _hrow(ref, row):
    return ref.at[pl.ds(row, 1)]


def _dispatch_kernel(pos_ref, h_ref, xs_ref, sem, *, tm):
    def issue(g, carry):
        base = g * ROW_GROUP
        for u in range(ROW_GROUP):
            for j in range(TOP_K):
                pltpu.make_async_copy(_vrow(h_ref, g, u), _hrow(xs_ref, pos_ref[j * tm + base + u]),
                                      sem).start(priority=j % 2)
        return carry

    lax.fori_loop(0, tm // ROW_GROUP, issue, 0)
    for j in range(TOP_K):
        pltpu.make_async_copy(xs_ref.at[pl.ds(0, tm)], xs_ref.at[pl.ds(0, tm)], sem).wait()


def dispatch(pos, h2p, n_slots, tm):
    G, _, W = h2p.shape
    T = G * ROW_GROUP
    return pl.pallas_call(
        functools.partial(_dispatch_kernel, tm=tm),
        grid=(T // tm,),
        in_specs=[
            pl.BlockSpec((TOP_K * tm,), lambda i: (i,), memory_space=pltpu.SMEM),
            pl.BlockSpec((tm // ROW_GROUP, ROW_GROUP, W), lambda i: (i, 0, 0)),
        ],
        out_specs=pl.BlockSpec(memory_space=pl.ANY),
        out_shape=jax.ShapeDtypeStruct((n_slots, W), I32),
        scratch_shapes=[pltpu.SemaphoreType.DMA(())],
        compiler_params=_cp(("arbitrary",), has_side_effects=True),
        name="dispatch",
    )(pos, h2p)


def _experts_kernel(gid_ref, nvalid_ref, nused_ref, wslot_ref, nxt_ref, xs_ref, wg_hbm, wu_hbm, wd_hbm, ys_ref,
                    wg_st, wu_st, wd_st, wg_bf, wu_bf, wd_bf, obuf, wsem, osem, *, tr, l, n_tiles):
    i = pl.program_id(0)
    n_used = nused_ref[0]
    first = jnp.logical_or(i == 0, gid_ref[i] != gid_ref[jnp.maximum(i - 1, 0)])
    n_lane_tiles = obuf.shape[2] // LANES

    def weight_copies(e, slot):
        return [pltpu.make_async_copy(src.at[l, e], dst.at[slot], wsem.at[slot])
                for src, dst in ((wg_hbm, wg_st), (wu_hbm, wu_st), (wd_hbm, wd_st))]

    def out_copies(tile, slot):
        return [pltpu.make_async_copy(obuf.at[slot, :, pl.ds(c * LANES, LANES)],
                                      ys_ref.at[pl.ds(tile * tr, tr), c], osem.at[slot])
                for c in range(n_lane_tiles)]

    @pl.when(i == 0)
    def _():
        for cp in weight_copies(gid_ref[0], wslot_ref[0]):
            cp.start(priority=1)

    @pl.when(first)
    def _():
        slot = wslot_ref[i]
        for cp in weight_copies(gid_ref[i], slot):
            cp.wait()
        wg_bf[...] = wg_st[slot].astype(BF16)
        wu_bf[...] = wu_st[slot].astype(BF16)
        wd_bf[...] = wd_st[slot].astype(BF16)

        @pl.when(nxt_ref[i] >= 0)
        def _():
            for cp in weight_copies(nxt_ref[i], 1 - slot):
                cp.start(priority=1)

    @pl.when(jnp.logical_and(i >= 2, i - 2 < n_used))
    def _():
        for cp in out_copies(i - 2, i % 2):
            cp.wait()

    @pl.when(i < n_used)
    def _():
        half = xs_ref.shape[1]
        valid = lax.broadcasted_iota(I32, (tr, 1), 0) < nvalid_ref[i]
        lo, hi = _unpack_halves(xs_ref[...])
        lo = jnp.where(valid, lo, 0.0).astype(BF16)
        hi = jnp.where(valid, hi, 0.0).astype(BF16)

        def proj(w_bf):
            return (jnp.dot(lo, w_bf[:half, :], preferred_element_type=F32)
                    + jnp.dot(hi, w_bf[half:, :], preferred_element_type=F32))

        hh = (jax.nn.silu(proj(wg_bf)) * proj(wu_bf)).astype(BF16)
        slot = i % 2
        obuf[slot] = _pack_halves(jnp.dot(hh, wd_bf[...], preferred_element_type=F32))
        for c, cp in enumerate(out_copies(i, slot)):
            cp.start(priority=c % 2)

    @pl.when(i == n_tiles - 1)
    def _():
        @pl.when(jnp.logical_and(i >= 1, i - 1 < n_used))
        def _():
            for cp in out_copies(i - 1, (i - 1) % 2):
                cp.wait()

        @pl.when(i < n_used)
        def _():
            for cp in out_copies(i, i % 2):
                cp.wait()


def experts(xs, gid, nvalid, nused, wslot, nxt, w_ge, w_ue, w_de, l, *, tr):
    P, W = xs.shape
    D = 2 * W
    F = w_ge.shape[3]
    n_tiles = P // tr

    def row_idx(i, gid, nvalid, nused, wslot, nxt):
        return (jnp.minimum(i, nused[0] - 1), 0)

    return pl.pallas_call(
        functools.partial(_experts_kernel, tr=tr, l=l, n_tiles=n_tiles),
        grid_spec=pltpu.PrefetchScalarGridSpec(
            num_scalar_prefetch=5,
            grid=(n_tiles,),
            in_specs=[
                pl.BlockSpec((tr, W), row_idx),
                pl.BlockSpec(memory_space=pl.ANY),
                pl.BlockSpec(memory_space=pl.ANY),
                pl.BlockSpec(memory_space=pl.ANY),
            ],
            out_specs=pl.BlockSpec(memory_space=pl.ANY),
            scratch_shapes=[
                pltpu.VMEM((2, D, F), F32), pltpu.VMEM((2, D, F), F32), pltpu.VMEM((2, F, D), F32),
                pltpu.VMEM((D, F), BF16), pltpu.VMEM((D, F), BF16), pltpu.VMEM((F, D), BF16),
                pltpu.VMEM((2, tr, W), I32),
                pltpu.SemaphoreType.DMA((2,)), pltpu.SemaphoreType.DMA((2,)),
            ],
        ),
        out_shape=jax.ShapeDtypeStruct((P, W // LANES, LANES), I32),
        compiler_params=_cp(("arbitrary",), has_side_effects=True),
        name="experts",
    )(gid, nvalid, nused, wslot, nxt, xs, w_ge, w_ue, w_de)


def _combine_kernel(pos_ref, posn_ref, w_ref, x_ref, h_ref, ga_ref, wgs_ref, wus_ref, wds_ref, ys_ref,
                    o_ref, gbuf2, acc, sems, *, tm, L, B):
    i = pl.program_id(0)
    r = _cond_row(i, tm, L, B)
    half = h_ref.shape[2]
    n_slab = gbuf2.shape[3]
    slot = i % 2
    gbuf = gbuf2.at[slot]
    sem = sems.at[slot]

    def issue_tile(p_ref, dst, dsem):
        def issue(t, carry):
            for j in range(TOP_K):
                pltpu.make_async_copy(ys_ref.at[p_ref[j * tm + t]], dst.at[j, t], dsem).start(priority=j % 2)
            return carry

        lax.fori_loop(0, tm, issue, 0, unroll=2)

    @pl.when(i == 0)
    def _():
        issue_tile(pos_ref, gbuf, sem)

    lo, hi = _unpack_halves(h_ref[...].reshape(tm, half))
    lo = lo.astype(BF16)
    hi = hi.astype(BF16)

    def proj(w_bf):
        return (jnp.dot(lo, w_bf[:half, :], preferred_element_type=F32)
                + jnp.dot(hi, w_bf[half:, :], preferred_element_type=F32))

    hh = (jax.nn.silu(proj(wgs_ref)) * proj(wus_ref)).astype(BF16)
    shared = jnp.dot(hh, wds_ref[...], preferred_element_type=F32)

    for j in range(TOP_K):
        pltpu.make_async_copy(ys_ref.at[pl.ds(0, tm)], gbuf.at[j], sem).wait()

    def token_loop(prefetch):
        nbuf = gbuf2.at[1 - slot]
        nsem = sems.at[1 - slot]

        def token(t, carry):
            a_lo = jnp.zeros((n_slab, LANES), F32)
            a_hi = jnp.zeros((n_slab, LANES), F32)
            for j in range(TOP_K):
                y_lo, y_hi = _unpack_halves(gbuf[j, t])
                w = w_ref[j * tm + t]
                a_lo = a_lo + w * y_lo
                a_hi = a_hi + w * y_hi
            acc[t, 0:n_slab, :] = a_lo
            acc[t, n_slab:2 * n_slab, :] = a_hi
            if prefetch:
                for j in range(TOP_K):
                    pltpu.make_async_copy(ys_ref.at[posn_ref[j * tm + t]], nbuf.at[j, t],
                                          nsem).start(priority=j % 2)
            return carry

        lax.fori_loop(0, tm, token, 0, unroll=2)

    not_last = i + 1 < pl.num_programs(0)

    @pl.when(not_last)
    def _():
        token_loop(True)

    @pl.when(jnp.logical_not(not_last))
    def _():
        token_loop(False)

    ga = ga_ref[pl.ds(r, 1), :]
    for c in range(2 * n_slab):
        cols = slice(c * LANES, (c + 1) * LANES)
        o_ref[:, cols] = x_ref[:, cols] + ga[:, cols] * (shared[:, cols] + acc[:, c, :])


def combine(pos, x, h2p, wsel_flat, ga, wgs_bf, wus_bf, wds_bf, ys, *, L, B, n_rows, tm):
    D = x.shape[1]
    W = D // 2
    F = wgs_bf.shape[1]
    n_slab = W // LANES
    n_steps = n_rows // tm
    return pl.pallas_call(
        functools.partial(_combine_kernel, tm=tm, L=L, B=B),
        grid=(n_steps,),
        in_specs=[
            pl.BlockSpec((TOP_K * tm,), lambda i: (i,), memory_space=pltpu.SMEM),
            pl.BlockSpec((TOP_K * tm,), lambda i: (jnp.minimum(i + 1, n_steps - 1),), memory_space=pltpu.SMEM),
            pl.BlockSpec((TOP_K * tm,), lambda i: (i,), memory_space=pltpu.SMEM),
            pl.BlockSpec((tm, D), lambda i: (i, 0)),
            pl.BlockSpec((tm // ROW_GROUP, ROW_GROUP, W), lambda i: (i, 0, 0)),
            pl.BlockSpec((COND_ROWS, D), lambda i: (0, 0)),
            pl.BlockSpec((D, F), lambda i: (0, 0)),
            pl.BlockSpec((D, F), lambda i: (0, 0)),
            pl.BlockSpec((F, D), lambda i: (0, 0)),
            pl.BlockSpec(memory_space=pl.ANY),
        ],
        out_specs=pl.BlockSpec((tm, D), lambda i: (i, 0)),
        out_shape=jax.ShapeDtypeStruct((n_rows, D), F32),
        scratch_shapes=[pltpu.VMEM((2, TOP_K, tm, n_slab, LANES), I32),
                        pltpu.VMEM((tm, 2 * n_slab, LANES), F32),
                        pltpu.SemaphoreType.DMA((2,))],
        compiler_params=_cp(("arbitrary",)),
        name="combine",
    )(pos, pos, wsel_flat, x, h2p, ga, wgs_bf, wus_bf, wds_bf, ys)


def _tile_major(a, tile):
    k, T = a.shape
    return a.reshape(k, T // tile, tile).transpose(1, 0, 2).reshape(-1)


def _dispatch_plan(cnt, eidx, rank, *, tr, n_tiles):
    n_exp = cnt.shape[0]
    e_ids = jnp.arange(n_exp, dtype=I32)
    tiles_e = (cnt + tr - 1) // tr
    tile_end = jnp.cumsum(tiles_e)
    tile_start = tile_end - tiles_e

    def lookup(table, idx, keys=e_ids):
        return jnp.sum(jnp.where(idx[..., None] == keys, table, 0), axis=-1)

    pos = lookup(tile_start * tr, eidx) + rank
    n_used = tile_end[-1]
    ti = jnp.arange(n_tiles, dtype=I32)
    ti_c = jnp.minimum(ti, n_used - 1)
    gid = jnp.sum((ti_c[:, None] >= tile_end[None, :]).astype(I32), axis=1)
    left = lookup(cnt, gid) - (ti - lookup(tile_start, gid)) * tr
    nvalid = jnp.where(ti < n_used, jnp.clip(left, 0, tr), 0)
    changed = jnp.concatenate([jnp.ones((1,), I32), (gid[1:] != gid[:-1]).astype(I32)])
    wslot = (jnp.cumsum(changed) - 1) % 2
    after = lookup(tile_end, gid)
    nxt = jnp.where(after < n_used, lookup(gid, jnp.minimum(after, n_tiles - 1), ti), -1)
    as_i32 = lambda a: a.astype(I32)
    return (as_i32(pos), as_i32(gid), as_i32(nvalid), n_used.reshape(1).astype(I32), as_i32(wslot), as_i32(nxt))


def kernel(x, c, ctx, c_ctx, w_ada, b_ada, g_norm1, g_norm2, w_in, g_q, g_k, w_fourier, w_dw, b_dw, g_conv_ln, b_conv_ln, w_pw, b_pw, g_sgu, w_spatial, b_spatial, w_out, w_router, b_router, w_gate_e, w_up_e, w_down_e, w_gate_s, w_up_s, w_down_s):
    B, L, D = x.shape
    Lc = ctx.shape[1]
    depth = w_ada.shape[0]
    assert B + 1 <= COND_ROWS and L % Lc == 0 and L % GRID_W == 0
    d_fourier = w_fourier.shape[1] * w_fourier.shape[2]
    d_conv = w_pw.shape[1]
    d_sgu = g_sgu.shape[1]
    d_attn = D - d_fourier - d_conv - d_sgu
    kv_w = d_attn // GQA_GROUP
    v_end = d_attn + 2 * kv_w
    f_end = v_end + d_fourier
    c_end = f_end + 2 * d_conv
    d_in = c_end + 2 * d_sgu
    assert w_in.shape[2] == d_in
    n_exp = w_router.shape[2]
    n_lat = B * L
    T_all = n_lat + B * Lc
    cb = D // 8
    exp_tr = 256
    tok_tile = _pick(L, 256, 128)
    comb_tile = 128

    assert n_exp // EXPERTS_PER_GROUP == 8 and (B * Lc) % _pick(L, 512) == 0
    assert all(c % cb == 0 for c in (v_end, f_end, c_end, d_in))
    fh = w_gate_s.shape[2]

    cond = jnp.concatenate([c, c_ctx[None, :], jnp.zeros((COND_ROWS - B - 1, D), F32)], axis=0)
    mod = modulation(cond, w_ada, b_ada)
    mod = mod.reshape(depth, COND_ROWS, N_MOD, D).transpose(0, 2, 1, 3)

    xa = (x.reshape(n_lat, D), ctx.reshape(B * Lc, D))

    for l in range(depth):
        last = l == depth - 1
        n_rows = n_lat if last else T_all
        sh1, sc1, ga1, sh2, sc2, ga2 = (mod[l, m] for m in range(N_MOD))

        w_out_bf = cast_cols(w_out, l, 0, D, cb)
        w_pw_bf = cast_cols(w_pw, l, 0, d_conv, _pick(d_conv, 512, 128))
        wgs_bf = cast_cols(w_gate_s, l, 0, fh, fh)
        wus_bf = cast_cols(w_up_s, l, 0, fh, fh)
        wds_bf = cast_cols(w_down_s, l, 0, D, cb)

        h = norm_mod(xa, g_norm1[l], sh1, sc1, L=L, B=B, T=T_all)
        z_qkv = matmul(h, w_in, l, 0, v_end, T_all, cb)
        z_f = matmul(h, w_in, l, v_end, d_fourier, n_rows, cb)
        z_c = matmul(h, w_in, l, f_end, 2 * d_conv, n_rows, cb)
        z_s = matmul(h, w_in, l, c_end, 2 * d_sgu, n_rows, cb)
        qn, kn = qk_norm_rope(z_qkv, g_q[l], g_k[l], L=L, B=B, d_attn=d_attn, kv_w=kv_w)
        y_a = attention(qn, kn, z_qkv, q_row0=0, Lq=L, segs=[(0, L), (n_lat, Lc)],
                        B=B, d_attn=d_attn, kv_w=kv_w)
        y_f = fourier(z_f, w_fourier[l], row0=0, Ls=L, B=B)
        if not last:
            y_a = (y_a, attention(qn, kn, z_qkv, q_row0=n_lat, Lq=Lc, segs=[(n_lat, Lc)],
                                  B=B, d_attn=d_attn, kv_w=kv_w))
            y_f = (y_f, fourier(z_f, w_fourier[l], row0=n_lat, Ls=Lc, B=B))
        y_c = conv_module(z_c, w_dw[l], b_dw[l], g_conv_ln[l], b_conv_ln[l], w_pw_bf, b_pw[l],
                          L=L, Lc=Lc, B=B, n_rows=n_rows)
        y_s = sgu_mixer(z_s, g_sgu[l], w_spatial[l], b_spatial[l], n_rows=n_rows)
        xa = out_proj([y_a, y_f, y_c, y_s], w_out_bf, xa, ga1, L=L, B=B, n_rows=n_rows)

        h2p, eidx, wsel, rank, cnt = router(xa, g_norm2[l], sh2, sc2, w_router[l], b_router[l],
                                            L=L, B=B, n_rows=n_rows)
        n_tiles = -(-n_rows * TOP_K // exp_tr) + n_exp
        pos, gid, nvalid, nused, wslot, nxt = _dispatch_plan(
            cnt[:, 0].astype(I32), eidx, rank, tr=exp_tr, n_tiles=n_tiles)
        xs = dispatch(_tile_major(pos, tok_tile), h2p, n_tiles * exp_tr, tok_tile)
        ys = experts(xs, gid, nvalid, nused, wslot, nxt, w_gate_e, w_up_e, w_down_e, l, tr=exp_tr)
        xa = combine(_tile_major(pos, comb_tile), xa, h2p, _tile_major(wsel, comb_tile), ga2,
                     wgs_bf, wus_bf, wds_bf, ys, L=L, B=B, n_rows=n_rows, tm=comb_tile)

    return xa.reshape(B, L, D)
```
